```python
import jax
import jax.numpy as jnp
from jax import lax
import numpy as np

D_MODEL = 1024
BATCH = 16
SEQ = 2048
DEPTH = 2

GRID_W = 64
CTX_LEN = 256
N_MIXERS = 4
GROUP_WIDTH = D_MODEL // N_MIXERS
MIX_WIDTH = N_MIXERS * GROUP_WIDTH
HEAD_DIM = 64
HGRN_HEADS = GROUP_WIDTH // HEAD_DIM
HGRN_CHUNK = 16
CONV_WIDTH = 3
RET_HEADS = GROUP_WIDTH // HEAD_DIM
RET_CHUNK = 128
ATT_Q_HEADS = GROUP_WIDTH // HEAD_DIM
ATT_KV_HEADS = 2
ATT_GROUP = ATT_Q_HEADS // ATT_KV_HEADS
KV_WIDTH = ATT_KV_HEADS * HEAD_DIM
ATT_BLOCK_Q = 128
ROPE_THETA = 10000.0
N_EXPERT_GROUPS = 4
EXPERTS_PER_GROUP = 4
N_EXPERTS = N_EXPERT_GROUPS * EXPERTS_PER_GROUP
TOP_K_IN_GROUP = 2
D_EXPERT = D_MODEL // 2
NORM_EPS = 1e-6
SPLIT_SIZES = (GROUP_WIDTH,) * 12 + (GROUP_WIDTH, KV_WIDTH, KV_WIDTH)
IN_WIDTH = 13 * GROUP_WIDTH + 2 * KV_WIDTH

kernel_name = 'hybrid_head_group_diffusion_block'


def rms_norm(x, gain):
    xf = x.astype(jnp.float32)
    y = xf * lax.rsqrt(jnp.mean(xf * xf, axis=-1, keepdims=True) + NORM_EPS)
    return (y * gain.astype(jnp.float32)).astype(x.dtype)


def head_rms_norm(a, gain):
    b, t, w = a.shape
    af = a.astype(jnp.float32).reshape(b, t, w // HEAD_DIM, HEAD_DIM)
    af = af * lax.rsqrt(jnp.mean(af * af, axis=-1, keepdims=True) + NORM_EPS)
    return (af.reshape(b, t, w) * gain.astype(jnp.float32)).astype(a.dtype)


def modulate(h, shift, scale):
    return h * (1.0 + scale) + shift


def split_heads(a, n_heads):
    b, t, _ = a.shape
    return a.reshape(b, t, n_heads, -1).transpose(0, 2, 1, 3)


def merge_heads(a):
    b, h, t, d = a.shape
    return a.transpose(0, 2, 1, 3).reshape(b, t, h * d)


def split_columns(p):
    parts, start = [], 0
    for width in SPLIT_SIZES:
        parts.append(p[..., start:start + width])
        start += width
    return parts


def flip_time(a, direction):
    return a if direction == 0 else jnp.flip(a, axis=2)


def axial_rope_tables(rows):
    t = jnp.arange(rows * GRID_W)
    row = (t // GRID_W).astype(jnp.float32)
    col = (t % GRID_W).astype(jnp.float32)
    n_freq = HEAD_DIM // 4
    inv_freq = ROPE_THETA ** (-jnp.arange(n_freq, dtype=jnp.float32) / n_freq)
    ang = jnp.concatenate([row[:, None] * inv_freq, col[:, None] * inv_freq], axis=-1)
    return jnp.cos(ang), jnp.sin(ang)


def apply_rope(a, cos, sin):
    af = a.astype(jnp.float32).reshape(a.shape[:-1] + (HEAD_DIM // 2, 2))
    x1, x2 = af[..., 0], af[..., 1]
    out = jnp.stack([x1 * cos - x2 * sin, x1 * sin + x2 * cos], axis=-1)
    return out.reshape(a.shape).astype(a.dtype)


def gla_chunk(q, k, v, log_f, s0):
    out_dtype = v.dtype
    q, k, v, log_f = (a.astype(jnp.float32) for a in (q, k, v, log_f))
    b, h, t, dk = q.shape
    dv = v.shape[-1]
    c = HGRN_CHUNK
    n = t // c
    q, k, log_f = (a.reshape(b, h, n, c, dk) for a in (q, k, log_f))
    v = v.reshape(b, h, n, c, dv)
    cum = jnp.cumsum(log_f, axis=3)
    causal = jnp.tril(jnp.ones((c, c), dtype=bool))
    rel = cum[:, :, :, :, None, :] - cum[:, :, :, None, :, :]
    decay = jnp.exp(jnp.where(causal[:, :, None], rel, -jnp.inf))
    scores = jnp.einsum('bhntd,bhnsd,bhntsd->bhnts', q, k, decay)
    o_intra = jnp.einsum('bhnts,bhnse->bhnte', scores, v)
    last = cum[:, :, :, -1]
    chunk_kv = jnp.einsum('bhnsd,bhnse->bhnde', k * jnp.exp(last[:, :, :, None] - cum), v)

    def step(state, inp):
        kv_c, last_c = inp
        return state * jnp.exp(last_c)[..., None] + kv_c, state

    s_final, s_in = lax.scan(step, s0.astype(jnp.float32),
                             (jnp.moveaxis(chunk_kv, 2, 0), jnp.moveaxis(last, 2, 0)))
    s_in = jnp.moveaxis(s_in, 0, 2)
    o_inter = jnp.einsum('bhntd,bhnde->bhnte', q * jnp.exp(cum), s_in)
    return (o_intra + o_inter).reshape(b, h, t, dv).astype(out_dtype), s_final


def gla_final_state(k, v, log_f):
    k, v, log_f = (a.astype(jnp.float32) for a in (k, v, log_f))
    cum = jnp.cumsum(log_f, axis=2)
    return jnp.einsum('bhsd,bhse->bhde', k * jnp.exp(cum[:, :, -1:] - cum), v)


def retention_chunk(q, k, v, log_gamma, s0):
    out_dtype = v.dtype
    q, k, v = (a.astype(jnp.float32) for a in (q, k, v))
    b, h, t, dk = q.shape
    dv = v.shape[-1]
    c = RET_CHUNK
    n = t // c
    q, k = (a.reshape(b, h, n, c, dk) for a in (q, k))
    v = v.reshape(b, h, n, c, dv)
    lg = log_gamma.astype(jnp.float32)[:, None]
    pos = jnp.arange(c, dtype=jnp.float32)
    rel = pos[:, None] - pos[None, :]
    decay = jnp.where(rel >= 0, jnp.exp(lg[:, :, None] * jnp.maximum(rel, 0.0)), 0.0)
    scores = jnp.einsum('bhntd,bhnsd->bhnts', q, k) * decay[None, :, None]
    o_intra = jnp.einsum('bhnts,bhnse->bhnte', scores, v)
    k_dec = k * jnp.exp(lg * (c - 1 - pos))[None, :, None, :, None]
    chunk_kv = jnp.einsum('bhnsd,bhnse->bhnde', k_dec, v)
    chunk_decay = jnp.exp(lg[:, 0] * c)[None, :, None, None]

    def step(state, kv_c):
        return state * chunk_decay + kv_c, state

    s_final, s_in = lax.scan(step, s0.astype(jnp.float32), jnp.moveaxis(chunk_kv, 2, 0))
    s_in = jnp.moveaxis(s_in, 0, 2)
    o_inter = jnp.einsum('bhntd,bhnde->bhnte', q * jnp.exp(lg * (pos + 1.0))[None, :, None, :, None], s_in)
    return (o_intra + o_inter).reshape(b, h, t, dv).astype(out_dtype), s_final


def retention_final_state(k, v, log_gamma):
    k, v = k.astype(jnp.float32), v.astype(jnp.float32)
    length = k.shape[2]
    lg = log_gamma.astype(jnp.float32)[:, None]
    w = jnp.exp(lg * (length - 1 - jnp.arange(length, dtype=jnp.float32)))
    return jnp.einsum('bhsd,bhse->bhde', k * w[None, :, :, None], v)


def hgrn2_mixer(px, pc, lower_bound, norm_gain, with_ctx_out):
    def key_and_log_decay(z, direction):
        lb = lower_bound[direction]
        f = lb + (1.0 - lb) * jax.nn.sigmoid(z.astype(jnp.float32))
        return split_heads(1.0 - f, HGRN_HEADS), split_heads(jnp.log(f), HGRN_HEADS)

    qx, ix = split_heads(px[0], HGRN_HEADS), split_heads(px[1], HGRN_HEADS)
    ic = split_heads(pc[1], HGRN_HEADS)
    zero = jnp.zeros((ic.shape[0], HGRN_HEADS, HEAD_DIM, HEAD_DIM), jnp.float32)
    ox, oc = 0.0, 0.0
    for direction in range(2):
        kx, lfx = key_and_log_decay(px[2 + direction], direction)
        kc, lfc = key_and_log_decay(pc[2 + direction], direction)
        if with_ctx_out:
            qc = split_heads(pc[0], HGRN_HEADS)
            oc_d, s_ctx = gla_chunk(flip_time(qc, direction), flip_time(kc, direction),
                                    flip_time(ic, direction), flip_time(lfc, direction), zero)
            oc = oc + flip_time(oc_d, direction)
        else:
            s_ctx = gla_final_state(flip_time(kc, direction), flip_time(ic, direction), flip_time(lfc, direction))
        ox_d, _ = gla_chunk(flip_time(qx, direction), flip_time(kx, direction),
                            flip_time(ix, direction), flip_time(lfx, direction), s_ctx)
        ox = ox + flip_time(ox_d, direction)
    out_x = head_rms_norm(merge_heads(ox), norm_gain) * jax.nn.silu(px[4])
    out_c = head_rms_norm(merge_heads(oc), norm_gain) * jax.nn.silu(pc[4]) if with_ctx_out else None
    return out_x, out_c


def depthwise_conv_centred(u, w):
    return lax.conv_general_dilated(
        u, w[:, None, :].astype(u.dtype), window_strides=(1,),
        padding=[(CONV_WIDTH // 2, CONV_WIDTH // 2)],
        dimension_numbers=('NWC', 'WIO', 'NWC'), feature_group_count=u.shape[-1])


def short_conv_mixer(px, pc, conv_w, with_ctx_out):
    def run(p):
        b_gate, c_gate, h = p
        return b_gate * depthwise_conv_centred(c_gate * h, conv_w)
    return run(px), (run(pc) if with_ctx_out else None)


def retention_mixer(px, pc, log_gamma, norm_gain, with_ctx_out):
    scale = HEAD_DIM ** -0.5

    def qkv(p):
        return split_heads(p[0], RET_HEADS) * scale, split_heads(p[1], RET_HEADS), split_heads(p[2], RET_HEADS)

    qx, kx, vx = qkv(px)
    qc, kc, vc = qkv(pc)
    zero = jnp.zeros((kc.shape[0], RET_HEADS, HEAD_DIM, HEAD_DIM), jnp.float32)
    ox, oc = 0.0, 0.0
    for direction in range(2):
        if with_ctx_out:
            oc_d, s_ctx = retention_chunk(flip_time(qc, direction), flip_time(kc, direction),
                                          flip_time(vc, direction), log_gamma[direction], zero)
            oc = oc + flip_time(oc_d, direction)
        else:
            s_ctx = retention_final_state(flip_time(kc, direction), flip_time(vc, direction), log_gamma[direction])
        ox_d, _ = retention_chunk(flip_time(qx, direction), flip_time(kx, direction),
                                  flip_time(vx, direction), log_gamma[direction], s_ctx)
        ox = ox + flip_time(ox_d, direction)
    out_x = head_rms_norm(merge_heads(ox), norm_gain) * jax.nn.silu(px[3])
    out_c = head_rms_norm(merge_heads(oc), norm_gain) * jax.nn.silu(pc[3]) if with_ctx_out else None
    return out_x, out_c


def gqa_softmax(q, k, v):
    s = jnp.einsum('bkgqd,bknd->bkgqn', q, k, preferred_element_type=jnp.float32) * (HEAD_DIM ** -0.5)
    p = jax.nn.softmax(s, axis=-1)
    return jnp.einsum('bkgqn,bknd->bkgqd', p.astype(v.dtype), v)


def blocked_attention(q, k, v):
    b, hq, t, d = q.shape
    nb = t // ATT_BLOCK_Q
    qb = q.reshape(b, ATT_KV_HEADS, ATT_GROUP, nb, ATT_BLOCK_Q, d).transpose(3, 0, 1, 2, 4, 5)
    ob = lax.map(lambda qi: gqa_softmax(qi, k, v), qb)
    return ob.transpose(1, 2, 3, 0, 4, 5).reshape(b, hq, t, d)


def attention_mixer(px, pc, q_norm, k_norm, rope_cos, rope_sin, with_ctx_out):
    qx = apply_rope(rms_norm(split_heads(px[0], ATT_Q_HEADS), q_norm), rope_cos, rope_sin)
    kx = apply_rope(rms_norm(split_heads(px[1], ATT_KV_HEADS), k_norm), rope_cos, rope_sin)
    vx = split_heads(px[2], ATT_KV_HEADS)
    kc = rms_norm(split_heads(pc[1], ATT_KV_HEADS), k_norm)
    vc = split_heads(pc[2], ATT_KV_HEADS)
    keys = jnp.concatenate([kc, kx], axis=2)
    vals = jnp.concatenate([vc, vx], axis=2)
    out_x = merge_heads(blocked_attention(qx, keys, vals))
    out_c = None
    if with_ctx_out:
        qc = rms_norm(split_heads(pc[0], ATT_Q_HEADS), q_norm)
        b, _, length, d = qc.shape
        oc = gqa_softmax(qc.reshape(b, ATT_KV_HEADS, ATT_GROUP, length, d), kc, vc)
        out_c = merge_heads(oc.reshape(b, ATT_Q_HEADS, length, d))
    return out_x, out_c


def hierarchical_moe(h, w_rg, b_rg, w_re, b_re, w_gate, w_up, w_down):
    n = h.shape[0]
    group_prob = jax.nn.softmax((h @ w_rg + b_rg).astype(jnp.float32), axis=-1)
    group_p, group_idx = lax.top_k(group_prob, 1)
    expert_logits = (h @ w_re + b_re).astype(jnp.float32).reshape(n, N_EXPERT_GROUPS, EXPERTS_PER_GROUP)
    in_group = jnp.take_along_axis(expert_logits, group_idx[:, :, None], axis=1)[:, 0]
    expert_p, expert_idx = lax.top_k(jax.nn.softmax(in_group, axis=-1), TOP_K_IN_GROUP)
    expert_p = expert_p / jnp.sum(expert_p, axis=-1, keepdims=True)
    weights = group_p * expert_p
    expert_id = group_idx * EXPERTS_PER_GROUP + expert_idx
    combine = jnp.sum(jax.nn.one_hot(expert_id, N_EXPERTS, dtype=jnp.float32) * weights[..., None], axis=1)
    y = jnp.zeros(h.shape, jnp.float32)
    for e in range(N_EXPERTS):
        hidden = jax.nn.silu(h @ w_gate[e]) * (h @ w_up[e])
        y = y + combine[:, e:e + 1] * (hidden @ w_down[e]).astype(jnp.float32)
    return y.astype(h.dtype)


def setup_inputs(seed: int = 0) -> dict:
    key = jax.random.key(seed)
    ks = jax.random.split(key, 24)

    def nrm(k, shape, scale):
        return jax.random.normal(k, shape, jnp.float32) * scale

    ret_base_logit = jnp.log(2.0 ** (5.0 + jnp.arange(RET_HEADS, dtype=jnp.float32)) - 1.0)
    return {
        'x': nrm(ks[0], (BATCH, SEQ, D_MODEL), 1.0),
        'c': nrm(ks[1], (BATCH, D_MODEL), 1.0),
        'ctx': nrm(ks[2], (BATCH, CTX_LEN, D_MODEL), 1.0),
        'c_ctx': nrm(ks[3], (D_MODEL,), 1.0),
        'ada_w': nrm(ks[4], (DEPTH, D_MODEL, 6 * D_MODEL), 0.5 * D_MODEL ** -0.5),
        'ada_b': nrm(ks[5], (DEPTH, 6 * D_MODEL), 0.02),
        'norm_mix': 1.0 + nrm(ks[6], (DEPTH, D_MODEL), 0.02),
        'norm_ffn': 1.0 + nrm(ks[7], (DEPTH, D_MODEL), 0.02),
        'w_in': nrm(ks[8], (DEPTH, D_MODEL, IN_WIDTH), D_MODEL ** -0.5),
        'hgrn_lb_logits': nrm(ks[9], (DEPTH, 2, GROUP_WIDTH), 0.5),
        'hgrn_norm': 1.0 + nrm(ks[10], (DEPTH, GROUP_WIDTH), 0.02),
        'conv_w': nrm(ks[11], (DEPTH, CONV_WIDTH, GROUP_WIDTH), CONV_WIDTH ** -0.5),
        'ret_decay_logit': ret_base_logit + nrm(ks[12], (DEPTH, 2, RET_HEADS), 0.1),
        'ret_norm': 1.0 + nrm(ks[13], (DEPTH, GROUP_WIDTH), 0.02),
        'q_norm': 1.0 + nrm(ks[14], (DEPTH, HEAD_DIM), 0.02),
        'k_norm': 1.0 + nrm(ks[15], (DEPTH, HEAD_DIM), 0.02),
        'w_out': nrm(ks[16], (DEPTH, MIX_WIDTH, D_MODEL), MIX_WIDTH ** -0.5),
        'router_group_w': nrm(ks[17], (DEPTH, D_MODEL, N_EXPERT_GROUPS), D_MODEL ** -0.5),
        'router_group_b': nrm(ks[18], (DEPTH, N_EXPERT_GROUPS), 0.01),
        'router_expert_w': nrm(ks[19], (DEPTH, D_MODEL, N_EXPERTS), D_MODEL ** -0.5),
        'router_expert_b': nrm(ks[20], (DEPTH, N_EXPERTS), 0.01),
        'expert_w_gate': nrm(ks[21], (DEPTH, N_EXPERTS, D_MODEL, D_EXPERT), D_MODEL ** -0.5),
        'expert_w_up': nrm(ks[22], (DEPTH, N_EXPERTS, D_MODEL, D_EXPERT), D_MODEL ** -0.5),
        'expert_w_down': nrm(ks[23], (DEPTH, N_EXPERTS, D_EXPERT, D_MODEL), D_EXPERT ** -0.5),
    }


def reference(x, c, ctx, c_ctx, ada_w, ada_b, norm_mix, norm_ffn, w_in, hgrn_lb_logits, hgrn_norm,
              conv_w, ret_decay_logit, ret_norm, q_norm, k_norm, w_out, router_group_w, router_group_b,
              router_expert_w, router_expert_b, expert_w_gate, expert_w_up, expert_w_down):
    b, t, d = x.shape
    rows = t // GRID_W
    rope_cos, rope_sin = axial_rope_tables(rows)
    lb_w = jax.nn.softmax(hgrn_lb_logits.astype(jnp.float32), axis=0)
    lower_bounds = jnp.cumsum(lb_w, axis=0) - lb_w[0]
    silu_c = jax.nn.silu(c)
    silu_cc = jax.nn.silu(c_ctx)
    for layer in range(DEPTH):
        with_ctx_out = layer < DEPTH - 1
        mod_x = jnp.split(silu_c @ ada_w[layer] + ada_b[layer], 6, axis=-1)
        mod_c = jnp.split(silu_cc @ ada_w[layer] + ada_b[layer], 6, axis=-1)

        hx = modulate(rms_norm(x, norm_mix[layer]), mod_x[0][:, None], mod_x[1][:, None])
        hc = modulate(rms_norm(ctx, norm_mix[layer]), mod_c[0], mod_c[1])
        px = split_columns(hx @ w_in[layer])
        pc = split_columns(hc @ w_in[layer])
        log_gamma = jax.nn.log_sigmoid(ret_decay_logit[layer].astype(jnp.float32))
        ax, ac = hgrn2_mixer(px[0:5], pc[0:5], lower_bounds[layer], hgrn_norm[layer], with_ctx_out)
        bx, bc = short_conv_mixer(px[5:8], pc[5:8], conv_w[layer], with_ctx_out)
        rx, rc = retention_mixer(px[8:12], pc[8:12], log_gamma, ret_norm[layer], with_ctx_out)
        gx, gc = attention_mixer(px[12:15], pc[12:15], q_norm[layer], k_norm[layer], rope_cos, rope_sin,
                                 with_ctx_out)
        x = x + mod_x[2][:, None] * (jnp.concatenate([ax, bx, rx, gx], axis=-1) @ w_out[layer])

        hx2 = modulate(rms_norm(x, norm_ffn[layer]), mod_x[3][:, None], mod_x[4][:, None])
        moe_args = (router_group_w[layer], router_group_b[layer], router_expert_w[layer],
                    router_expert_b[layer], expert_w_gate[layer], expert_w_up[layer], expert_w_down[layer])
        if with_ctx_out:
            ctx = ctx + mod_c[2] * (jnp.concatenate([ac, bc, rc, gc], axis=-1) @ w_out[layer])
            hc2 = modulate(rms_norm(ctx, norm_ffn[layer]), mod_c[3], mod_c[4])
            y = hierarchical_moe(jnp.concatenate([hx2.reshape(-1, d), hc2.reshape(-1, d)], axis=0), *moe_args)
            x = x + mod_x[5][:, None] * y[:b * t].reshape(b, t, d)
            ctx = ctx + mod_c[5] * y[b * t:].reshape(ctx.shape)
        else:
            y = hierarchical_moe(hx2.reshape(-1, d), *moe_args)
            x = x + mod_x[5][:, None] * y.reshape(b, t, d)
    return x
```

```python
import functools

import jax
import jax.numpy as jnp
from jax import lax
from jax.experimental import pallas as pl
from jax.experimental.pallas import tpu as pltpu

F32 = jnp.float32
BF16 = jnp.bfloat16
HIGHEST = lax.Precision.HIGHEST

HEAD_DIM = 64
GROUP_W = 256
N_HEADS = GROUP_W // HEAD_DIM
KV_W = 128
GRID_W = 64
ROPE_THETA = 10000.0
EXPERTS_PER_GROUP = 4
NORM_EPS = 1e-6
ROW_BLK = 256
SUB_BLK = 16
LANES = 128
MOD_ROWS = 24
VMEM_LIMIT_BYTES = 56 * 1024 * 1024
PA_W, PB_W, PC_W, PD_W = 5 * GROUP_W, 3 * GROUP_W, 4 * GROUP_W, GROUP_W + 2 * KV_W


def _params(*semantics):
    return pltpu.CompilerParams(dimension_semantics=semantics, vmem_limit_bytes=VMEM_LIMIT_BYTES)


def _dot(a, b):
    return jnp.dot(a, b, preferred_element_type=F32)


def _dot_nt(a, b):
    return lax.dot_general(a, b, (((1,), (1,)), ((), ())), preferred_element_type=F32)


def _dot_tn(a, b):
    return lax.dot_general(a, b, (((0,), (0,)), ((), ())), preferred_element_type=F32)


def _sigmoid(x):
    return 1.0 / (1.0 + jnp.exp(-x))


def _silu(x):
    return x * _sigmoid(x)


def _split_bf16(x, terms):
    parts = []
    rem = x
    for i in range(terms):
        p = rem.astype(BF16)
        parts.append(p)
        if i + 1 < terms:
            rem = rem - p.astype(F32)
    return parts


def _head_ones(width):
    r = lax.broadcasted_iota(jnp.int32, (width, width), 0) // HEAD_DIM
    c = lax.broadcasted_iota(jnp.int32, (width, width), 1) // HEAD_DIM
    return r == c


def _head_mean_sq(x, ones_bf16):
    hi, lo = _split_bf16(x * x, 2)
    return (_dot(hi, ones_bf16) + _dot(lo, ones_bf16)) * (1.0 / HEAD_DIM)


def _head_rms_norm(x, gain, ones_bf16):
    return x * lax.rsqrt(_head_mean_sq(x, ones_bf16) + NORM_EPS) * gain


def _norm_modulate(x, gain, mod, idx):
    d = x.shape[-1]
    ms = jnp.mean(x * x, axis=-1, keepdims=True)
    y = x * lax.rsqrt(ms + NORM_EPS) * gain
    shift = mod[:, idx * d:(idx + 1) * d]
    scale = mod[:, (idx + 1) * d:(idx + 2) * d]
    return y * (1.0 + scale) + shift


def _mod_kernel(cv_ref, w_ref, b_ref, o_ref):
    s = _silu(cv_ref[...])
    o_ref[0] = jnp.dot(s, w_ref[0], precision=HIGHEST, preferred_element_type=F32) + b_ref[0]


def _modulation(cv, ada_w, ada_b):
    depth, d, n = ada_w.shape
    tn = n // 4
    return pl.pallas_call(
        _mod_kernel,
        grid=(depth, n // tn),
        in_specs=[pl.BlockSpec((MOD_ROWS, d), lambda l, j: (0, 0)),
                  pl.BlockSpec((1, d, tn), lambda l, j: (l, 0, j)),
                  pl.BlockSpec((1, 1, tn), lambda l, j: (l, 0, j))],
        out_specs=pl.BlockSpec((1, MOD_ROWS, tn), lambda l, j: (l, 0, j)),
        out_shape=jax.ShapeDtypeStruct((depth, MOD_ROWS, n), F32),
        compiler_params=_params("parallel", "parallel"),
        name="adaln_mod",
    )(cv, ada_w, ada_b.reshape(depth, 1, n))


def _inproj_kernel(x_ref, mod_ref, gain_ref, w_ref, pa_ref, pb_ref, pc_ref, pd_ref):
    h = _norm_modulate(x_ref[...], gain_ref[...], mod_ref[0], 0)
    p = _dot(h.astype(BF16), w_ref[...])
    pa_ref[...] = p[:, 0:PA_W]
    pb_ref[...] = p[:, PA_W:PA_W + PB_W]
    pc_ref[...] = p[:, PA_W + PB_W:PA_W + PB_W + PC_W]
    pd_ref[...] = p[:, PA_W + PB_W + PC_W:]


def _inproj(tokens, mod, gain, w_bf16, blocks_per_batch, ctx_row):
    n, d = tokens.shape
    nblk = n // ROW_BLK

    def mod_idx(i):
        return (jnp.where(i % blocks_per_batch == 0, ctx_row, i // blocks_per_batch), 0, 0)

    widths = (PA_W, PB_W, PC_W, PD_W)
    return pl.pallas_call(
        _inproj_kernel,
        grid=(nblk,),
        in_specs=[pl.BlockSpec((ROW_BLK, d), lambda i: (i, 0)),
                  pl.BlockSpec((1, 1, mod.shape[-1]), mod_idx),
                  pl.BlockSpec((1, d), lambda i: (0, 0)),
                  pl.BlockSpec(w_bf16.shape, lambda i: (0, 0))],
        out_specs=[pl.BlockSpec((ROW_BLK, w), lambda i: (i, 0)) for w in widths],
        out_shape=[jax.ShapeDtypeStruct((n, w), F32) for w in widths],
        compiler_params=_params("parallel"),
        name="in_proj",
    )(tokens, mod, gain.reshape(1, d), w_bf16)


def _hgrn_chunk(blk, lb_row, z_col, anti, s_ref, c_scr, q_scr, k_scr, v_scr, od_scr, consts):
    tril, triu, ones_bf16, bmask, hmasks = consts
    w = GROUP_W
    c_len = blk.shape[0]
    q = blk[:, 0:w]
    v = blk[:, w:2 * w]
    z = blk[:, z_col * w:(z_col + 1) * w]
    f = lb_row + (1.0 - lb_row) * _sigmoid(z)
    g = jnp.log(f)
    k = 1.0 - f
    tri = triu if anti else tril
    c = sum(_dot(tri, part) for part in _split_bf16(g, 3))
    tot = c[0:1] if anti else c[c_len - 1:c_len]

    s_t = s_ref[...]
    o = _dot_nt((q * jnp.exp(c)).astype(BF16), s_t.astype(BF16))
    k_end = (k * jnp.exp(tot - c)).astype(BF16)
    v_bf = v.astype(BF16)
    s_ref[...] = s_t * jnp.exp(tot) + jnp.where(bmask, _dot_tn(v_bf, k_end), 0.0)

    v_heads = [jnp.where(hm, v_bf, jnp.zeros_like(v_bf)) for hm in hmasks]
    m = c_len // 2
    while m >= SUB_BLK:
        nb = c_len // (2 * m)
        refs = [jnp.broadcast_to(c[b * 2 * m + (m if anti else m - 1):b * 2 * m + (m if anti else m - 1) + 1],
                                 (2 * m, w)) for b in range(nb)]
        ref = refs[0] if nb == 1 else jnp.concatenate(refs, axis=0)
        e = jnp.exp(-jnp.abs(c - ref))
        qt = (q * e).astype(BF16)
        kt = (k * e).astype(BF16)
        pieces = []
        for b in range(nb):
            first, second = b * 2 * m, b * 2 * m + m
            q0, k0 = (first, second) if anti else (second, first)
            kb = kt[k0:k0 + m]
            k_exp = jnp.concatenate([jnp.where(hm, kb, jnp.zeros_like(kb)) for hm in hmasks], axis=0)
            v_exp = jnp.concatenate([vh[k0:k0 + m] for vh in v_heads], axis=0)
            a = _dot_nt(qt[q0:q0 + m], k_exp)
            ob = _dot(a.astype(BF16), v_exp)
            zero = jnp.zeros((m, w), F32)
            pieces += [ob, zero] if anti else [zero, ob]
        o = o + jnp.concatenate(pieces, axis=0)
        m //= 2

    c_scr[...] = c
    q_scr[...] = q
    k_scr[...] = k
    v_scr[...] = v
    row = lax.broadcasted_iota(jnp.int32, (SUB_BLK, w), 0)

    def sub(i, carry):
        r = pl.multiple_of(i * SUB_BLK, SUB_BLK)
        cs = c_scr[pl.ds(r, SUB_BLK), :]
        qs = q_scr[pl.ds(r, SUB_BLK), :]
        ks = k_scr[pl.ds(r, SUB_BLK), :]
        vs = v_scr[pl.ds(r, SUB_BLK), :]
        prods = []
        for s in range(SUB_BLK):
            valid = (row <= s) if anti else (row >= s)
            dec = jnp.exp(jnp.where(valid, cs - cs[s:s + 1], -jnp.inf))
            prods.append((qs * dec * ks[s:s + 1]).astype(BF16))
        scores = _dot(jnp.concatenate(prods, axis=0), ones_bf16)
        od = scores[0:SUB_BLK] * vs[0:1]
        for s in range(1, SUB_BLK):
            od = od + scores[s * SUB_BLK:(s + 1) * SUB_BLK] * vs[s:s + 1]
        od_scr[pl.ds(r, SUB_BLK), :] = od
        return carry

    lax.fori_loop(0, c_len // SUB_BLK, sub, 0)
    return o


def _hgrn_kernel(p_ref, lbl_ref, gain_ref, o_ref, acc, s_f, s_b, c_scr, q_scr, k_scr, v_scr, od_scr,
                 *, layer, out_off):
    w = GROUP_W
    c_len = ROW_BLK
    rows = p_ref.shape[0]
    nchunk = rows // c_len
    depth = lbl_ref.shape[0]

    logits = [lbl_ref[l] for l in range(depth)]
    mx = functools.reduce(jnp.maximum, logits)
    exps = [jnp.exp(l - mx) for l in logits]
    lb = sum(exps[1:layer + 1], jnp.zeros_like(mx)) / sum(exps)

    ri = lax.broadcasted_iota(jnp.int32, (c_len, c_len), 0)
    ci = lax.broadcasted_iota(jnp.int32, (c_len, c_len), 1)
    tril = (ci <= ri).astype(BF16)
    triu = (ci >= ri).astype(BF16)
    bmask = _head_ones(w)
    ones_bf16 = bmask.astype(BF16)
    lane_head = lax.broadcasted_iota(jnp.int32, (1, w), 1) // HEAD_DIM
    hmasks = [lane_head == h for h in range(N_HEADS)]
    consts = (tril, triu, ones_bf16, bmask, hmasks)

    acc[...] = jnp.zeros_like(acc)
    s_f[...] = jnp.zeros_like(s_f)
    s_b[...] = jnp.zeros_like(s_b)

    def step(j, carry):
        rf = pl.multiple_of(j * c_len, c_len)
        rb = pl.multiple_of(jnp.where(j == 0, 0, nchunk - j) * c_len, c_len)
        of = _hgrn_chunk(p_ref[pl.ds(rf, c_len), :], lb[0:1], 2, False, s_f,
                         c_scr, q_scr, k_scr, v_scr, od_scr, consts)
        acc[pl.ds(rf, c_len), :] += of + od_scr[...]
        ob = _hgrn_chunk(p_ref[pl.ds(rb, c_len), :], lb[1:2], 3, True, s_b,
                         c_scr, q_scr, k_scr, v_scr, od_scr, consts)
        acc[pl.ds(rb, c_len), :] += ob + od_scr[...]
        return carry

    lax.fori_loop(0, nchunk, step, 0)

    o = acc[out_off:, :]
    gate = p_ref[out_off:, 4 * w:5 * w]
    o_ref[...] = (_head_rms_norm(o, gain_ref[...], ones_bf16) * _silu(gate)).astype(o_ref.dtype)


def _hgrn(pa, lb_logits, gain, layer, batch, out_rows):
    n, width = pa.shape
    rows = n // batch
    w = GROUP_W
    scr = lambda r: pltpu.VMEM((r, w), F32)
    return pl.pallas_call(
        functools.partial(_hgrn_kernel, layer=layer, out_off=rows - out_rows),
        grid=(batch,),
        in_specs=[pl.BlockSpec((rows, width), lambda b: (b, 0)),
                  pl.BlockSpec(lb_logits.shape, lambda b: (0, 0, 0)),
                  pl.BlockSpec((1, w), lambda b: (0, 0))],
        out_specs=pl.BlockSpec((out_rows, w), lambda b: (b, 0)),
        out_shape=jax.ShapeDtypeStruct((batch * out_rows, w), BF16),
        scratch_shapes=[scr(rows), scr(w), scr(w), scr(ROW_BLK), scr(ROW_BLK), scr(ROW_BLK), scr(ROW_BLK),
                        scr(ROW_BLK)],
        compiler_params=_params("parallel"),
        name="hgrn2_mixer",
    )(pa, lb_logits, gain.reshape(1, w))


def _conv_kernel(p_ref, w_ref, o_ref, *, ctx_len, out_off):
    w = GROUP_W
    p = p_ref[...]
    u = p[:, w:2 * w] * p[:, 2 * w:3 * w]
    n = u.shape[0]
    row = lax.broadcasted_iota(jnp.int32, u.shape, 0)
    prev = jnp.where((row == 0) | (row == ctx_len), 0.0, pltpu.roll(u, 1, 0))
    nxt = jnp.where((row == ctx_len - 1) | (row == n - 1), 0.0, pltpu.roll(u, n - 1, 0))
    cw = w_ref[...]
    y = p[:, 0:w] * (cw[0:1] * prev + cw[1:2] * u + cw[2:3] * nxt)
    o_ref[...] = y[out_off:].astype(o_ref.dtype)


def _conv(pb, conv_w, batch, ctx_len, out_rows):
    n, width = pb.shape
    rows = n // batch
    return pl.pallas_call(
        functools.partial(_conv_kernel, ctx_len=ctx_len, out_off=rows - out_rows),
        grid=(batch,),
        in_specs=[pl.BlockSpec((rows, width), lambda b: (b, 0)),
                  pl.BlockSpec(conv_w.shape, lambda b: (0, 0))],
        out_specs=pl.BlockSpec((out_rows, GROUP_W), lambda b: (b, 0)),
        out_shape=jax.ShapeDtypeStruct((batch * out_rows, GROUP_W), BF16),
        compiler_params=_params("parallel"),
        name="conv_mixer",
    )(pb, conv_w)


def _ret_kernel(q_ref, k_ref, v_ref, g_ref, dl_ref, gain_ref, o_ref, m_scr, *, ctx_len, seq_len, qb_off):
    w = GROUP_W
    qb = pl.program_id(0) + qb_off
    rows = ctx_len + seq_len

    @pl.when(pl.program_id(1) == 0)
    def _():
        x = dl_ref[...]
        lg = jnp.minimum(x, 0.0) - jnp.log(1.0 + jnp.exp(-jnp.abs(x)))
        qrow = lax.broadcasted_iota(jnp.int32, (ROW_BLK, rows), 0) + qb * ROW_BLK
        krow = lax.broadcasted_iota(jnp.int32, (ROW_BLK, rows), 1)
        q_bwd = jnp.where(qrow < ctx_len, qrow + seq_len, qrow - ctx_len)
        k_bwd = jnp.where(krow < ctx_len, krow + seq_len, krow - ctx_len)
        df = (qrow - krow).astype(F32)
        db = (k_bwd - q_bwd).astype(F32)
        for h in range(N_HEADS):
            mf = jnp.where(df >= 0.0, jnp.exp(lg[0:1, h:h + 1] * jnp.maximum(df, 0.0)), 0.0)
            mb = jnp.where(db >= 0.0, jnp.exp(lg[1:2, h:h + 1] * jnp.maximum(db, 0.0)), 0.0)
            m_scr[h] = mf + mb

    q = (q_ref[...] * (HEAD_DIM ** -0.5)).astype(BF16)
    k = k_ref[...].astype(BF16)
    v = v_ref[...].astype(BF16)
    outs = []
    for h in range(N_HEADS):
        sl = slice(h * HEAD_DIM, (h + 1) * HEAD_DIM)
        s = _dot_nt(q[:, sl], k[:, sl]) * m_scr[h]
        outs.append(_dot(s.astype(BF16), v[:, sl]))
    o = jnp.concatenate(outs, axis=1)
    ones_bf16 = _head_ones(w).astype(BF16)
    o_ref[...] = (_head_rms_norm(o, gain_ref[...], ones_bf16) * _silu(g_ref[...])).astype(o_ref.dtype)


def _retention(pc, decay_logit, gain, batch, ctx_len, out_rows):
    n, width = pc.shape
    rows = n // batch
    w = GROUP_W
    bpb = rows // ROW_BLK
    nqb = out_rows // ROW_BLK
    qb_off = bpb - nqb
    dl = jnp.zeros((8, LANES), F32).at[:decay_logit.shape[0], :decay_logit.shape[1]].set(decay_logit)
    return pl.pallas_call(
        functools.partial(_ret_kernel, ctx_len=ctx_len, seq_len=rows - ctx_len, qb_off=qb_off),
        grid=(nqb, batch),
        in_specs=[pl.BlockSpec((ROW_BLK, w), lambda j, b: (b * bpb + j + qb_off, 0)),
                  pl.BlockSpec((rows, w), lambda j, b: (b, 1)),
                  pl.BlockSpec((rows, w), lambda j, b: (b, 2)),
                  pl.BlockSpec((ROW_BLK, w), lambda j, b: (b * bpb + j + qb_off, 3)),
                  pl.BlockSpec((8, LANES), lambda j, b: (0, 0)),
                  pl.BlockSpec((1, w), lambda j, b: (0, 0))],
        out_specs=pl.BlockSpec((ROW_BLK, w), lambda j, b: (b * nqb + j, 0)),
        out_shape=jax.ShapeDtypeStruct((batch * out_rows, w), BF16),
        scratch_shapes=[pltpu.VMEM((N_HEADS, ROW_BLK, rows), F32)],
        compiler_params=_params("arbitrary", "arbitrary"),
        name="retention_mixer",
    )(pc, pc, pc, pc, dl, gain.reshape(1, w))


def _rope(x, cos, sin_signed):
    n = x.shape[-1]
    lane = lax.broadcasted_iota(jnp.int32, x.shape, 1)
    swapped = jnp.where(lane % 2 == 0, pltpu.roll(x, n - 1, 1), pltpu.roll(x, 1, 1))
    return x * cos + swapped * sin_signed


def _attn_kernel(q_ref, k_ref, v_ref, cos_ref, sin_ref, qn_ref, kn_ref, o_ref, k_buf, v_buf, *, ctx_len, qb_off):
    j = pl.program_id(1)
    rows = k_ref.shape[0]

    @pl.when(j == 0)
    def _():
        ones_kv = _head_ones(KV_W).astype(BF16)
        kn = _head_rms_norm(k_ref[...], kn_ref[...], ones_kv)
        k_buf[...] = _rope(kn, cos_ref[:, 0:KV_W], sin_ref[:, 0:KV_W]).astype(BF16)
        v_buf[...] = v_ref[...].astype(BF16)

    qb = j + qb_off
    r0 = pl.multiple_of(qb * ROW_BLK, ROW_BLK)
    ones_q = _head_ones(GROUP_W).astype(BF16)
    qn = _head_rms_norm(q_ref[...], qn_ref[...], ones_q)
    qr = _rope(qn, cos_ref[pl.ds(r0, ROW_BLK), :], sin_ref[pl.ds(r0, ROW_BLK), :])
    qr = (qr * (HEAD_DIM ** -0.5)).astype(BF16)
    kidx = lax.broadcasted_iota(jnp.int32, (1, rows), 1)
    visible = kidx < jnp.where(qb == 0, ctx_len, rows)
    group = GROUP_W // KV_W
    for kv in range(KV_W // HEAD_DIM):
        ksl = slice(kv * HEAD_DIM, (kv + 1) * HEAD_DIM)
        q2 = jnp.concatenate([qr[:, (kv * group + g) * HEAD_DIM:(kv * group + g + 1) * HEAD_DIM]
                              for g in range(group)], axis=0)
        s = jnp.where(visible, _dot_nt(q2, k_buf[:, ksl]), -jnp.inf)
        p = jnp.exp(s - jnp.max(s, axis=-1, keepdims=True))
        o2 = _dot(p.astype(BF16), v_buf[:, ksl]) / jnp.sum(p, axis=-1, keepdims=True)
        for g in range(group):
            h = kv * group + g
            o_ref[:, h * HEAD_DIM:(h + 1) * HEAD_DIM] = o2[g * ROW_BLK:(g + 1) * ROW_BLK].astype(o_ref.dtype)


def _attention(pd, cos, sin_signed, q_norm, k_norm, batch, ctx_len, out_rows):
    n, width = pd.shape
    rows = n // batch
    bpb = rows // ROW_BLK
    nqb = out_rows // ROW_BLK
    qb_off = bpb - nqb
    kcol = GROUP_W // KV_W
    tile = lambda g, reps: jnp.tile(g.reshape(1, HEAD_DIM), (1, reps))
    return pl.pallas_call(
        functools.partial(_attn_kernel, ctx_len=ctx_len, qb_off=qb_off),
        grid=(batch, nqb),
        in_specs=[pl.BlockSpec((ROW_BLK, GROUP_W), lambda b, j: (b * bpb + j + qb_off, 0)),
                  pl.BlockSpec((rows, KV_W), lambda b, j: (b, kcol)),
                  pl.BlockSpec((rows, KV_W), lambda b, j: (b, kcol + 1)),
                  pl.BlockSpec((rows, GROUP_W), lambda b, j: (0, 0)),
                  pl.BlockSpec((rows, GROUP_W), lambda b, j: (0, 0)),
                  pl.BlockSpec((1, GROUP_W), lambda b, j: (0, 0)),
                  pl.BlockSpec((1, KV_W), lambda b, j: (0, 0))],
        out_specs=pl.BlockSpec((ROW_BLK, GROUP_W), lambda b, j: (b * nqb + j, 0)),
        out_shape=jax.ShapeDtypeStruct((batch * out_rows, GROUP_W), BF16),
        scratch_shapes=[pltpu.VMEM((rows, KV_W), BF16), pltpu.VMEM((rows, KV_W), BF16)],
        compiler_params=_params("parallel", "arbitrary"),
        name="attention_mixer",
    )(pd, pd, pd, cos, sin_signed, tile(q_norm, N_HEADS), tile(k_norm, KV_W // HEAD_DIM))


def _rope_tables(ctx_len, seq_len):
    t = jnp.arange(seq_len)
    row = (t // GRID_W).astype(F32)
    col = (t % GRID_W).astype(F32)
    n_freq = HEAD_DIM // 4
    inv_freq = ROPE_THETA ** (-jnp.arange(n_freq, dtype=F32) / n_freq)
    ang = jnp.concatenate([row[:, None] * inv_freq, col[:, None] * inv_freq], axis=-1)
    cos = jnp.repeat(jnp.cos(ang), 2, axis=-1)
    sin = jnp.stack([-jnp.sin(ang), jnp.sin(ang)], axis=-1).reshape(seq_len, HEAD_DIM)
    cos = jnp.concatenate([jnp.ones((ctx_len, HEAD_DIM), F32), cos], axis=0)
    sin = jnp.concatenate([jnp.zeros((ctx_len, HEAD_DIM), F32), sin], axis=0)
    return jnp.tile(cos, (1, N_HEADS)), jnp.tile(sin, (1, N_HEADS))


def _route(logits, n_experts):
    n_groups = n_experts // EXPERTS_PER_GROUP
    lane = lax.broadcasted_iota(jnp.int32, logits.shape, 1)
    lane_f = lane.astype(F32)
    big = float(LANES)
    is_group = (lane >= n_experts) & (lane < n_experts + n_groups)
    gmax = jnp.max(jnp.where(is_group, logits, -jnp.inf), axis=-1, keepdims=True)
    gexp = jnp.where(is_group, jnp.exp(logits - gmax), 0.0)
    gprob = gexp / jnp.sum(gexp, axis=-1, keepdims=True)
    group_p = jnp.max(gprob, axis=-1, keepdims=True)
    gidx = jnp.min(jnp.where(is_group & (gprob == group_p), lane_f, big), axis=-1, keepdims=True) - n_experts
    in_group = (lane < n_experts) & ((lane // EXPERTS_PER_GROUP).astype(F32) == gidx)
    emax = jnp.max(jnp.where(in_group, logits, -jnp.inf), axis=-1, keepdims=True)
    eexp = jnp.where(in_group, jnp.exp(logits - emax), 0.0)
    eprob = eexp / jnp.sum(eexp, axis=-1, keepdims=True)
    p1 = jnp.max(jnp.where(in_group, eprob, -1.0), axis=-1, keepdims=True)
    i1 = jnp.min(jnp.where(in_group & (eprob == p1), lane_f, big), axis=-1, keepdims=True)
    rest = in_group & (lane_f != i1)
    p2 = jnp.max(jnp.where(rest, eprob, -1.0), axis=-1, keepdims=True)
    i2 = jnp.min(jnp.where(rest & (eprob == p2), lane_f, big), axis=-1, keepdims=True)
    denom = p1 + p2
    return (jnp.where(lane_f == i1, group_p * (p1 / denom), 0.0)
            + jnp.where(lane_f == i2, group_p * (p2 / denom), 0.0))


def _outproj_kernel(x_ref, a_ref, b_ref, c_ref, d_ref, mod_ref, gain_ref, w_ref, wr_ref, br_ref,
                    x1_ref, h2_ref, cmb_ref, *, n_experts):
    d = x_ref.shape[-1]
    mix = jnp.concatenate([a_ref[...], b_ref[...], c_ref[...], d_ref[...]], axis=1)
    mod = mod_ref[0]
    x1 = x_ref[...] + mod[:, 2 * d:3 * d] * _dot(mix, w_ref[...])
    x1_ref[...] = x1
    h2 = _norm_modulate(x1, gain_ref[...], mod, 3)
    h2_ref[...] = h2.astype(BF16)
    logits = jnp.dot(h2, wr_ref[...], precision=HIGHEST, preferred_element_type=F32) + br_ref[...]
    cmb_ref[...] = _route(logits, n_experts)


def _outproj(tokens, mixers, mod, gain, w_bf16, w_router, b_router, n_experts, batch, ctx_row):
    d = tokens.shape[1]
    n_out = mixers[0].shape[0]
    nblk = n_out // ROW_BLK
    out_bpb = nblk // batch
    tok_bpb = tokens.shape[0] // ROW_BLK // batch
    off = tok_bpb - out_bpb
    with_ctx = off == 0

    def mod_idx(i):
        b = i // out_bpb
        return ((jnp.where(i % out_bpb == 0, ctx_row, b) if with_ctx else b), 0, 0)

    row_spec = lambda w: pl.BlockSpec((ROW_BLK, w), lambda i: (i, 0))
    full = lambda a: pl.BlockSpec(a.shape, lambda i: (0,) * a.ndim)
    return pl.pallas_call(
        functools.partial(_outproj_kernel, n_experts=n_experts),
        grid=(nblk,),
        in_specs=[pl.BlockSpec((ROW_BLK, d), lambda i: ((i // out_bpb) * tok_bpb + i % out_bpb + off, 0)),
                  row_spec(GROUP_W), row_spec(GROUP_W), row_spec(GROUP_W), row_spec(GROUP_W),
                  pl.BlockSpec((1, 1, mod.shape[-1]), mod_idx),
                  pl.BlockSpec((1, d), lambda i: (0, 0)),
                  full(w_bf16), full(w_router), full(b_router)],
        out_specs=[row_spec(d), row_spec(d), row_spec(LANES)],
        out_shape=[jax.ShapeDtypeStruct((n_out, d), F32),
                   jax.ShapeDtypeStruct((n_out, d), BF16),
                   jax.ShapeDtypeStruct((n_out, LANES), F32)],
        compiler_params=_params("parallel"),
        name="out_proj_router",
    )(tokens, *mixers, mod, gain.reshape(1, d), w_bf16, w_router, b_router)


def _moe_kernel(h_ref, cmb_ref, x1_ref, modc_ref, modb_ref, wg_ref, wu_ref, wd_ref, o_ref, acc,
                *, ctx_len, tiles_per_batch):
    e = pl.program_id(1)
    d = x1_ref.shape[-1]

    @pl.when(e == 0)
    def _():
        acc[...] = jnp.zeros_like(acc)

    h = h_ref[...]
    hidden = _silu(_dot(h, wg_ref[0])) * _dot(h, wu_ref[0])
    y = _dot(hidden.astype(BF16), wd_ref[0])
    cmb = cmb_ref[...]
    lane = lax.broadcasted_iota(jnp.int32, cmb.shape, 1)
    acc[...] += jnp.sum(jnp.where(lane == e, cmb, 0.0), axis=-1, keepdims=True) * y

    @pl.when(e == pl.num_programs(1) - 1)
    def _():
        gate_b = modb_ref[0][:, 5 * d:6 * d]
        if ctx_len:
            first = pl.program_id(0) % tiles_per_batch == 0
            row = lax.broadcasted_iota(jnp.int32, acc.shape, 0)
            gate = jnp.where(first & (row < ctx_len), modc_ref[0][:, 5 * d:6 * d], gate_b)
        else:
            gate = gate_b
        o_ref[...] = x1_ref[...] + gate * acc[...]


def _moe(h2, cmb, x1, mod, wg, wu, wd, batch, ctx_len, ctx_row, with_ctx):
    n, d = x1.shape
    rows = n // batch
    tm = max(t for t in range(ROW_BLK, 4 * ROW_BLK + 1, ROW_BLK) if rows % t == 0)
    tiles_per_batch = rows // tm
    n_experts = wg.shape[0]
    row_spec = lambda w: pl.BlockSpec((tm, w), lambda i, e: (i, 0))
    wspec = lambda a: pl.BlockSpec((1,) + a.shape[1:], lambda i, e: (e, 0, 0))
    return pl.pallas_call(
        functools.partial(_moe_kernel, ctx_len=ctx_len if with_ctx else 0, tiles_per_batch=tiles_per_batch),
        grid=(n // tm, n_experts),
        in_specs=[row_spec(d), row_spec(LANES), row_spec(d),
                  pl.BlockSpec((1, 1, mod.shape[-1]), lambda i, e: (ctx_row, 0, 0)),
                  pl.BlockSpec((1, 1, mod.shape[-1]), lambda i, e: (i // tiles_per_batch, 0, 0)),
                  wspec(wg), wspec(wu), wspec(wd)],
        out_specs=row_spec(d),
        out_shape=jax.ShapeDtypeStruct((n, d), F32),
        scratch_shapes=[pltpu.VMEM((tm, d), F32)],
        compiler_params=_params("parallel", "arbitrary"),
        name="moe_experts",
    )(h2, cmb, x1, mod, mod, wg, wu, wd)


def kernel(x, c, ctx, c_ctx, ada_w, ada_b, norm_mix, norm_ffn, w_in, hgrn_lb_logits, hgrn_norm, conv_w,
           ret_decay_logit, ret_norm, q_norm, k_norm, w_out, router_group_w, router_group_b, router_expert_w,
           router_expert_b, expert_w_gate, expert_w_up, expert_w_down):
    batch, seq_len, d = x.shape
    ctx_len = ctx.shape[1]
    depth = ada_w.shape[0]
    n_experts = expert_w_gate.shape[1]
    n_groups = router_group_w.shape[-1]
    assert ctx_len == ROW_BLK and seq_len % ROW_BLK == 0 and batch + 1 <= MOD_ROWS
    assert n_experts + n_groups <= LANES and n_experts == n_groups * EXPERTS_PER_GROUP
    rows = ctx_len + seq_len
    bpb = rows // ROW_BLK
    ctx_row = batch

    cv = jnp.concatenate([c, c_ctx[None], jnp.zeros((MOD_ROWS - batch - 1, d), F32)], axis=0)
    mod_all = _modulation(cv, ada_w, ada_b)
    cos, sin_signed = _rope_tables(ctx_len, seq_len)
    tokens = jnp.concatenate([ctx, x], axis=1).reshape(batch * rows, d)

    for layer in range(depth):
        with_ctx = layer < depth - 1
        out_rows = rows if with_ctx else seq_len
        mod = mod_all[layer].reshape(MOD_ROWS, 1, 6 * d)
        pa, pb, pc, pd = _inproj(tokens, mod, norm_mix[layer], w_in[layer].astype(BF16), bpb, ctx_row)
        mixers = (
            _hgrn(pa, hgrn_lb_logits, hgrn_norm[layer], layer, batch, out_rows),
            _conv(pb, conv_w[layer], batch, ctx_len, out_rows),
            _retention(pc, ret_decay_logit[layer], ret_norm[layer], batch, ctx_len, out_rows),
            _attention(pd, cos, sin_signed, q_norm[layer], k_norm[layer], batch, ctx_len, out_rows),
        )
        pad = LANES - n_experts - n_groups
        w_router = jnp.concatenate([router_expert_w[layer], router_group_w[layer], jnp.zeros((d, pad), F32)], axis=1)
        b_router = jnp.concatenate([router_expert_b[layer], router_group_b[layer], jnp.zeros((pad,), F32)])[None]
        x1, h2, cmb = _outproj(tokens, mixers, mod, norm_ffn[layer], w_out[layer].astype(BF16), w_router, b_router,
                               n_experts, batch, ctx_row)
        tokens = _moe(h2, cmb, x1, mod, expert_w_gate[layer].astype(BF16), expert_w_up[layer].astype(BF16),
                      expert_w_down[layer].astype(BF16), batch, ctx_len, ctx_row, with_ctx)
    return tokens.reshape(batch, seq_len, d)
```

```python
import functools

import jax
import jax.numpy as jnp
from jax import lax
from jax.experimental import pallas as pl
from jax.experimental.pallas import tpu as pltpu

F32 = jnp.float32
BF16 = jnp.bfloat16
HIGHEST = lax.Precision.HIGHEST

HEAD_DIM = 64
GROUP_W = 256
N_HEADS = GROUP_W // HEAD_DIM
KV_W = 128
GRID_W = 64
ROPE_THETA = 10000.0
EXPERTS_PER_GROUP = 4
NORM_EPS = 1e-6
ROW_BLK = 256
SUB_BLK = 8
LEVEL_ROWS = 128
LANES = 128
MOD_ROWS = 24
ROUTER_ROWS = 32
VMEM_LIMIT_BYTES = 56 * 1024 * 1024
PA_W, PB_W, PC_W, PD_W = 5 * GROUP_W, 3 * GROUP_W, 4 * GROUP_W, GROUP_W + 2 * KV_W


def _params(*semantics):
    return pltpu.CompilerParams(dimension_semantics=semantics, vmem_limit_bytes=VMEM_LIMIT_BYTES)


def _dot(a, b):
    return jnp.dot(a, b, preferred_element_type=F32)


def _dot_nt(a, b):
    return lax.dot_general(a, b, (((1,), (1,)), ((), ())), preferred_element_type=F32)


def _dot_tn(a, b):
    return lax.dot_general(a, b, (((0,), (0,)), ((), ())), preferred_element_type=F32)


def _recip(x):
    return pl.reciprocal(x, approx=True)


def _sigmoid(x):
    return _recip(1.0 + jnp.exp(-x))


def _silu(x):
    return x * _sigmoid(x)


def _log_sigmoid(x):
    return jnp.minimum(x, 0.0) - jnp.log(1.0 + jnp.exp(-jnp.abs(x)))


def _split_bf16(x, terms):
    parts = []
    rem = x
    for i in range(terms):
        p = rem.astype(BF16)
        parts.append(p)
        if i + 1 < terms:
            rem = rem - p.astype(F32)
    return parts


def _head_ones(width):
    r = lax.broadcasted_iota(jnp.int32, (width, width), 0) // HEAD_DIM
    c = lax.broadcasted_iota(jnp.int32, (width, width), 1) // HEAD_DIM
    return r == c


def _head_masks(width):
    lane_head = lax.broadcasted_iota(jnp.int32, (1, width), 1) // HEAD_DIM
    return [lane_head == h for h in range(width // HEAD_DIM)]


def _head_expand(x, hmasks):
    zero = jnp.zeros_like(x)
    return jnp.concatenate([jnp.where(hm, x, zero) for hm in hmasks], axis=0).astype(BF16)


def _head_mean_sq(x, ones_bf16):
    hi, lo = _split_bf16(x * x, 2)
    return (_dot(hi, ones_bf16) + _dot(lo, ones_bf16)) * (1.0 / HEAD_DIM)


def _head_rms_norm(x, gain, ones_bf16):
    return x * lax.rsqrt(_head_mean_sq(x, ones_bf16) + NORM_EPS) * gain


def _norm_modulate(x, gain, mod, idx):
    d = x.shape[-1]
    ms = jnp.mean(x * x, axis=-1, keepdims=True)
    y = x * lax.rsqrt(ms + NORM_EPS) * gain
    shift = mod[:, idx * d:(idx + 1) * d]
    scale = mod[:, (idx + 1) * d:(idx + 2) * d]
    return y * (1.0 + scale) + shift


def _scan_rows(j, nchunk):
    rf = pl.multiple_of(j * ROW_BLK, ROW_BLK)
    rb = pl.multiple_of(jnp.where(j == 0, 0, nchunk - j) * ROW_BLK, ROW_BLK)
    return rf, rb


def _mod_kernel(cv_ref, w_ref, b_ref, o_ref):
    s = _silu(cv_ref[...])
    o_ref[0] = jnp.dot(s, w_ref[0], precision=HIGHEST, preferred_element_type=F32) + b_ref[0]


def _modulation(cv, ada_w, ada_b):
    depth, d, n = ada_w.shape
    tn = n // 4
    return pl.pallas_call(
        _mod_kernel,
        grid=(depth, n // tn),
        in_specs=[pl.BlockSpec((MOD_ROWS, d), lambda l, j: (0, 0)),
                  pl.BlockSpec((1, d, tn), lambda l, j: (l, 0, j)),
                  pl.BlockSpec((1, 1, tn), lambda l, j: (l, 0, j))],
        out_specs=pl.BlockSpec((1, MOD_ROWS, tn), lambda l, j: (l, 0, j)),
        out_shape=jax.ShapeDtypeStruct((depth, MOD_ROWS, n), F32),
        compiler_params=_params("parallel", "parallel"),
        name="adaln_mod",
    )(cv, ada_w, ada_b.reshape(depth, 1, n))


def _inproj_kernel(x_ref, mod_ref, gain_ref, w_ref, pa_ref, pb_ref, pc_ref, pd_ref):
    h = _norm_modulate(x_ref[...], gain_ref[...], mod_ref[0], 0)
    p = _dot(h.astype(BF16), w_ref[...])
    pa_ref[...] = p[:, 0:PA_W]
    pb_ref[...] = p[:, PA_W:PA_W + PB_W].astype(pb_ref.dtype)
    pc_ref[...] = p[:, PA_W + PB_W:PA_W + PB_W + PC_W].astype(pc_ref.dtype)
    pd_ref[...] = p[:, PA_W + PB_W + PC_W:].astype(pd_ref.dtype)


def _inproj(tokens, mod, gain, w_bf16, blocks_per_batch, ctx_row):
    n, d = tokens.shape
    nblk = n // ROW_BLK

    def mod_idx(i):
        return (jnp.where(i % blocks_per_batch == 0, ctx_row, i // blocks_per_batch), 0, 0)

    outs = ((PA_W, F32), (PB_W, BF16), (PC_W, BF16), (PD_W, BF16))
    return pl.pallas_call(
        _inproj_kernel,
        grid=(nblk,),
        in_specs=[pl.BlockSpec((ROW_BLK, d), lambda i: (i, 0)),
                  pl.BlockSpec((1, 1, mod.shape[-1]), mod_idx),
                  pl.BlockSpec((1, d), lambda i: (0, 0)),
                  pl.BlockSpec(w_bf16.shape, lambda i: (0, 0))],
        out_specs=[pl.BlockSpec((ROW_BLK, w), lambda i: (i, 0)) for w, _ in outs],
        out_shape=[jax.ShapeDtypeStruct((n, w), dt) for w, dt in outs],
        compiler_params=_params("parallel"),
        name="in_proj",
    )(tokens, mod, gain.reshape(1, d), w_bf16)


def _hgrn_prepare(blk, lb_row, z_col, anti, s_ref, scr, consts):
    tril, triu, bmask, hmasks, level_masks = consts
    c_scr, q_scr, k_scr, v_scr = scr
    w = GROUP_W
    c_len = blk.shape[0]
    q = blk[:, 0:w]
    v = blk[:, w:2 * w]
    z = blk[:, z_col * w:(z_col + 1) * w]
    f = lb_row + (1.0 - lb_row) * _sigmoid(z)
    g = jnp.log(f)
    k = 1.0 - f
    tri = triu if anti else tril
    c = sum(_dot(tri, part) for part in _split_bf16(g, 3))
    tot = c[0:1] if anti else c[c_len - 1:c_len]
    c_scr[...] = c
    q_scr[...] = q
    k_scr[...] = k
    v_scr[...] = v

    s_t = s_ref[...]
    o = _dot_nt((q * jnp.exp(c)).astype(BF16), s_t.astype(BF16))
    k_end = (k * jnp.exp(tot - c)).astype(BF16)
    s_ref[...] = s_t * jnp.exp(tot) + jnp.where(bmask, _dot_tn(v.astype(BF16), k_end), 0.0)

    v_exp = {}
    m = c_len // 2
    while m >= SUB_BLK:
        nb = c_len // (2 * m)
        mid = m if anti else m - 1
        refs = [jnp.broadcast_to(c[b * 2 * m + mid:b * 2 * m + mid + 1], (2 * m, w)) for b in range(nb)]
        ref = refs[0] if nb == 1 else jnp.concatenate(refs, axis=0)
        e = jnp.exp(-jnp.abs(c - ref))
        qt = q * e
        kt = k * e
        rg = max(LEVEL_ROWS, 2 * m)
        pieces = []
        for r0 in range(0, c_len, rg):
            if (rg, r0) not in v_exp:
                v_exp[(rg, r0)] = _head_expand(v[r0:r0 + rg], hmasks)
            a = _dot_nt(qt[r0:r0 + rg].astype(BF16), _head_expand(kt[r0:r0 + rg], hmasks))
            a = jnp.where(level_masks[(m, anti)], a, 0.0).astype(BF16)
            pieces.append(_dot(a, v_exp[(rg, r0)]))
        o = o + (pieces[0] if len(pieces) == 1 else jnp.concatenate(pieces, axis=0))
        m //= 2
    return o


def _hgrn_level_masks(c_len):
    masks = {}
    m = c_len // 2
    while m >= SUB_BLK:
        rg = max(LEVEL_ROWS, 2 * m)
        t = lax.broadcasted_iota(jnp.int32, (rg, N_HEADS * rg), 0)
        s = lax.broadcasted_iota(jnp.int32, (rg, N_HEADS * rg), 1) % rg
        same = (t // (2 * m)) == (s // (2 * m))
        t_late = (t // m) % 2 == 1
        s_late = (s // m) % 2 == 1
        masks[(m, False)] = same & t_late & ~s_late
        masks[(m, True)] = same & ~t_late & s_late
        m //= 2
    return masks


def _hgrn_diag_unit(scr, r, anti, ones_bf16):
    c_scr, q_scr, k_scr, v_scr = scr
    cs = c_scr[pl.ds(r, SUB_BLK), :]
    qs = q_scr[pl.ds(r, SUB_BLK), :]
    ks = k_scr[pl.ds(r, SUB_BLK), :]
    vs = v_scr[pl.ds(r, SUB_BLK), :]
    row = lax.broadcasted_iota(jnp.int32, cs.shape, 0)
    prods = []
    for s in range(SUB_BLK):
        valid = (row <= s) if anti else (row >= s)
        dec = jnp.exp(jnp.where(valid, cs - cs[s:s + 1], -jnp.inf))
        prods.append(qs * dec * ks[s:s + 1])
    scores = _dot(jnp.concatenate(prods, axis=0).astype(BF16), ones_bf16)
    od = scores[0:SUB_BLK] * vs[0:1]
    for s in range(1, SUB_BLK):
        od = od + scores[s * SUB_BLK:(s + 1) * SUB_BLK] * vs[s:s + 1]
    return od


def _hgrn_kernel(p_ref, lbl_ref, gain_ref, o_ref, acc, s_f, s_b, od_f, od_b, *scr, layer, out_off):
    w = GROUP_W
    c_len = ROW_BLK
    nchunk = p_ref.shape[0] // c_len
    depth = lbl_ref.shape[0]
    scr_f, scr_b = scr[:4], scr[4:]

    logits = [lbl_ref[l] for l in range(depth)]
    mx = functools.reduce(jnp.maximum, logits)
    exps = [jnp.exp(l - mx) for l in logits]
    lb = sum(exps[1:layer + 1], jnp.zeros_like(mx)) * _recip(sum(exps))

    ri = lax.broadcasted_iota(jnp.int32, (c_len, c_len), 0)
    ci = lax.broadcasted_iota(jnp.int32, (c_len, c_len), 1)
    tril = (ci <= ri).astype(BF16)
    triu = (ci >= ri).astype(BF16)
    bmask = _head_ones(w)
    ones_bf16 = bmask.astype(BF16)
    consts = (tril, triu, bmask, _head_masks(w), _hgrn_level_masks(c_len))

    acc[...] = jnp.zeros_like(acc)
    s_f[...] = jnp.zeros_like(s_f)
    s_b[...] = jnp.zeros_like(s_b)

    def step(j, carry):
        rf, rb = _scan_rows(j, nchunk)
        of = _hgrn_prepare(p_ref[pl.ds(rf, c_len), :], lb[0:1], 2, False, s_f, scr_f, consts)
        ob = _hgrn_prepare(p_ref[pl.ds(rb, c_len), :], lb[1:2], 3, True, s_b, scr_b, consts)

        def diag(i, c2):
            r = pl.multiple_of(i * SUB_BLK, SUB_BLK)
            od_f[pl.ds(r, SUB_BLK), :] = _hgrn_diag_unit(scr_f, r, False, ones_bf16)
            od_b[pl.ds(r, SUB_BLK), :] = _hgrn_diag_unit(scr_b, r, True, ones_bf16)
            return c2

        lax.fori_loop(0, c_len // SUB_BLK, diag, 0, unroll=4)
        acc[pl.ds(rf, c_len), :] += of + od_f[...]
        acc[pl.ds(rb, c_len), :] += ob + od_b[...]
        return carry

    lax.fori_loop(0, nchunk, step, 0)

    o = acc[out_off:, :]
    gate = p_ref[out_off:, 4 * w:5 * w]
    o_ref[...] = (_head_rms_norm(o, gain_ref[...], ones_bf16) * _silu(gate)).astype(o_ref.dtype)


def _hgrn(pa, lb_logits, gain, layer, batch, out_rows):
    n, width = pa.shape
    rows = n // batch
    w = GROUP_W
    scr = lambda r: pltpu.VMEM((r, w), F32)
    return pl.pallas_call(
        functools.partial(_hgrn_kernel, layer=layer, out_off=rows - out_rows),
        grid=(batch,),
        in_specs=[pl.BlockSpec((rows, width), lambda b: (b, 0)),
                  pl.BlockSpec(lb_logits.shape, lambda b: (0, 0, 0)),
                  pl.BlockSpec((1, w), lambda b: (0, 0))],
        out_specs=pl.BlockSpec((out_rows, w), lambda b: (b, 0)),
        out_shape=jax.ShapeDtypeStruct((batch * out_rows, w), BF16),
        scratch_shapes=[scr(rows), scr(w), scr(w)] + [scr(ROW_BLK)] * 10,
        compiler_params=_params("parallel"),
        name="hgrn2_mixer",
    )(pa, lb_logits, gain.reshape(1, w))


def _conv_kernel(p_ref, w_ref, o_ref, *, ctx_len, out_off):
    w = GROUP_W
    p = p_ref[...].astype(F32)
    u = p[:, w:2 * w] * p[:, 2 * w:3 * w]
    n = u.shape[0]
    row = lax.broadcasted_iota(jnp.int32, u.shape, 0)
    prev = jnp.where((row == 0) | (row == ctx_len), 0.0, pltpu.roll(u, 1, 0))
    nxt = jnp.where((row == ctx_len - 1) | (row == n - 1), 0.0, pltpu.roll(u, n - 1, 0))
    cw = w_ref[...]
    y = p[:, 0:w] * (cw[0:1] * prev + cw[1:2] * u + cw[2:3] * nxt)
    o_ref[...] = y[out_off:].astype(o_ref.dtype)


def _conv(pb, conv_w, batch, ctx_len, out_rows):
    n, width = pb.shape
    rows = n // batch
    return pl.pallas_call(
        functools.partial(_conv_kernel, ctx_len=ctx_len, out_off=rows - out_rows),
        grid=(batch,),
        in_specs=[pl.BlockSpec((rows, width), lambda b: (b, 0)),
                  pl.BlockSpec(conv_w.shape, lambda b: (0, 0))],
        out_specs=pl.BlockSpec((out_rows, GROUP_W), lambda b: (b, 0)),
        out_shape=jax.ShapeDtypeStruct((batch * out_rows, GROUP_W), BF16),
        compiler_params=_params("parallel"),
        name="conv_mixer",
    )(pb, conv_w)


def _ret_kernel(p_ref, dl_ref, gain_ref, o_ref, acc, s_f, s_b, d2_scr, *, out_off):
    w = GROUP_W
    c_len = ROW_BLK
    nchunk = p_ref.shape[0] // c_len
    hmasks = _head_masks(w)
    bmask = _head_ones(w)
    ones_bf16 = bmask.astype(BF16)

    lg = _log_sigmoid(dl_ref[...])
    lane_w = lambda r: sum(jnp.where(hm, lg[r:r + 1, h:h + 1], 0.0) for h, hm in enumerate(hmasks))
    lgf, lgb = lane_w(0), lane_w(1)
    t = lax.broadcasted_iota(jnp.int32, (c_len, w), 0).astype(F32)
    scale = HEAD_DIM ** -0.5
    qf_dec = jnp.exp(lgf * (t + 1.0)) * scale
    kf_dec = jnp.exp(lgf * (c_len - 1.0 - t))
    qb_dec = jnp.exp(lgb * (c_len - t)) * scale
    kb_dec = jnp.exp(lgb * t)
    tot_f = jnp.exp(lgf * float(c_len))
    tot_b = jnp.exp(lgb * float(c_len))

    col = lax.broadcasted_iota(jnp.int32, (c_len, N_HEADS * c_len), 1)
    row = lax.broadcasted_iota(jnp.int32, (c_len, N_HEADS * c_len), 0)
    col_head = lax.broadcasted_iota(jnp.int32, (1, N_HEADS * c_len), 1) // c_len
    col_lg = lambda r: sum(jnp.where(col_head == h, lg[r:r + 1, h:h + 1], 0.0) for h in range(N_HEADS))
    dist = (row - col % c_len).astype(F32)
    d2_scr[...] = (jnp.where(dist >= 0.0, jnp.exp(col_lg(0) * jnp.maximum(dist, 0.0)), 0.0)
                   + jnp.where(dist <= 0.0, jnp.exp(col_lg(1) * jnp.maximum(-dist, 0.0)), 0.0))

    acc[...] = jnp.zeros_like(acc)
    s_f[...] = jnp.zeros_like(s_f)
    s_b[...] = jnp.zeros_like(s_b)

    def inter(q, k, v, q_dec, k_dec, tot, s_ref):
        s_t = s_ref[...]
        o = _dot_nt((q * q_dec).astype(BF16), s_t.astype(BF16))
        s_ref[...] = s_t * tot + jnp.where(bmask, _dot_tn(v, (k * k_dec).astype(BF16)), 0.0)
        return o

    def step(j, carry):
        rf, rb = _scan_rows(j, nchunk)
        blk = p_ref[pl.ds(rf, c_len), :]
        q = blk[:, 0:w].astype(F32)
        k = blk[:, w:2 * w]
        v = blk[:, 2 * w:3 * w]
        a = _dot_nt((q * scale).astype(BF16), _head_expand(k, hmasks)) * d2_scr[...]
        o = _dot(a.astype(BF16), _head_expand(v, hmasks))
        o = o + inter(q, k.astype(F32), v, qf_dec, kf_dec, tot_f, s_f)
        acc[pl.ds(rf, c_len), :] += o
        blk = p_ref[pl.ds(rb, c_len), :]
        acc[pl.ds(rb, c_len), :] += inter(blk[:, 0:w].astype(F32), blk[:, w:2 * w].astype(F32),
                                          blk[:, 2 * w:3 * w], qb_dec, kb_dec, tot_b, s_b)
        return carry

    lax.fori_loop(0, nchunk, step, 0)

    o = acc[out_off:, :]
    gate = p_ref[out_off:, 3 * w:4 * w].astype(F32)
    o_ref[...] = (_head_rms_norm(o, gain_ref[...], ones_bf16) * _silu(gate)).astype(o_ref.dtype)


def _retention(pc, decay_logit, gain, batch, out_rows):
    n, width = pc.shape
    rows = n // batch
    w = GROUP_W
    dl = jnp.zeros((8, LANES), F32).at[:decay_logit.shape[0], :decay_logit.shape[1]].set(decay_logit)
    return pl.pallas_call(
        functools.partial(_ret_kernel, out_off=rows - out_rows),
        grid=(batch,),
        in_specs=[pl.BlockSpec((rows, width), lambda b: (b, 0)),
                  pl.BlockSpec((8, LANES), lambda b: (0, 0)),
                  pl.BlockSpec((1, w), lambda b: (0, 0))],
        out_specs=pl.BlockSpec((out_rows, w), lambda b: (b, 0)),
        out_shape=jax.ShapeDtypeStruct((batch * out_rows, w), BF16),
        scratch_shapes=[pltpu.VMEM((rows, w), F32), pltpu.VMEM((w, w), F32), pltpu.VMEM((w, w), F32),
                        pltpu.VMEM((ROW_BLK, N_HEADS * ROW_BLK), F32)],
        compiler_params=_params("parallel"),
        name="retention_mixer",
    )(pc, dl, gain.reshape(1, w))


def _rope(x, cos, sin_signed):
    n = x.shape[-1]
    lane = lax.broadcasted_iota(jnp.int32, x.shape, 1)
    swapped = jnp.where(lane % 2 == 0, pltpu.roll(x, n - 1, 1), pltpu.roll(x, 1, 1))
    return x * cos + swapped * sin_signed


def _attn_kernel(q_ref, k_ref, v_ref, cos_ref, sin_ref, qn_ref, kn_ref, o_ref, k_buf, *, ctx_len, qb_off):
    j = pl.program_id(1)
    rows = k_ref.shape[0]

    @pl.when(j == 0)
    def _():
        ones_kv = _head_ones(KV_W).astype(BF16)
        kn = _head_rms_norm(k_ref[...].astype(F32), kn_ref[...], ones_kv)
        k_buf[...] = _rope(kn, cos_ref[:, 0:KV_W], sin_ref[:, 0:KV_W]).astype(BF16)

    qb = j + qb_off
    r0 = pl.multiple_of(qb * ROW_BLK, ROW_BLK)
    ones_q = _head_ones(GROUP_W).astype(BF16)
    qn = _head_rms_norm(q_ref[...].astype(F32), qn_ref[...], ones_q)
    qr = _rope(qn, cos_ref[pl.ds(r0, ROW_BLK), :], sin_ref[pl.ds(r0, ROW_BLK), :])
    qr = (qr * (HEAD_DIM ** -0.5)).astype(BF16)
    kidx = lax.broadcasted_iota(jnp.int32, (1, rows), 1)
    visible = kidx < jnp.where(qb == 0, ctx_len, rows)
    group = GROUP_W // KV_W
    for kv in range(KV_W // HEAD_DIM):
        ksl = slice(kv * HEAD_DIM, (kv + 1) * HEAD_DIM)
        q2 = jnp.concatenate([qr[:, (kv * group + g) * HEAD_DIM:(kv * group + g + 1) * HEAD_DIM]
                              for g in range(group)], axis=0)
        s = jnp.where(visible, _dot_nt(q2, k_buf[:, ksl]), -jnp.inf)
        p = jnp.exp(s - jnp.max(s, axis=-1, keepdims=True))
        o2 = _dot(p.astype(BF16), v_ref[:, ksl]) * _recip(jnp.sum(p, axis=-1, keepdims=True))
        for g in range(group):
            h = kv * group + g
            o_ref[:, h * HEAD_DIM:(h + 1) * HEAD_DIM] = o2[g * ROW_BLK:(g + 1) * ROW_BLK].astype(o_ref.dtype)


def _attention(pd, cos, sin_signed, q_norm, k_norm, batch, ctx_len, out_rows):
    n, width = pd.shape
    rows = n // batch
    bpb = rows // ROW_BLK
    nqb = out_rows // ROW_BLK
    qb_off = bpb - nqb
    kcol = GROUP_W // KV_W
    tile = lambda g, reps: jnp.tile(g.reshape(1, HEAD_DIM), (1, reps))
    return pl.pallas_call(
        functools.partial(_attn_kernel, ctx_len=ctx_len, qb_off=qb_off),
        grid=(batch, nqb),
        in_specs=[pl.BlockSpec((ROW_BLK, GROUP_W), lambda b, j: (b * bpb + j + qb_off, 0)),
                  pl.BlockSpec((rows, KV_W), lambda b, j: (b, kcol)),
                  pl.BlockSpec((rows, KV_W), lambda b, j: (b, kcol + 1)),
                  pl.BlockSpec((rows, GROUP_W), lambda b, j: (0, 0)),
                  pl.BlockSpec((rows, GROUP_W), lambda b, j: (0, 0)),
                  pl.BlockSpec((1, GROUP_W), lambda b, j: (0, 0)),
                  pl.BlockSpec((1, KV_W), lambda b, j: (0, 0))],
        out_specs=pl.BlockSpec((ROW_BLK, GROUP_W), lambda b, j: (b * nqb + j, 0)),
        out_shape=jax.ShapeDtypeStruct((batch * out_rows, GROUP_W), BF16),
        scratch_shapes=[pltpu.VMEM((rows, KV_W), BF16)],
        compiler_params=_params("parallel", "arbitrary"),
        name="attention_mixer",
    )(pd, pd, pd, cos, sin_signed, tile(q_norm, N_HEADS), tile(k_norm, KV_W // HEAD_DIM))


def _rope_tables(ctx_len, seq_len):
    t = jnp.arange(seq_len)
    row = (t // GRID_W).astype(F32)
    col = (t % GRID_W).astype(F32)
    n_freq = HEAD_DIM // 4
    inv_freq = ROPE_THETA ** (-jnp.arange(n_freq, dtype=F32) / n_freq)
    ang = jnp.concatenate([row[:, None] * inv_freq, col[:, None] * inv_freq], axis=-1)
    cos = jnp.repeat(jnp.cos(ang), 2, axis=-1)
    sin = jnp.stack([-jnp.sin(ang), jnp.sin(ang)], axis=-1).reshape(seq_len, HEAD_DIM)
    cos = jnp.concatenate([jnp.ones((ctx_len, HEAD_DIM), F32), cos], axis=0)
    sin = jnp.concatenate([jnp.zeros((ctx_len, HEAD_DIM), F32), sin], axis=0)
    return jnp.tile(cos, (1, N_HEADS)), jnp.tile(sin, (1, N_HEADS))


def _route(lt, n_experts):
    n_groups = n_experts // EXPERTS_PER_GROUP
    first = lambda x, hit, n: jnp.min(jnp.where(hit, x, float(n)), axis=0, keepdims=True)

    gl = lt[n_experts:n_experts + n_groups]
    gexp = jnp.exp(gl - jnp.max(gl, axis=0, keepdims=True))
    gprob = gexp * _recip(jnp.sum(gexp, axis=0, keepdims=True))
    group_p = jnp.max(gprob, axis=0, keepdims=True)
    grow = lax.broadcasted_iota(jnp.int32, gl.shape, 0).astype(F32)
    gidx = first(grow, gprob == group_p, n_groups)

    el = sum(jnp.where(gidx == float(g), lt[g * EXPERTS_PER_GROUP:(g + 1) * EXPERTS_PER_GROUP], 0.0)
             for g in range(n_groups))
    eexp = jnp.exp(el - jnp.max(el, axis=0, keepdims=True))
    eprob = eexp * _recip(jnp.sum(eexp, axis=0, keepdims=True))
    erow = lax.broadcasted_iota(jnp.int32, el.shape, 0).astype(F32)
    p1 = jnp.max(eprob, axis=0, keepdims=True)
    i1 = first(erow, eprob == p1, EXPERTS_PER_GROUP)
    rest = erow != i1
    p2 = jnp.max(jnp.where(rest, eprob, -1.0), axis=0, keepdims=True)
    i2 = first(erow, rest & (eprob == p2), EXPERTS_PER_GROUP)
    scale = group_p * _recip(p1 + p2)
    in_group = jnp.where(erow == i1, p1 * scale, 0.0) + jnp.where(erow == i2, p2 * scale, 0.0)
    return jnp.concatenate([jnp.where(gidx == float(g), in_group, 0.0) for g in range(n_groups)], axis=0)


def _outproj_kernel(x_ref, a_ref, b_ref, c_ref, d_ref, mod_ref, gain_ref, w_ref, wr_ref, br_ref,
                    x1_ref, h2_ref, cmb_ref, *, n_experts):
    d = x_ref.shape[-1]
    mix = jnp.concatenate([a_ref[...], b_ref[...], c_ref[...], d_ref[...]], axis=1)
    mod = mod_ref[0]
    x1 = x_ref[...] + mod[:, 2 * d:3 * d] * _dot(mix, w_ref[...])
    x1_ref[...] = x1
    h2 = _norm_modulate(x1, gain_ref[...], mod, 3)
    h2_ref[...] = h2.astype(BF16)
    lt = lax.dot_general(wr_ref[...], h2, (((1,), (1,)), ((), ())), precision=HIGHEST,
                         preferred_element_type=F32) + br_ref[...]
    cmb = _route(lt, n_experts)
    pad = jnp.zeros((LANES - n_experts, cmb.shape[1]), F32)
    cmb_ref[...] = jnp.concatenate([cmb, pad], axis=0).T


def _outproj(tokens, mixers, mod, gain, w_bf16, w_router_t, b_router_t, n_experts, batch, ctx_row):
    d = tokens.shape[1]
    n_out = mixers[0].shape[0]
    nblk = n_out // ROW_BLK
    out_bpb = nblk // batch
    tok_bpb = tokens.shape[0] // ROW_BLK // batch
    off = tok_bpb - out_bpb
    with_ctx = off == 0

    def mod_idx(i):
        b = i // out_bpb
        return ((jnp.where(i % out_bpb == 0, ctx_row, b) if with_ctx else b), 0, 0)

    row_spec = lambda w: pl.BlockSpec((ROW_BLK, w), lambda i: (i, 0))
    full = lambda a: pl.BlockSpec(a.shape, lambda i: (0,) * a.ndim)
    return pl.pallas_call(
        functools.partial(_outproj_kernel, n_experts=n_experts),
        grid=(nblk,),
        in_specs=[pl.BlockSpec((ROW_BLK, d), lambda i: ((i // out_bpb) * tok_bpb + i % out_bpb + off, 0)),
                  row_spec(GROUP_W), row_spec(GROUP_W), row_spec(GROUP_W), row_spec(GROUP_W),
                  pl.BlockSpec((1, 1, mod.shape[-1]), mod_idx),
                  pl.BlockSpec((1, d), lambda i: (0, 0)),
                  full(w_bf16), full(w_router_t), full(b_router_t)],
        out_specs=[row_spec(d), row_spec(d), row_spec(LANES)],
        out_shape=[jax.ShapeDtypeStruct((n_out, d), F32),
                   jax.ShapeDtypeStruct((n_out, d), BF16),
                   jax.ShapeDtypeStruct((n_out, LANES), F32)],
        compiler_params=_params("parallel"),
        name="out_proj_router",
    )(tokens, *mixers, mod, gain.reshape(1, d), w_bf16, w_router_t, b_router_t)


def _moe_kernel(h_ref, cmb_ref, x1_ref, modc_ref, modb_ref, wg_ref, wu_ref, wd_ref, o_ref, acc,
                *, ctx_len, tiles_per_batch):
    e = pl.program_id(1)
    d = x1_ref.shape[-1]

    @pl.when(e == 0)
    def _():
        acc[...] = jnp.zeros_like(acc)

    h = h_ref[...]
    hidden = _silu(_dot(h, wg_ref[0])) * _dot(h, wu_ref[0])
    y = _dot(hidden.astype(BF16), wd_ref[0])
    cmb = cmb_ref[...]
    lane = lax.broadcasted_iota(jnp.int32, cmb.shape, 1)
    acc[...] += jnp.sum(jnp.where(lane == e, cmb, 0.0), axis=-1, keepdims=True) * y

    @pl.when(e == pl.num_programs(1) - 1)
    def _():
        gate_b = modb_ref[0][:, 5 * d:6 * d]
        if ctx_len:
            first = pl.program_id(0) % tiles_per_batch == 0
            row = lax.broadcasted_iota(jnp.int32, acc.shape, 0)
            gate = jnp.where(first & (row < ctx_len), modc_ref[0][:, 5 * d:6 * d], gate_b)
        else:
            gate = gate_b
        o_ref[...] = x1_ref[...] + gate * acc[...]


def _moe(h2, cmb, x1, mod, wg, wu, wd, batch, ctx_len, ctx_row, with_ctx):
    n, d = x1.shape
    rows = n // batch
    tm = max(t for t in range(ROW_BLK, 4 * ROW_BLK + 1, ROW_BLK) if rows % t == 0)
    tiles_per_batch = rows // tm
    n_experts = wg.shape[0]
    row_spec = lambda w: pl.BlockSpec((tm, w), lambda i, e: (i, 0))
    wspec = lambda a: pl.BlockSpec((1,) + a.shape[1:], lambda i, e: (e, 0, 0))
    return pl.pallas_call(
        functools.partial(_moe_kernel, ctx_len=ctx_len if with_ctx else 0, tiles_per_batch=tiles_per_batch),
        grid=(n // tm, n_experts),
        in_specs=[row_spec(d), row_spec(LANES), row_spec(d),
                  pl.BlockSpec((1, 1, mod.shape[-1]), lambda i, e: (ctx_row, 0, 0)),
                  pl.BlockSpec((1, 1, mod.shape[-1]), lambda i, e: (i // tiles_per_batch, 0, 0)),
                  wspec(wg), wspec(wu), wspec(wd)],
        out_specs=row_spec(d),
        out_shape=jax.ShapeDtypeStruct((n, d), F32),
        scratch_shapes=[pltpu.VMEM((tm, d), F32)],
        compiler_params=_params("parallel", "arbitrary"),
        name="moe_experts",
    )(h2, cmb, x1, mod, mod, wg, wu, wd)


def kernel(x, c, ctx, c_ctx, ada_w, ada_b, norm_mix, norm_ffn, w_in, hgrn_lb_logits, hgrn_norm, conv_w,
           ret_decay_logit, ret_norm, q_norm, k_norm, w_out, router_group_w, router_group_b, router_expert_w,
           router_expert_b, expert_w_gate, expert_w_up, expert_w_down):
    batch, seq_len, d = x.shape
    ctx_len = ctx.shape[1]
    depth = ada_w.shape[0]
    n_experts = expert_w_gate.shape[1]
    n_groups = router_group_w.shape[-1]
    assert ctx_len == ROW_BLK and seq_len % ROW_BLK == 0 and batch + 1 <= MOD_ROWS
    assert n_experts + n_groups <= ROUTER_ROWS and n_experts == n_groups * EXPERTS_PER_GROUP
    rows = ctx_len + seq_len
    bpb = rows // ROW_BLK
    ctx_row = batch

    cv = jnp.concatenate([c, c_ctx[None], jnp.zeros((MOD_ROWS - batch - 1, d), F32)], axis=0)
    mod_all = _modulation(cv, ada_w, ada_b)
    cos, sin_signed = _rope_tables(ctx_len, seq_len)
    tokens = jnp.concatenate([ctx, x], axis=1).reshape(batch * rows, d)

    for layer in range(depth):
        with_ctx = layer < depth - 1
        out_rows = rows if with_ctx else seq_len
        mod = mod_all[layer].reshape(MOD_ROWS, 1, 6 * d)
        pa, pb, pc, pd = _inproj(tokens, mod, norm_mix[layer], w_in[layer].astype(BF16), bpb, ctx_row)
        mixers = (
            _hgrn(pa, hgrn_lb_logits, hgrn_norm[layer], layer, batch, out_rows),
            _conv(pb, conv_w[layer], batch, ctx_len, out_rows),
            _retention(pc, ret_decay_logit[layer], ret_norm[layer], batch, out_rows),
            _attention(pd, cos, sin_signed, q_norm[layer], k_norm[layer], batch, ctx_len, out_rows),
        )
        pad = ROUTER_ROWS - n_experts - n_groups
        w_router_t = jnp.concatenate([router_expert_w[layer].T, router_group_w[layer].T, jnp.zeros((pad, d), F32)])
        b_router = jnp.concatenate([router_expert_b[layer], router_group_b[layer], jnp.zeros((pad,), F32)])
        b_router_t = jnp.broadcast_to(b_router[:, None], (ROUTER_ROWS, ROW_BLK))
        x1, h2, cmb = _outproj(tokens, mixers, mod, norm_ffn[layer], w_out[layer].astype(BF16), w_router_t,
                               b_router_t, n_experts, batch, ctx_row)
        tokens = _moe(h2, cmb, x1, mod, expert_w_gate[layer].astype(BF16), expert_w_up[layer].astype(BF16),
                      expert_w_down[layer].astype(BF16), batch, ctx_len, ctx_row, with_ctx)
    return tokens.reshape(batch, seq_len, d)
```

```python
import functools

import jax
import jax.numpy as jnp
from jax import lax
from jax.experimental import pallas as pl
from jax.experimental.pallas import tpu as pltpu

F32 = jnp.float32
BF16 = jnp.bfloat16
HIGHEST = lax.Precision.HIGHEST

HEAD_DIM = 64
GROUP_W = 256
N_HEADS = GROUP_W // HEAD_DIM
KV_W = 128
GRID_W = 64
ROPE_THETA = 10000.0
EXPERTS_PER_GROUP = 4
NORM_EPS = 1e-6
ROW_BLK = 256
SUB_BLK = 8
LEVEL_ROWS = 128
LANES = 128
MOD_ROWS = 24
ROUTER_ROWS = 32
ROUTE_ROWS = 8
PAIRS_PER_GROUP = EXPERTS_PER_GROUP * (EXPERTS_PER_GROUP - 1) // 2
TOKEN_SUB = 8
VMEM_LIMIT_BYTES = 56 * 1024 * 1024
PA_W, PB_W, PC_W, PD_W = 5 * GROUP_W, 3 * GROUP_W, 4 * GROUP_W, GROUP_W + 2 * KV_W


def _params(*semantics):
    return pltpu.CompilerParams(dimension_semantics=semantics, vmem_limit_bytes=VMEM_LIMIT_BYTES)


def _dot(a, b):
    return jnp.dot(a, b, preferred_element_type=F32)


def _dot_nt(a, b):
    return lax.dot_general(a, b, (((1,), (1,)), ((), ())), preferred_element_type=F32)


def _dot_tn(a, b):
    return lax.dot_general(a, b, (((0,), (0,)), ((), ())), preferred_element_type=F32)


def _recip(x):
    return pl.reciprocal(x, approx=True)


def _sigmoid(x):
    return _recip(1.0 + jnp.exp(-x))


def _silu(x):
    return x * _sigmoid(x)


def _log_sigmoid(x):
    return jnp.minimum(x, 0.0) - jnp.log(1.0 + jnp.exp(-jnp.abs(x)))


def _split_bf16(x, terms):
    parts = []
    rem = x
    for i in range(terms):
        p = rem.astype(BF16)
        parts.append(p)
        if i + 1 < terms:
            rem = rem - p.astype(F32)
    return parts


def _head_ones(width):
    r = lax.broadcasted_iota(jnp.int32, (width, width), 0) // HEAD_DIM
    c = lax.broadcasted_iota(jnp.int32, (width, width), 1) // HEAD_DIM
    return r == c


def _head_masks(width):
    lane_head = lax.broadcasted_iota(jnp.int32, (1, width), 1) // HEAD_DIM
    return [lane_head == h for h in range(width // HEAD_DIM)]


def _head_expand(x, hmasks):
    zero = jnp.zeros_like(x)
    return jnp.concatenate([jnp.where(hm, x, zero) for hm in hmasks], axis=0).astype(BF16)


def _head_mean_sq(x, ones_bf16):
    hi, lo = _split_bf16(x * x, 2)
    return (_dot(hi, ones_bf16) + _dot(lo, ones_bf16)) * (1.0 / HEAD_DIM)


def _head_rms_norm(x, gain, ones_bf16):
    return x * lax.rsqrt(_head_mean_sq(x, ones_bf16) + NORM_EPS) * gain


def _norm_modulate(x, gain, mod, idx):
    d = x.shape[-1]
    ms = jnp.mean(x * x, axis=-1, keepdims=True)
    y = x * lax.rsqrt(ms + NORM_EPS) * gain
    shift = mod[:, idx * d:(idx + 1) * d]
    scale = mod[:, (idx + 1) * d:(idx + 2) * d]
    return y * (1.0 + scale) + shift


def _scan_rows(j, nchunk):
    rf = pl.multiple_of(j * ROW_BLK, ROW_BLK)
    rb = pl.multiple_of(jnp.where(j == 0, 0, nchunk - j) * ROW_BLK, ROW_BLK)
    return rf, rb


def _mod_kernel(cv_ref, w_ref, b_ref, o_ref):
    s = _silu(cv_ref[...])
    o_ref[0] = jnp.dot(s, w_ref[0], precision=HIGHEST, preferred_element_type=F32) + b_ref[0]


def _modulation(cv, ada_w, ada_b):
    depth, d, n = ada_w.shape
    tn = n // 4
    return pl.pallas_call(
        _mod_kernel,
        grid=(depth, n // tn),
        in_specs=[pl.BlockSpec((MOD_ROWS, d), lambda l, j: (0, 0)),
                  pl.BlockSpec((1, d, tn), lambda l, j: (l, 0, j)),
                  pl.BlockSpec((1, 1, tn), lambda l, j: (l, 0, j))],
        out_specs=pl.BlockSpec((1, MOD_ROWS, tn), lambda l, j: (l, 0, j)),
        out_shape=jax.ShapeDtypeStruct((depth, MOD_ROWS, n), F32),
        compiler_params=_params("parallel", "parallel"),
        name="adaln_mod",
    )(cv, ada_w, ada_b.reshape(depth, 1, n))


def _inproj_kernel(x_ref, mod_ref, gain_ref, w_ref, pa_ref, pb_ref, pc_ref, pd_ref):
    h = _norm_modulate(x_ref[...], gain_ref[...], mod_ref[0], 0)
    p = _dot(h.astype(BF16), w_ref[...])
    pa_ref[...] = p[:, 0:PA_W]
    pb_ref[...] = p[:, PA_W:PA_W + PB_W].astype(pb_ref.dtype)
    pc_ref[...] = p[:, PA_W + PB_W:PA_W + PB_W + PC_W].astype(pc_ref.dtype)
    pd_ref[...] = p[:, PA_W + PB_W + PC_W:].astype(pd_ref.dtype)


def _inproj(tokens, mod, gain, w_bf16, blocks_per_batch, ctx_row):
    n, d = tokens.shape
    nblk = n // ROW_BLK

    def mod_idx(i):
        return (jnp.where(i % blocks_per_batch == 0, ctx_row, i // blocks_per_batch), 0, 0)

    outs = ((PA_W, F32), (PB_W, BF16), (PC_W, BF16), (PD_W, BF16))
    return pl.pallas_call(
        _inproj_kernel,
        grid=(nblk,),
        in_specs=[pl.BlockSpec((ROW_BLK, d), lambda i: (i, 0)),
                  pl.BlockSpec((1, 1, mod.shape[-1]), mod_idx),
                  pl.BlockSpec((1, d), lambda i: (0, 0)),
                  pl.BlockSpec(w_bf16.shape, lambda i: (0, 0))],
        out_specs=[pl.BlockSpec((ROW_BLK, w), lambda i: (i, 0)) for w, _ in outs],
        out_shape=[jax.ShapeDtypeStruct((n, w), dt) for w, dt in outs],
        compiler_params=_params("parallel"),
        name="in_proj",
    )(tokens, mod, gain.reshape(1, d), w_bf16)


def _hgrn_prepare(blk, lb_row, z_col, anti, s_ref, scr, consts):
    tril, triu, bmask, hmasks, level_masks = consts
    c_scr, q_scr, k_scr, v_scr = scr
    w = GROUP_W
    c_len = blk.shape[0]
    q = blk[:, 0:w]
    v = blk[:, w:2 * w]
    z = blk[:, z_col * w:(z_col + 1) * w]
    f = lb_row + (1.0 - lb_row) * _sigmoid(z)
    g = jnp.log(f)
    k = 1.0 - f
    tri = triu if anti else tril
    c = sum(_dot(tri, part) for part in _split_bf16(g, 3))
    tot = c[0:1] if anti else c[c_len - 1:c_len]
    c_scr[...] = c
    q_scr[...] = q
    k_scr[...] = k
    v_scr[...] = v

    s_t = s_ref[...]
    o = _dot_nt((q * jnp.exp(c)).astype(BF16), s_t.astype(BF16))
    k_end = (k * jnp.exp(tot - c)).astype(BF16)
    s_ref[...] = s_t * jnp.exp(tot) + jnp.where(bmask, _dot_tn(v.astype(BF16), k_end), 0.0)

    v_exp = {}
    m = c_len // 2
    while m >= SUB_BLK:
        nb = c_len // (2 * m)
        mid = m if anti else m - 1
        refs = [jnp.broadcast_to(c[b * 2 * m + mid:b * 2 * m + mid + 1], (2 * m, w)) for b in range(nb)]
        ref = refs[0] if nb == 1 else jnp.concatenate(refs, axis=0)
        e = jnp.exp(-jnp.abs(c - ref))
        qt = q * e
        kt = k * e
        rg = max(LEVEL_ROWS, 2 * m)
        pieces = []
        for r0 in range(0, c_len, rg):
            if (rg, r0) not in v_exp:
                v_exp[(rg, r0)] = _head_expand(v[r0:r0 + rg], hmasks)
            a = _dot_nt(qt[r0:r0 + rg].astype(BF16), _head_expand(kt[r0:r0 + rg], hmasks))
            a = jnp.where(level_masks[(m, anti)], a, 0.0).astype(BF16)
            pieces.append(_dot(a, v_exp[(rg, r0)]))
        o = o + (pieces[0] if len(pieces) == 1 else jnp.concatenate(pieces, axis=0))
        m //= 2
    return o


def _hgrn_level_masks(c_len):
    masks = {}
    m = c_len // 2
    while m >= SUB_BLK:
        rg = max(LEVEL_ROWS, 2 * m)
        t = lax.broadcasted_iota(jnp.int32, (rg, N_HEADS * rg), 0)
        s = lax.broadcasted_iota(jnp.int32, (rg, N_HEADS * rg), 1) % rg
        same = (t // (2 * m)) == (s // (2 * m))
        t_late = (t // m) % 2 == 1
        s_late = (s // m) % 2 == 1
        masks[(m, False)] = same & t_late & ~s_late
        masks[(m, True)] = same & ~t_late & s_late
        m //= 2
    return masks


def _hgrn_diag_unit(scr, r, anti, ones_bf16):
    c_scr, q_scr, k_scr, v_scr = scr
    cs = c_scr[pl.ds(r, SUB_BLK), :]
    qs = q_scr[pl.ds(r, SUB_BLK), :]
    ks = k_scr[pl.ds(r, SUB_BLK), :]
    vs = v_scr[pl.ds(r, SUB_BLK), :]
    row = lax.broadcasted_iota(jnp.int32, cs.shape, 0)
    prods = []
    for s in range(SUB_BLK):
        valid = (row <= s) if anti else (row >= s)
        dec = jnp.exp(jnp.where(valid, cs - cs[s:s + 1], -jnp.inf))
        prods.append(qs * dec * ks[s:s + 1])
    scores = _dot(jnp.concatenate(prods, axis=0).astype(BF16), ones_bf16)
    od = scores[0:SUB_BLK] * vs[0:1]
    for s in range(1, SUB_BLK):
        od = od + scores[s * SUB_BLK:(s + 1) * SUB_BLK] * vs[s:s + 1]
    return od


def _hgrn_kernel(p_ref, lbl_ref, gain_ref, o_ref, acc, s_f, s_b, od_f, od_b, *scr, layer, out_off):
    w = GROUP_W
    c_len = ROW_BLK
    nchunk = p_ref.shape[0] // c_len
    depth = lbl_ref.shape[0]
    scr_f, scr_b = scr[:4], scr[4:]

    logits = [lbl_ref[l] for l in range(depth)]
    mx = functools.reduce(jnp.maximum, logits)
    exps = [jnp.exp(l - mx) for l in logits]
    lb = sum(exps[1:layer + 1], jnp.zeros_like(mx)) * _recip(sum(exps))

    ri = lax.broadcasted_iota(jnp.int32, (c_len, c_len), 0)
    ci = lax.broadcasted_iota(jnp.int32, (c_len, c_len), 1)
    tril = (ci <= ri).astype(BF16)
    triu = (ci >= ri).astype(BF16)
    bmask = _head_ones(w)
    ones_bf16 = bmask.astype(BF16)
    consts = (tril, triu, bmask, _head_masks(w), _hgrn_level_masks(c_len))

    acc[...] = jnp.zeros_like(acc)
    s_f[...] = jnp.zeros_like(s_f)
    s_b[...] = jnp.zeros_like(s_b)

    def step(j, carry):
        rf, rb = _scan_rows(j, nchunk)
        of = _hgrn_prepare(p_ref[pl.ds(rf, c_len), :], lb[0:1], 2, False, s_f, scr_f, consts)
        ob = _hgrn_prepare(p_ref[pl.ds(rb, c_len), :], lb[1:2], 3, True, s_b, scr_b, consts)

        def diag(i, c2):
            r = pl.multiple_of(i * SUB_BLK, SUB_BLK)
            od_f[pl.ds(r, SUB_BLK), :] = _hgrn_diag_unit(scr_f, r, False, ones_bf16)
            od_b[pl.ds(r, SUB_BLK), :] = _hgrn_diag_unit(scr_b, r, True, ones_bf16)
            return c2

        lax.fori_loop(0, c_len // SUB_BLK, diag, 0, unroll=4)
        acc[pl.ds(rf, c_len), :] += of + od_f[...]
        acc[pl.ds(rb, c_len), :] += ob + od_b[...]
        return carry

    lax.fori_loop(0, nchunk, step, 0)

    o = acc[out_off:, :]
    gate = p_ref[out_off:, 4 * w:5 * w]
    o_ref[...] = (_head_rms_norm(o, gain_ref[...], ones_bf16) * _silu(gate)).astype(o_ref.dtype)


def _hgrn(pa, lb_logits, gain, layer, batch, out_rows):
    n, width = pa.shape
    rows = n // batch
    w = GROUP_W
    scr = lambda r: pltpu.VMEM((r, w), F32)
    return pl.pallas_call(
        functools.partial(_hgrn_kernel, layer=layer, out_off=rows - out_rows),
        grid=(batch,),
        in_specs=[pl.BlockSpec((rows, width), lambda b: (b, 0)),
                  pl.BlockSpec(lb_logits.shape, lambda b: (0, 0, 0)),
                  pl.BlockSpec((1, w), lambda b: (0, 0))],
        out_specs=pl.BlockSpec((out_rows, w), lambda b: (b, 0)),
        out_shape=jax.ShapeDtypeStruct((batch * out_rows, w), BF16),
        scratch_shapes=[scr(rows), scr(w), scr(w)] + [scr(ROW_BLK)] * 10,
        compiler_params=_params("parallel"),
        name="hgrn2_mixer",
    )(pa, lb_logits, gain.reshape(1, w))


def _conv_kernel(p_ref, w_ref, o_ref, *, ctx_len, out_off):
    w = GROUP_W
    p = p_ref[...].astype(F32)
    u = p[:, w:2 * w] * p[:, 2 * w:3 * w]
    n = u.shape[0]
    row = lax.broadcasted_iota(jnp.int32, u.shape, 0)
    prev = jnp.where((row == 0) | (row == ctx_len), 0.0, pltpu.roll(u, 1, 0))
    nxt = jnp.where((row == ctx_len - 1) | (row == n - 1), 0.0, pltpu.roll(u, n - 1, 0))
    cw = w_ref[...]
    y = p[:, 0:w] * (cw[0:1] * prev + cw[1:2] * u + cw[2:3] * nxt)
    o_ref[...] = y[out_off:].astype(o_ref.dtype)


def _conv(pb, conv_w, batch, ctx_len, out_rows):
    n, width = pb.shape
    rows = n // batch
    return pl.pallas_call(
        functools.partial(_conv_kernel, ctx_len=ctx_len, out_off=rows - out_rows),
        grid=(batch,),
        in_specs=[pl.BlockSpec((rows, width), lambda b: (b, 0)),
                  pl.BlockSpec(conv_w.shape, lambda b: (0, 0))],
        out_specs=pl.BlockSpec((out_rows, GROUP_W), lambda b: (b, 0)),
        out_shape=jax.ShapeDtypeStruct((batch * out_rows, GROUP_W), BF16),
        compiler_params=_params("parallel"),
        name="conv_mixer",
    )(pb, conv_w)


def _ret_kernel(p_ref, dl_ref, gain_ref, o_ref, acc, s_f, s_b, d2_scr, *, out_off):
    w = GROUP_W
    c_len = ROW_BLK
    nchunk = p_ref.shape[0] // c_len
    hmasks = _head_masks(w)
    bmask = _head_ones(w)
    ones_bf16 = bmask.astype(BF16)

    lg = _log_sigmoid(dl_ref[...])
    lane_w = lambda r: sum(jnp.where(hm, lg[r:r + 1, h:h + 1], 0.0) for h, hm in enumerate(hmasks))
    lgf, lgb = lane_w(0), lane_w(1)
    t = lax.broadcasted_iota(jnp.int32, (c_len, w), 0).astype(F32)
    scale = HEAD_DIM ** -0.5
    qf_dec = jnp.exp(lgf * (t + 1.0)) * scale
    kf_dec = jnp.exp(lgf * (c_len - 1.0 - t))
    qb_dec = jnp.exp(lgb * (c_len - t)) * scale
    kb_dec = jnp.exp(lgb * t)
    tot_f = jnp.exp(lgf * float(c_len))
    tot_b = jnp.exp(lgb * float(c_len))

    col = lax.broadcasted_iota(jnp.int32, (c_len, N_HEADS * c_len), 1)
    row = lax.broadcasted_iota(jnp.int32, (c_len, N_HEADS * c_len), 0)
    col_head = lax.broadcasted_iota(jnp.int32, (1, N_HEADS * c_len), 1) // c_len
    col_lg = lambda r: sum(jnp.where(col_head == h, lg[r:r + 1, h:h + 1], 0.0) for h in range(N_HEADS))
    dist = (row - col % c_len).astype(F32)
    d2_scr[...] = (jnp.where(dist >= 0.0, jnp.exp(col_lg(0) * jnp.maximum(dist, 0.0)), 0.0)
                   + jnp.where(dist <= 0.0, jnp.exp(col_lg(1) * jnp.maximum(-dist, 0.0)), 0.0))

    acc[...] = jnp.zeros_like(acc)
    s_f[...] = jnp.zeros_like(s_f)
    s_b[...] = jnp.zeros_like(s_b)

    def inter(q, k, v, q_dec, k_dec, tot, s_ref):
        s_t = s_ref[...]
        o = _dot_nt((q * q_dec).astype(BF16), s_t.astype(BF16))
        s_ref[...] = s_t * tot + jnp.where(bmask, _dot_tn(v, (k * k_dec).astype(BF16)), 0.0)
        return o

    def step(j, carry):
        rf, rb = _scan_rows(j, nchunk)
        blk = p_ref[pl.ds(rf, c_len), :]
        q = blk[:, 0:w].astype(F32)
        k = blk[:, w:2 * w]
        v = blk[:, 2 * w:3 * w]
        a = _dot_nt((q * scale).astype(BF16), _head_expand(k, hmasks)) * d2_scr[...]
        o = _dot(a.astype(BF16), _head_expand(v, hmasks))
        o = o + inter(q, k.astype(F32), v, qf_dec, kf_dec, tot_f, s_f)
        acc[pl.ds(rf, c_len), :] += o
        blk = p_ref[pl.ds(rb, c_len), :]
        acc[pl.ds(rb, c_len), :] += inter(blk[:, 0:w].astype(F32), blk[:, w:2 * w].astype(F32),
                                          blk[:, 2 * w:3 * w], qb_dec, kb_dec, tot_b, s_b)
        return carry

    lax.fori_loop(0, nchunk, step, 0)

    o = acc[out_off:, :]
    gate = p_ref[out_off:, 3 * w:4 * w].astype(F32)
    o_ref[...] = (_head_rms_norm(o, gain_ref[...], ones_bf16) * _silu(gate)).astype(o_ref.dtype)


def _retention(pc, decay_logit, gain, batch, out_rows):
    n, width = pc.shape
    rows = n // batch
    w = GROUP_W
    dl = jnp.zeros((8, LANES), F32).at[:decay_logit.shape[0], :decay_logit.shape[1]].set(decay_logit)
    return pl.pallas_call(
        functools.partial(_ret_kernel, out_off=rows - out_rows),
        grid=(batch,),
        in_specs=[pl.BlockSpec((rows, width), lambda b: (b, 0)),
                  pl.BlockSpec((8, LANES), lambda b: (0, 0)),
                  pl.BlockSpec((1, w), lambda b: (0, 0))],
        out_specs=pl.BlockSpec((out_rows, w), lambda b: (b, 0)),
        out_shape=jax.ShapeDtypeStruct((batch * out_rows, w), BF16),
        scratch_shapes=[pltpu.VMEM((rows, w), F32), pltpu.VMEM((w, w), F32), pltpu.VMEM((w, w), F32),
                        pltpu.VMEM((ROW_BLK, N_HEADS * ROW_BLK), F32)],
        compiler_params=_params("parallel"),
        name="retention_mixer",
    )(pc, dl, gain.reshape(1, w))


def _rope(x, cos, sin_signed):
    n = x.shape[-1]
    lane = lax.broadcasted_iota(jnp.int32, x.shape, 1)
    swapped = jnp.where(lane % 2 == 0, pltpu.roll(x, n - 1, 1), pltpu.roll(x, 1, 1))
    return x * cos + swapped * sin_signed


def _attn_kernel(q_ref, k_ref, v_ref, cos_ref, sin_ref, qn_ref, kn_ref, o_ref, k_buf, *, ctx_len, qb_off):
    j = pl.program_id(1)
    rows = k_ref.shape[0]

    @pl.when(j == 0)
    def _():
        ones_kv = _head_ones(KV_W).astype(BF16)
        kn = _head_rms_norm(k_ref[...].astype(F32), kn_ref[...], ones_kv)
        k_buf[...] = _rope(kn, cos_ref[:, 0:KV_W], sin_ref[:, 0:KV_W]).astype(BF16)

    qb = j + qb_off
    r0 = pl.multiple_of(qb * ROW_BLK, ROW_BLK)
    ones_q = _head_ones(GROUP_W).astype(BF16)
    qn = _head_rms_norm(q_ref[...].astype(F32), qn_ref[...], ones_q)
    qr = _rope(qn, cos_ref[pl.ds(r0, ROW_BLK), :], sin_ref[pl.ds(r0, ROW_BLK), :])
    qr = (qr * (HEAD_DIM ** -0.5)).astype(BF16)
    kidx = lax.broadcasted_iota(jnp.int32, (1, rows), 1)
    visible = kidx < jnp.where(qb == 0, ctx_len, rows)
    group = GROUP_W // KV_W
    for kv in range(KV_W // HEAD_DIM):
        ksl = slice(kv * HEAD_DIM, (kv + 1) * HEAD_DIM)
        q2 = jnp.concatenate([qr[:, (kv * group + g) * HEAD_DIM:(kv * group + g + 1) * HEAD_DIM]
                              for g in range(group)], axis=0)
        s = jnp.where(visible, _dot_nt(q2, k_buf[:, ksl]), -jnp.inf)
        p = jnp.exp(s - jnp.max(s, axis=-1, keepdims=True))
        o2 = _dot(p.astype(BF16), v_ref[:, ksl]) * _recip(jnp.sum(p, axis=-1, keepdims=True))
        for g in range(group):
            h = kv * group + g
            o_ref[:, h * HEAD_DIM:(h + 1) * HEAD_DIM] = o2[g * ROW_BLK:(g + 1) * ROW_BLK].astype(o_ref.dtype)


def _attention(pd, cos, sin_signed, q_norm, k_norm, batch, ctx_len, out_rows):
    n, width = pd.shape
    rows = n // batch
    bpb = rows // ROW_BLK
    nqb = out_rows // ROW_BLK
    qb_off = bpb - nqb
    kcol = GROUP_W // KV_W
    tile = lambda g, reps: jnp.tile(g.reshape(1, HEAD_DIM), (1, reps))
    return pl.pallas_call(
        functools.partial(_attn_kernel, ctx_len=ctx_len, qb_off=qb_off),
        grid=(batch, nqb),
        in_specs=[pl.BlockSpec((ROW_BLK, GROUP_W), lambda b, j: (b * bpb + j + qb_off, 0)),
                  pl.BlockSpec((rows, KV_W), lambda b, j: (b, kcol)),
                  pl.BlockSpec((rows, KV_W), lambda b, j: (b, kcol + 1)),
                  pl.BlockSpec((rows, GROUP_W), lambda b, j: (0, 0)),
                  pl.BlockSpec((rows, GROUP_W), lambda b, j: (0, 0)),
                  pl.BlockSpec((1, GROUP_W), lambda b, j: (0, 0)),
                  pl.BlockSpec((1, KV_W), lambda b, j: (0, 0))],
        out_specs=pl.BlockSpec((ROW_BLK, GROUP_W), lambda b, j: (b * nqb + j, 0)),
        out_shape=jax.ShapeDtypeStruct((batch * out_rows, GROUP_W), BF16),
        scratch_shapes=[pltpu.VMEM((rows, KV_W), BF16)],
        compiler_params=_params("parallel", "arbitrary"),
        name="attention_mixer",
    )(pd, pd, pd, cos, sin_signed, tile(q_norm, N_HEADS), tile(k_norm, KV_W // HEAD_DIM))


def _rope_tables(ctx_len, seq_len):
    t = jnp.arange(seq_len)
    row = (t // GRID_W).astype(F32)
    col = (t % GRID_W).astype(F32)
    n_freq = HEAD_DIM // 4
    inv_freq = ROPE_THETA ** (-jnp.arange(n_freq, dtype=F32) / n_freq)
    ang = jnp.concatenate([row[:, None] * inv_freq, col[:, None] * inv_freq], axis=-1)
    cos = jnp.repeat(jnp.cos(ang), 2, axis=-1)
    sin = jnp.stack([-jnp.sin(ang), jnp.sin(ang)], axis=-1).reshape(seq_len, HEAD_DIM)
    cos = jnp.concatenate([jnp.ones((ctx_len, HEAD_DIM), F32), cos], axis=0)
    sin = jnp.concatenate([jnp.zeros((ctx_len, HEAD_DIM), F32), sin], axis=0)
    return jnp.tile(cos, (1, N_HEADS)), jnp.tile(sin, (1, N_HEADS))


def _route(lt, n_experts):
    n_groups = n_experts // EXPERTS_PER_GROUP
    first = lambda x, hit, n: jnp.min(jnp.where(hit, x, float(n)), axis=0, keepdims=True)

    gl = lt[n_experts:n_experts + n_groups]
    gexp = jnp.exp(gl - jnp.max(gl, axis=0, keepdims=True))
    gprob = gexp * _recip(jnp.sum(gexp, axis=0, keepdims=True))
    group_p = jnp.max(gprob, axis=0, keepdims=True)
    grow = lax.broadcasted_iota(jnp.int32, gl.shape, 0).astype(F32)
    gidx = first(grow, gprob == group_p, n_groups)

    el = sum(jnp.where(gidx == float(g), lt[g * EXPERTS_PER_GROUP:(g + 1) * EXPERTS_PER_GROUP], 0.0)
             for g in range(n_groups))
    eexp = jnp.exp(el - jnp.max(el, axis=0, keepdims=True))
    eprob = eexp * _recip(jnp.sum(eexp, axis=0, keepdims=True))
    erow = lax.broadcasted_iota(jnp.int32, el.shape, 0).astype(F32)
    p1 = jnp.max(eprob, axis=0, keepdims=True)
    i1 = first(erow, eprob == p1, EXPERTS_PER_GROUP)
    rest = erow != i1
    p2 = jnp.max(jnp.where(rest, eprob, -1.0), axis=0, keepdims=True)
    i2 = first(erow, rest & (eprob == p2), EXPERTS_PER_GROUP)
    scale = group_p * _recip(p1 + p2)
    lo = jnp.minimum(i1, i2)
    hi = jnp.maximum(i1, i2)
    w_lo = jnp.where(i1 < i2, p1, p2) * scale
    w_hi = jnp.where(i1 < i2, p2, p1) * scale
    pair = lo * (2.0 * EXPERTS_PER_GROUP - 1.0 - lo) * 0.5 + (hi - lo - 1.0)
    bucket = gidx * float(PAIRS_PER_GROUP) + pair
    return jnp.concatenate([bucket, w_lo, w_hi, jnp.zeros((ROUTE_ROWS - 3, lt.shape[1]), F32)], axis=0)


def _outproj_kernel(x_ref, a_ref, b_ref, c_ref, d_ref, mod_ref, gain_ref, w_ref, wr_ref, br_ref,
                    x1_ref, h2_ref, rt_ref, *, n_experts):
    d = x_ref.shape[-1]
    mix = jnp.concatenate([a_ref[...], b_ref[...], c_ref[...], d_ref[...]], axis=1)
    mod = mod_ref[0]
    x1 = x_ref[...] + mod[:, 2 * d:3 * d] * _dot(mix, w_ref[...])
    x1_ref[...] = x1
    h2 = _norm_modulate(x1, gain_ref[...], mod, 3)
    _store_token_tiles(h2_ref, h2)
    lt = lax.dot_general(wr_ref[...], h2, (((1,), (1,)), ((), ())), precision=HIGHEST,
                         preferred_element_type=F32) + br_ref[...]
    rt_ref[...] = _route(lt, n_experts)


def _outproj(tokens, mixers, mod, gain, w_bf16, w_router_t, b_router_t, n_experts, batch, ctx_row):
    d = tokens.shape[1]
    n_out = mixers[0].shape[0]
    nblk = n_out // ROW_BLK
    out_bpb = nblk // batch
    tok_bpb = tokens.shape[0] // ROW_BLK // batch
    off = tok_bpb - out_bpb
    with_ctx = off == 0

    def mod_idx(i):
        b = i // out_bpb
        return ((jnp.where(i % out_bpb == 0, ctx_row, b) if with_ctx else b), 0, 0)

    row_spec = lambda w: pl.BlockSpec((ROW_BLK, w), lambda i: (i, 0))
    full = lambda a: pl.BlockSpec(a.shape, lambda i: (0,) * a.ndim)
    return pl.pallas_call(
        functools.partial(_outproj_kernel, n_experts=n_experts),
        grid=(nblk,),
        in_specs=[pl.BlockSpec((ROW_BLK, d), lambda i: ((i // out_bpb) * tok_bpb + i % out_bpb + off, 0)),
                  row_spec(GROUP_W), row_spec(GROUP_W), row_spec(GROUP_W), row_spec(GROUP_W),
                  pl.BlockSpec((1, 1, mod.shape[-1]), mod_idx),
                  pl.BlockSpec((1, d), lambda i: (0, 0)),
                  full(w_bf16), full(w_router_t), full(b_router_t)],
        out_specs=[row_spec(d),
                   pl.BlockSpec((ROW_BLK, TOKEN_SUB, LANES), lambda i: (i, 0, 0)),
                   pl.BlockSpec((ROUTE_ROWS, ROW_BLK), lambda i: (0, i))],
        out_shape=[jax.ShapeDtypeStruct((n_out, d), F32),
                   jax.ShapeDtypeStruct((n_out, TOKEN_SUB, LANES), F32),
                   jax.ShapeDtypeStruct((ROUTE_ROWS, n_out), F32)],
        compiler_params=_params("parallel"),
        name="out_proj_router",
    )(tokens, *mixers, mod, gain.reshape(1, d), w_bf16, w_router_t, b_router_t)


def _store_token_tiles(ref, x):
    for j in range(TOKEN_SUB):
        ref[:, j, :] = x[:, j * LANES:(j + 1) * LANES]


def _load_token_tiles(ref):
    return jnp.concatenate([ref[:, j, :] for j in range(TOKEN_SUB)], axis=-1)


def _gather_tokens(idx_ref, src_hbm, buf, sem):
    n = buf.shape[0]
    copy = lambda r, i: pltpu.make_async_copy(src_hbm.at[i], buf.at[r], sem)

    def start(r, carry):
        copy(r, idx_ref[0, 0, r]).start()
        return carry

    def wait(r, carry):
        copy(r, 0).wait()
        return carry

    lax.fori_loop(0, n, start, 0, unroll=8)
    lax.fori_loop(0, n, wait, 0, unroll=8)


def _moe_plan(rt, n_buckets):
    n = rt.shape[1]
    n_tiles = n // ROW_BLK + n_buckets
    bucket = rt[0].astype(jnp.int32)
    order = jnp.argsort(bucket, stable=True).astype(jnp.int32)
    rank = jnp.zeros((n,), jnp.int32).at[order].set(jnp.arange(n, dtype=jnp.int32))
    counts = jnp.sum((bucket[:, None] == jnp.arange(n_buckets)[None, :]).astype(jnp.int32), axis=0)
    cstart = jnp.cumsum(counts) - counts
    tiles = (counts + ROW_BLK - 1) // ROW_BLK
    tend = jnp.cumsum(tiles)
    tstart = tend - tiles
    used = tend[-1]
    tile_id = jnp.arange(n_tiles, dtype=jnp.int32)
    tile_valid = tile_id < used
    tile_bucket = jnp.searchsorted(tend, tile_id, side='right').astype(jnp.int32)
    tile_bucket = jnp.where(tile_valid, tile_bucket, tile_bucket[jnp.maximum(used - 1, 0)])
    row = jnp.arange(n_tiles * ROW_BLK, dtype=jnp.int32)
    row_bucket = tile_bucket[row // ROW_BLK]
    within = row - tstart[row_bucket] * ROW_BLK
    row_valid = tile_valid[row // ROW_BLK] & (within < counts[row_bucket])
    src = jnp.where(row_valid, order[jnp.clip(cstart[row_bucket] + within, 0, n - 1)], 0)
    w_sorted = jnp.where(row_valid[:, None], rt[1:3].T[src], 0.0)
    w_sorted = jnp.pad(w_sorted, ((0, 0), (0, ROUTE_ROWS - 2)))
    pos = tstart[bucket] * ROW_BLK + rank - cstart[bucket]
    group = tile_bucket // PAIRS_PER_GROUP
    pair = tile_bucket % PAIRS_PER_GROUP
    lo_tab = jnp.array([a for a in range(EXPERTS_PER_GROUP) for _ in range(a + 1, EXPERTS_PER_GROUP)], jnp.int32)
    hi_tab = jnp.array([b for a in range(EXPERTS_PER_GROUP) for b in range(a + 1, EXPERTS_PER_GROUP)], jnp.int32)
    tile_lo = group * EXPERTS_PER_GROUP + lo_tab[pair]
    tile_hi = group * EXPERTS_PER_GROUP + hi_tab[pair]
    return (tile_lo, tile_hi, tile_valid.astype(jnp.int32), src.reshape(n_tiles, 1, ROW_BLK), w_sorted,
            pos.reshape(n // ROW_BLK, 1, ROW_BLK))


def _moe_expert_kernel(lo_ref, hi_ref, valid_ref, idx_ref, w_ref, h_hbm, wg_lo, wu_lo, wd_lo, wg_hi, wu_hi, wd_hi,
                       y_ref, buf, sem):
    i = pl.program_id(0)

    @pl.when(valid_ref[i] == 0)
    def _():
        y_ref[...] = jnp.zeros_like(y_ref)

    @pl.when(valid_ref[i] == 1)
    def _():
        _gather_tokens(idx_ref, h_hbm, buf, sem)
        h = _load_token_tiles(buf).astype(BF16)
        wts = w_ref[...]

        def expert(wg, wu, wd):
            hidden = _silu(_dot(h, wg[0])) * _dot(h, wu[0])
            return _dot(hidden.astype(BF16), wd[0])

        y = wts[:, 0:1] * expert(wg_lo, wu_lo, wd_lo) + wts[:, 1:2] * expert(wg_hi, wu_hi, wd_hi)
        _store_token_tiles(y_ref, y)


def _moe_experts(plan, h2, wg, wu, wd):
    tile_lo, tile_hi, tile_valid, src, w_sorted, _ = plan
    n_tiles = src.shape[0]
    by_lo = lambda a: pl.BlockSpec((1,) + a.shape[1:], lambda i, lo, hi, v: (lo[i], 0, 0))
    by_hi = lambda a: pl.BlockSpec((1,) + a.shape[1:], lambda i, lo, hi, v: (hi[i], 0, 0))
    grid_spec = pltpu.PrefetchScalarGridSpec(
        num_scalar_prefetch=3,
        grid=(n_tiles,),
        in_specs=[pl.BlockSpec((1, 1, ROW_BLK), lambda i, lo, hi, v: (i, 0, 0), memory_space=pltpu.SMEM),
                  pl.BlockSpec((ROW_BLK, ROUTE_ROWS), lambda i, lo, hi, v: (i, 0)),
                  pl.BlockSpec(memory_space=pl.ANY),
                  by_lo(wg), by_lo(wu), by_lo(wd), by_hi(wg), by_hi(wu), by_hi(wd)],
        out_specs=pl.BlockSpec((ROW_BLK, TOKEN_SUB, LANES), lambda i, lo, hi, v: (i, 0, 0)),
        scratch_shapes=[pltpu.VMEM((ROW_BLK, TOKEN_SUB, LANES), F32), pltpu.SemaphoreType.DMA(())],
    )
    return pl.pallas_call(
        _moe_expert_kernel,
        grid_spec=grid_spec,
        out_shape=jax.ShapeDtypeStruct((n_tiles * ROW_BLK, TOKEN_SUB, LANES), F32),
        compiler_params=_params("arbitrary"),
        name="moe_experts",
    )(tile_lo, tile_hi, tile_valid, src, w_sorted, h2, wg, wu, wd, wg, wu, wd)


def _moe_combine_kernel(pos_ref, y_hbm, x1_ref, mod_ref, o_ref, buf, sem):
    d = x1_ref.shape[-1]
    _gather_tokens(pos_ref, y_hbm, buf, sem)
    o_ref[...] = x1_ref[...] + mod_ref[0][:, 5 * d:6 * d] * _load_token_tiles(buf)


def _moe_combine(plan, y_sorted, x1, mod, batch, ctx_row, with_ctx):
    pos = plan[-1]
    n, d = x1.shape
    nblk = n // ROW_BLK
    bpb = nblk // batch

    def mod_idx(i):
        return ((jnp.where(i % bpb == 0, ctx_row, i // bpb) if with_ctx else i // bpb), 0, 0)

    return pl.pallas_call(
        _moe_combine_kernel,
        grid=(nblk,),
        in_specs=[pl.BlockSpec((1, 1, ROW_BLK), lambda i: (i, 0, 0), memory_space=pltpu.SMEM),
                  pl.BlockSpec(memory_space=pl.ANY),
                  pl.BlockSpec((ROW_BLK, d), lambda i: (i, 0)),
                  pl.BlockSpec((1, 1, mod.shape[-1]), mod_idx)],
        out_specs=pl.BlockSpec((ROW_BLK, d), lambda i: (i, 0)),
        out_shape=jax.ShapeDtypeStruct((n, d), F32),
        scratch_shapes=[pltpu.VMEM((ROW_BLK, TOKEN_SUB, LANES), F32), pltpu.SemaphoreType.DMA(())],
        compiler_params=_params("arbitrary"),
        name="moe_combine",
    )(pos, y_sorted, x1, mod)


def kernel(x, c, ctx, c_ctx, ada_w, ada_b, norm_mix, norm_ffn, w_in, hgrn_lb_logits, hgrn_norm, conv_w,
           ret_decay_logit, ret_norm, q_norm, k_norm, w_out, router_group_w, router_group_b, router_expert_w,
           router_expert_b, expert_w_gate, expert_w_up, expert_w_down):
    batch, seq_len, d = x.shape
    ctx_len = ctx.shape[1]
    depth = ada_w.shape[0]
    n_experts = expert_w_gate.shape[1]
    n_groups = router_group_w.shape[-1]
    assert ctx_len == ROW_BLK and seq_len % ROW_BLK == 0 and batch + 1 <= MOD_ROWS
    assert n_experts + n_groups <= ROUTER_ROWS and n_experts == n_groups * EXPERTS_PER_GROUP
    assert d == TOKEN_SUB * LANES
    rows = ctx_len + seq_len
    bpb = rows // ROW_BLK
    ctx_row = batch

    cv = jnp.concatenate([c, c_ctx[None], jnp.zeros((MOD_ROWS - batch - 1, d), F32)], axis=0)
    mod_all = _modulation(cv, ada_w, ada_b)
    cos, sin_signed = _rope_tables(ctx_len, seq_len)
    tokens = jnp.concatenate([ctx, x], axis=1).reshape(batch * rows, d)

    for layer in range(depth):
        with_ctx = layer < depth - 1
        out_rows = rows if with_ctx else seq_len
        mod = mod_all[layer].reshape(MOD_ROWS, 1, 6 * d)
        pa, pb, pc, pd = _inproj(tokens, mod, norm_mix[layer], w_in[layer].astype(BF16), bpb, ctx_row)
        mixers = (
            _hgrn(pa, hgrn_lb_logits, hgrn_norm[layer], layer, batch, out_rows),
            _conv(pb, conv_w[layer], batch, ctx_len, out_rows),
            _retention(pc, ret_decay_logit[layer], ret_norm[layer], batch, out_rows),
            _attention(pd, cos, sin_signed, q_norm[layer], k_norm[layer], batch, ctx_len, out_rows),
        )
        pad = ROUTER_ROWS - n_experts - n_groups
        w_router_t = jnp.concatenate([router_expert_w[layer].T, router_group_w[layer].T, jnp.zeros((pad, d), F32)])
        b_router = jnp.concatenate([router_expert_b[layer], router_group_b[layer], jnp.zeros((pad,), F32)])
        b_router_t = jnp.broadcast_to(b_router[:, None], (ROUTER_ROWS, ROW_BLK))
        x1, h2, rt = _outproj(tokens, mixers, mod, norm_ffn[layer], w_out[layer].astype(BF16), w_router_t,
                              b_router_t, n_experts, batch, ctx_row)
        plan = _moe_plan(rt, n_groups * PAIRS_PER_GROUP)
        y_sorted = _moe_experts(plan, h2, expert_w_gate[layer].astype(BF16), expert_w_up[layer].astype(BF16),
                                expert_w_down[layer].astype(BF16))
        tokens = _moe_combine(plan, y_sorted, x1, mod, batch, ctx_row, with_ctx)
    return tokens.reshape(batch, seq_len, d)
```

```python
import functools

import jax
import jax.numpy as jnp
from jax import lax
from jax.experimental import pallas as pl
from jax.experimental.pallas import tpu as pltpu

F32 = jnp.float32
BF16 = jnp.bfloat16
HIGHEST = lax.Precision.HIGHEST

HEAD_DIM = 64
GROUP_W = 256
N_HEADS = GROUP_W // HEAD_DIM
KV_W = 128
GRID_W = 64
ROPE_THETA = 10000.0
EXPERTS_PER_GROUP = 4
NORM_EPS = 1e-6
ROW_BLK = 256
SUB_BLK = 8
LEVEL_ROWS = 128
LANES = 128
MOD_ROWS = 24
ROUTER_ROWS = 32
ROUTE_ROWS = 8
PAIRS_PER_GROUP = EXPERTS_PER_GROUP * (EXPERTS_PER_GROUP - 1) // 2
TOKEN_SUB = 8
VMEM_LIMIT_BYTES = 56 * 1024 * 1024
PA_W, PB_W, PC_W, PD_W = 5 * GROUP_W, 3 * GROUP_W, 4 * GROUP_W, GROUP_W + 2 * KV_W


def _params(*semantics):
    return pltpu.CompilerParams(dimension_semantics=semantics, vmem_limit_bytes=VMEM_LIMIT_BYTES)


def _dot(a, b):
    return jnp.dot(a, b, preferred_element_type=F32)


def _dot_nt(a, b):
    return lax.dot_general(a, b, (((1,), (1,)), ((), ())), preferred_element_type=F32)


def _dot_tn(a, b):
    return lax.dot_general(a, b, (((0,), (0,)), ((), ())), preferred_element_type=F32)


def _recip(x):
    return pl.reciprocal(x, approx=True)


def _sigmoid(x):
    return _recip(1.0 + jnp.exp(-x))


def _silu(x):
    return x * _sigmoid(x)


def _log_sigmoid(x):
    return jnp.minimum(x, 0.0) - jnp.log(1.0 + jnp.exp(-jnp.abs(x)))


def _split_bf16(x, terms):
    parts = []
    rem = x
    for i in range(terms):
        p = rem.astype(BF16)
        parts.append(p)
        if i + 1 < terms:
            rem = rem - p.astype(F32)
    return parts


def _head_ones(width):
    r = lax.broadcasted_iota(jnp.int32, (width, width), 0) // HEAD_DIM
    c = lax.broadcasted_iota(jnp.int32, (width, width), 1) // HEAD_DIM
    return r == c


def _head_masks(width):
    lane_head = lax.broadcasted_iota(jnp.int32, (1, width), 1) // HEAD_DIM
    return [lane_head == h for h in range(width // HEAD_DIM)]


def _head_expand(x, hmasks):
    zero = jnp.zeros_like(x)
    return jnp.concatenate([jnp.where(hm, x, zero) for hm in hmasks], axis=0).astype(BF16)


def _head_mean_sq(x, ones_bf16):
    hi, lo = _split_bf16(x * x, 2)
    return (_dot(hi, ones_bf16) + _dot(lo, ones_bf16)) * (1.0 / HEAD_DIM)


def _head_rms_norm(x, gain, ones_bf16):
    return x * lax.rsqrt(_head_mean_sq(x, ones_bf16) + NORM_EPS) * gain


def _norm_modulate(x, gain, mod, idx):
    d = x.shape[-1]
    ms = jnp.mean(x * x, axis=-1, keepdims=True)
    y = x * lax.rsqrt(ms + NORM_EPS) * gain
    shift = mod[:, idx * d:(idx + 1) * d]
    scale = mod[:, (idx + 1) * d:(idx + 2) * d]
    return y * (1.0 + scale) + shift


def _scan_rows(j, nchunk):
    rf = pl.multiple_of(j * ROW_BLK, ROW_BLK)
    rb = pl.multiple_of(jnp.where(j == 0, 0, nchunk - j) * ROW_BLK, ROW_BLK)
    return rf, rb


def _mod_kernel(cv_ref, w_ref, b_ref, o_ref):
    s = _silu(cv_ref[...])
    o_ref[0] = jnp.dot(s, w_ref[0], precision=HIGHEST, preferred_element_type=F32) + b_ref[0]


def _modulation(cv, ada_w, ada_b):
    depth, d, n = ada_w.shape
    tn = n // 4
    return pl.pallas_call(
        _mod_kernel,
        grid=(depth, n // tn),
        in_specs=[pl.BlockSpec((MOD_ROWS, d), lambda l, j: (0, 0)),
                  pl.BlockSpec((1, d, tn), lambda l, j: (l, 0, j)),
                  pl.BlockSpec((1, 1, tn), lambda l, j: (l, 0, j))],
        out_specs=pl.BlockSpec((1, MOD_ROWS, tn), lambda l, j: (l, 0, j)),
        out_shape=jax.ShapeDtypeStruct((depth, MOD_ROWS, n), F32),
        compiler_params=_params("parallel", "parallel"),
        name="adaln_mod",
    )(cv, ada_w, ada_b.reshape(depth, 1, n))


def _inproj_kernel(x_ref, mod_ref, gain_ref, w_ref, pa_ref, pb_ref, pc_ref, pd_ref):
    h = _norm_modulate(x_ref[...], gain_ref[...], mod_ref[0], 0)
    p = _dot(h.astype(BF16), w_ref[...])
    pa_ref[...] = p[:, 0:PA_W]
    pb_ref[...] = p[:, PA_W:PA_W + PB_W].astype(pb_ref.dtype)
    pc_ref[...] = p[:, PA_W + PB_W:PA_W + PB_W + PC_W].astype(pc_ref.dtype)
    pd_ref[...] = p[:, PA_W + PB_W + PC_W:].astype(pd_ref.dtype)


def _inproj(tokens, mod, gain, w_bf16, blocks_per_batch, ctx_row):
    n, d = tokens.shape
    nblk = n // ROW_BLK

    def mod_idx(i):
        return (jnp.where(i % blocks_per_batch == 0, ctx_row, i // blocks_per_batch), 0, 0)

    outs = ((PA_W, F32), (PB_W, BF16), (PC_W, BF16), (PD_W, BF16))
    return pl.pallas_call(
        _inproj_kernel,
        grid=(nblk,),
        in_specs=[pl.BlockSpec((ROW_BLK, d), lambda i: (i, 0)),
                  pl.BlockSpec((1, 1, mod.shape[-1]), mod_idx),
                  pl.BlockSpec((1, d), lambda i: (0, 0)),
                  pl.BlockSpec(w_bf16.shape, lambda i: (0, 0))],
        out_specs=[pl.BlockSpec((ROW_BLK, w), lambda i: (i, 0)) for w, _ in outs],
        out_shape=[jax.ShapeDtypeStruct((n, w), dt) for w, dt in outs],
        compiler_params=_params("parallel"),
        name="in_proj",
    )(tokens, mod, gain.reshape(1, d), w_bf16)


def _hgrn_prepare(blk, lb_row, z_col, anti, s_ref, scr, consts):
    tril, triu, bmask, hmasks, level_masks = consts
    c_scr, q_scr, k_scr, v_scr = scr
    w = GROUP_W
    c_len = blk.shape[0]
    q = blk[:, 0:w]
    v = blk[:, w:2 * w]
    z = blk[:, z_col * w:(z_col + 1) * w]
    f = lb_row + (1.0 - lb_row) * _sigmoid(z)
    g = jnp.log(f)
    k = 1.0 - f
    tri = triu if anti else tril
    c = sum(_dot(tri, part) for part in _split_bf16(g, 3))
    tot = c[0:1] if anti else c[c_len - 1:c_len]
    c_scr[...] = c
    q_scr[...] = q
    k_scr[...] = k
    v_scr[...] = v

    s_t = s_ref[...]
    o = _dot_nt((q * jnp.exp(c)).astype(BF16), s_t.astype(BF16))
    k_end = (k * jnp.exp(tot - c)).astype(BF16)
    s_ref[...] = s_t * jnp.exp(tot) + jnp.where(bmask, _dot_tn(v.astype(BF16), k_end), 0.0)

    v_exp = {}
    m = c_len // 2
    while m >= SUB_BLK:
        nb = c_len // (2 * m)
        mid = m if anti else m - 1
        refs = [jnp.broadcast_to(c[b * 2 * m + mid:b * 2 * m + mid + 1], (2 * m, w)) for b in range(nb)]
        ref = refs[0] if nb == 1 else jnp.concatenate(refs, axis=0)
        e = jnp.exp(-jnp.abs(c - ref))
        qt = q * e
        kt = k * e
        rg = max(LEVEL_ROWS, 2 * m)
        pieces = []
        for r0 in range(0, c_len, rg):
            if (rg, r0) not in v_exp:
                v_exp[(rg, r0)] = _head_expand(v[r0:r0 + rg], hmasks)
            a = _dot_nt(qt[r0:r0 + rg].astype(BF16), _head_expand(kt[r0:r0 + rg], hmasks))
            a = jnp.where(level_masks[(m, anti)], a, 0.0).astype(BF16)
            pieces.append(_dot(a, v_exp[(rg, r0)]))
        o = o + (pieces[0] if len(pieces) == 1 else jnp.concatenate(pieces, axis=0))
        m //= 2
    return o


def _hgrn_level_masks(c_len):
    masks = {}
    m = c_len // 2
    while m >= SUB_BLK:
        rg = max(LEVEL_ROWS, 2 * m)
        t = lax.broadcasted_iota(jnp.int32, (rg, N_HEADS * rg), 0)
        s = lax.broadcasted_iota(jnp.int32, (rg, N_HEADS * rg), 1) % rg
        same = (t // (2 * m)) == (s // (2 * m))
        t_late = (t // m) % 2 == 1
        s_late = (s // m) % 2 == 1
        masks[(m, False)] = same & t_late & ~s_late
        masks[(m, True)] = same & ~t_late & s_late
        m //= 2
    return masks


def _hgrn_diag_unit(scr, r, anti, ones_bf16):
    c_scr, q_scr, k_scr, v_scr = scr
    cs = c_scr[pl.ds(r, SUB_BLK), :]
    qs = q_scr[pl.ds(r, SUB_BLK), :]
    ks = k_scr[pl.ds(r, SUB_BLK), :]
    vs = v_scr[pl.ds(r, SUB_BLK), :]
    row = lax.broadcasted_iota(jnp.int32, cs.shape, 0)
    prods = []
    for s in range(SUB_BLK):
        valid = (row <= s) if anti else (row >= s)
        dec = jnp.exp(jnp.where(valid, cs - cs[s:s + 1], -jnp.inf))
        prods.append(qs * dec * ks[s:s + 1])
    scores = _dot(jnp.concatenate(prods, axis=0).astype(BF16), ones_bf16)
    od = scores[0:SUB_BLK] * vs[0:1]
    for s in range(1, SUB_BLK):
        od = od + scores[s * SUB_BLK:(s + 1) * SUB_BLK] * vs[s:s + 1]
    return od


def _hgrn_kernel(p_ref, lbl_ref, gain_ref, o_ref, acc, s_f, s_b, od_f, od_b, *scr, layer, out_off):
    w = GROUP_W
    c_len = ROW_BLK
    nchunk = p_ref.shape[0] // c_len
    depth = lbl_ref.shape[0]
    scr_f, scr_b = scr[:4], scr[4:]

    logits = [lbl_ref[l] for l in range(depth)]
    mx = functools.reduce(jnp.maximum, logits)
    exps = [jnp.exp(l - mx) for l in logits]
    lb = sum(exps[1:layer + 1], jnp.zeros_like(mx)) * _recip(sum(exps))

    ri = lax.broadcasted_iota(jnp.int32, (c_len, c_len), 0)
    ci = lax.broadcasted_iota(jnp.int32, (c_len, c_len), 1)
    tril = (ci <= ri).astype(BF16)
    triu = (ci >= ri).astype(BF16)
    bmask = _head_ones(w)
    ones_bf16 = bmask.astype(BF16)
    consts = (tril, triu, bmask, _head_masks(w), _hgrn_level_masks(c_len))

    acc[...] = jnp.zeros_like(acc)
    s_f[...] = jnp.zeros_like(s_f)
    s_b[...] = jnp.zeros_like(s_b)

    def step(j, carry):
        rf, rb = _scan_rows(j, nchunk)
        of = _hgrn_prepare(p_ref[pl.ds(rf, c_len), :], lb[0:1], 2, False, s_f, scr_f, consts)
        ob = _hgrn_prepare(p_ref[pl.ds(rb, c_len), :], lb[1:2], 3, True, s_b, scr_b, consts)

        def diag(i, c2):
            r = pl.multiple_of(i * SUB_BLK, SUB_BLK)
            od_f[pl.ds(r, SUB_BLK), :] = _hgrn_diag_unit(scr_f, r, False, ones_bf16)
            od_b[pl.ds(r, SUB_BLK), :] = _hgrn_diag_unit(scr_b, r, True, ones_bf16)
            return c2

        lax.fori_loop(0, c_len // SUB_BLK, diag, 0, unroll=4)
        acc[pl.ds(rf, c_len), :] += of + od_f[...]
        acc[pl.ds(rb, c_len), :] += ob + od_b[...]
        return carry

    lax.fori_loop(0, nchunk, step, 0)

    o = acc[out_off:, :]
    gate = p_ref[out_off:, 4 * w:5 * w]
    o_ref[...] = (_head_rms_norm(o, gain_ref[...], ones_bf16) * _silu(gate)).astype(o_ref.dtype)


def _hgrn(pa, lb_logits, gain, layer, batch, out_rows):
    n, width = pa.shape
    rows = n // batch
    w = GROUP_W
    scr = lambda r: pltpu.VMEM((r, w), F32)
    return pl.pallas_call(
        functools.partial(_hgrn_kernel, layer=layer, out_off=rows - out_rows),
        grid=(batch,),
        in_specs=[pl.BlockSpec((rows, width), lambda b: (b, 0)),
                  pl.BlockSpec(lb_logits.shape, lambda b: (0, 0, 0)),
                  pl.BlockSpec((1, w), lambda b: (0, 0))],
        out_specs=pl.BlockSpec((out_rows, w), lambda b: (b, 0)),
        out_shape=jax.ShapeDtypeStruct((batch * out_rows, w), BF16),
        scratch_shapes=[scr(rows), scr(w), scr(w)] + [scr(ROW_BLK)] * 10,
        compiler_params=_params("parallel"),
        name="hgrn2_mixer",
    )(pa, lb_logits, gain.reshape(1, w))


def _conv_kernel(p_ref, w_ref, o_ref, *, ctx_len, out_off):
    w = GROUP_W
    p = p_ref[...].astype(F32)
    u = p[:, w:2 * w] * p[:, 2 * w:3 * w]
    n = u.shape[0]
    row = lax.broadcasted_iota(jnp.int32, u.shape, 0)
    prev = jnp.where((row == 0) | (row == ctx_len), 0.0, pltpu.roll(u, 1, 0))
    nxt = jnp.where((row == ctx_len - 1) | (row == n - 1), 0.0, pltpu.roll(u, n - 1, 0))
    cw = w_ref[...]
    y = p[:, 0:w] * (cw[0:1] * prev + cw[1:2] * u + cw[2:3] * nxt)
    o_ref[...] = y[out_off:].astype(o_ref.dtype)


def _conv(pb, conv_w, batch, ctx_len, out_rows):
    n, width = pb.shape
    rows = n // batch
    return pl.pallas_call(
        functools.partial(_conv_kernel, ctx_len=ctx_len, out_off=rows - out_rows),
        grid=(batch,),
        in_specs=[pl.BlockSpec((rows, width), lambda b: (b, 0)),
                  pl.BlockSpec(conv_w.shape, lambda b: (0, 0))],
        out_specs=pl.BlockSpec((out_rows, GROUP_W), lambda b: (b, 0)),
        out_shape=jax.ShapeDtypeStruct((batch * out_rows, GROUP_W), BF16),
        compiler_params=_params("parallel"),
        name="conv_mixer",
    )(pb, conv_w)


def _ret_kernel(p_ref, dl_ref, gain_ref, o_ref, acc, s_f, s_b, d2_scr, *, out_off):
    w = GROUP_W
    c_len = ROW_BLK
    nchunk = p_ref.shape[0] // c_len
    hmasks = _head_masks(w)
    bmask = _head_ones(w)
    ones_bf16 = bmask.astype(BF16)

    lg = _log_sigmoid(dl_ref[...])
    lane_w = lambda r: sum(jnp.where(hm, lg[r:r + 1, h:h + 1], 0.0) for h, hm in enumerate(hmasks))
    lgf, lgb = lane_w(0), lane_w(1)
    t = lax.broadcasted_iota(jnp.int32, (c_len, w), 0).astype(F32)
    scale = HEAD_DIM ** -0.5
    qf_dec = jnp.exp(lgf * (t + 1.0)) * scale
    kf_dec = jnp.exp(lgf * (c_len - 1.0 - t))
    qb_dec = jnp.exp(lgb * (c_len - t)) * scale
    kb_dec = jnp.exp(lgb * t)
    tot_f = jnp.exp(lgf * float(c_len))
    tot_b = jnp.exp(lgb * float(c_len))

    col = lax.broadcasted_iota(jnp.int32, (c_len, N_HEADS * c_len), 1)
    row = lax.broadcasted_iota(jnp.int32, (c_len, N_HEADS * c_len), 0)
    col_head = lax.broadcasted_iota(jnp.int32, (1, N_HEADS * c_len), 1) // c_len
    col_lg = lambda r: sum(jnp.where(col_head == h, lg[r:r + 1, h:h + 1], 0.0) for h in range(N_HEADS))
    dist = (row - col % c_len).astype(F32)
    d2_scr[...] = (jnp.where(dist >= 0.0, jnp.exp(col_lg(0) * jnp.maximum(dist, 0.0)), 0.0)
                   + jnp.where(dist <= 0.0, jnp.exp(col_lg(1) * jnp.maximum(-dist, 0.0)), 0.0))

    acc[...] = jnp.zeros_like(acc)
    s_f[...] = jnp.zeros_like(s_f)
    s_b[...] = jnp.zeros_like(s_b)

    def inter(q, k, v, q_dec, k_dec, tot, s_ref):
        s_t = s_ref[...]
        o = _dot_nt((q * q_dec).astype(BF16), s_t.astype(BF16))
        s_ref[...] = s_t * tot + jnp.where(bmask, _dot_tn(v, (k * k_dec).astype(BF16)), 0.0)
        return o

    def step(j, carry):
        rf, rb = _scan_rows(j, nchunk)
        blk = p_ref[pl.ds(rf, c_len), :]
        q = blk[:, 0:w].astype(F32)
        k = blk[:, w:2 * w]
        v = blk[:, 2 * w:3 * w]
        a = _dot_nt((q * scale).astype(BF16), _head_expand(k, hmasks)) * d2_scr[...]
        o = _dot(a.astype(BF16), _head_expand(v, hmasks))
        o = o + inter(q, k.astype(F32), v, qf_dec, kf_dec, tot_f, s_f)
        acc[pl.ds(rf, c_len), :] += o
        blk = p_ref[pl.ds(rb, c_len), :]
        acc[pl.ds(rb, c_len), :] += inter(blk[:, 0:w].astype(F32), blk[:, w:2 * w].astype(F32),
                                          blk[:, 2 * w:3 * w], qb_dec, kb_dec, tot_b, s_b)
        return carry

    lax.fori_loop(0, nchunk, step, 0)

    o = acc[out_off:, :]
    gate = p_ref[out_off:, 3 * w:4 * w].astype(F32)
    o_ref[...] = (_head_rms_norm(o, gain_ref[...], ones_bf16) * _silu(gate)).astype(o_ref.dtype)


def _retention(pc, decay_logit, gain, batch, out_rows):
    n, width = pc.shape
    rows = n // batch
    w = GROUP_W
    dl = jnp.zeros((8, LANES), F32).at[:decay_logit.shape[0], :decay_logit.shape[1]].set(decay_logit)
    return pl.pallas_call(
        functools.partial(_ret_kernel, out_off=rows - out_rows),
        grid=(batch,),
        in_specs=[pl.BlockSpec((rows, width), lambda b: (b, 0)),
                  pl.BlockSpec((8, LANES), lambda b: (0, 0)),
                  pl.BlockSpec((1, w), lambda b: (0, 0))],
        out_specs=pl.BlockSpec((out_rows, w), lambda b: (b, 0)),
        out_shape=jax.ShapeDtypeStruct((batch * out_rows, w), BF16),
        scratch_shapes=[pltpu.VMEM((rows, w), F32), pltpu.VMEM((w, w), F32), pltpu.VMEM((w, w), F32),
                        pltpu.VMEM((ROW_BLK, N_HEADS * ROW_BLK), F32)],
        compiler_params=_params("parallel"),
        name="retention_mixer",
    )(pc, dl, gain.reshape(1, w))


def _rope(x, cos, sin_signed):
    n = x.shape[-1]
    lane = lax.broadcasted_iota(jnp.int32, x.shape, 1)
    swapped = jnp.where(lane % 2 == 0, pltpu.roll(x, n - 1, 1), pltpu.roll(x, 1, 1))
    return x * cos + swapped * sin_signed


def _attn_kernel(q_ref, k_ref, v_ref, cos_ref, sin_ref, qn_ref, kn_ref, o_ref, k_buf, *, ctx_len, qb_off):
    j = pl.program_id(1)
    rows = k_ref.shape[0]

    @pl.when(j == 0)
    def _():
        ones_kv = _head_ones(KV_W).astype(BF16)
        kn = _head_rms_norm(k_ref[...].astype(F32), kn_ref[...], ones_kv)
        k_buf[...] = _rope(kn, cos_ref[:, 0:KV_W], sin_ref[:, 0:KV_W]).astype(BF16)

    qb = j + qb_off
    r0 = pl.multiple_of(qb * ROW_BLK, ROW_BLK)
    ones_q = _head_ones(GROUP_W).astype(BF16)
    qn = _head_rms_norm(q_ref[...].astype(F32), qn_ref[...], ones_q)
    qr = _rope(qn, cos_ref[pl.ds(r0, ROW_BLK), :], sin_ref[pl.ds(r0, ROW_BLK), :])
    qr = (qr * (HEAD_DIM ** -0.5)).astype(BF16)
    kidx = lax.broadcasted_iota(jnp.int32, (1, rows), 1)
    visible = kidx < jnp.where(qb == 0, ctx_len, rows)
    group = GROUP_W // KV_W
    for kv in range(KV_W // HEAD_DIM):
        ksl = slice(kv * HEAD_DIM, (kv + 1) * HEAD_DIM)
        q2 = jnp.concatenate([qr[:, (kv * group + g) * HEAD_DIM:(kv * group + g + 1) * HEAD_DIM]
                              for g in range(group)], axis=0)
        s = jnp.where(visible, _dot_nt(q2, k_buf[:, ksl]), -jnp.inf)
        p = jnp.exp(s - jnp.max(s, axis=-1, keepdims=True))
        o2 = _dot(p.astype(BF16), v_ref[:, ksl]) * _recip(jnp.sum(p, axis=-1, keepdims=True))
        for g in range(group):
            h = kv * group + g
            o_ref[:, h * HEAD_DIM:(h + 1) * HEAD_DIM] = o2[g * ROW_BLK:(g + 1) * ROW_BLK].astype(o_ref.dtype)


def _attention(pd, cos, sin_signed, q_norm, k_norm, batch, ctx_len, out_rows):
    n, width = pd.shape
    rows = n // batch
    bpb = rows // ROW_BLK
    nqb = out_rows // ROW_BLK
    qb_off = bpb - nqb
    kcol = GROUP_W // KV_W
    tile = lambda g, reps: jnp.tile(g.reshape(1, HEAD_DIM), (1, reps))
    return pl.pallas_call(
        functools.partial(_attn_kernel, ctx_len=ctx_len, qb_off=qb_off),
        grid=(batch, nqb),
        in_specs=[pl.BlockSpec((ROW_BLK, GROUP_W), lambda b, j: (b * bpb + j + qb_off, 0)),
                  pl.BlockSpec((rows, KV_W), lambda b, j: (b, kcol)),
                  pl.BlockSpec((rows, KV_W), lambda b, j: (b, kcol + 1)),
                  pl.BlockSpec((rows, GROUP_W), lambda b, j: (0, 0)),
                  pl.BlockSpec((rows, GROUP_W), lambda b, j: (0, 0)),
                  pl.BlockSpec((1, GROUP_W), lambda b, j: (0, 0)),
                  pl.BlockSpec((1, KV_W), lambda b, j: (0, 0))],
        out_specs=pl.BlockSpec((ROW_BLK, GROUP_W), lambda b, j: (b * nqb + j, 0)),
        out_shape=jax.ShapeDtypeStruct((batch * out_rows, GROUP_W), BF16),
        scratch_shapes=[pltpu.VMEM((rows, KV_W), BF16)],
        compiler_params=_params("parallel", "arbitrary"),
        name="attention_mixer",
    )(pd, pd, pd, cos, sin_signed, tile(q_norm, N_HEADS), tile(k_norm, KV_W // HEAD_DIM))


def _rope_tables(ctx_len, seq_len):
    t = jnp.arange(seq_len)
    row = (t // GRID_W).astype(F32)
    col = (t % GRID_W).astype(F32)
    n_freq = HEAD_DIM // 4
    inv_freq = ROPE_THETA ** (-jnp.arange(n_freq, dtype=F32) / n_freq)
    ang = jnp.concatenate([row[:, None] * inv_freq, col[:, None] * inv_freq], axis=-1)
    cos = jnp.repeat(jnp.cos(ang), 2, axis=-1)
    sin = jnp.stack([-jnp.sin(ang), jnp.sin(ang)], axis=-1).reshape(seq_len, HEAD_DIM)
    cos = jnp.concatenate([jnp.ones((ctx_len, HEAD_DIM), F32), cos], axis=0)
    sin = jnp.concatenate([jnp.zeros((ctx_len, HEAD_DIM), F32), sin], axis=0)
    return jnp.tile(cos, (1, N_HEADS)), jnp.tile(sin, (1, N_HEADS))


def _route(lt, n_experts):
    n_groups = n_experts // EXPERTS_PER_GROUP
    first = lambda x, hit, n: jnp.min(jnp.where(hit, x, float(n)), axis=0, keepdims=True)

    gl = lt[n_experts:n_experts + n_groups]
    gexp = jnp.exp(gl - jnp.max(gl, axis=0, keepdims=True))
    gprob = gexp * _recip(jnp.sum(gexp, axis=0, keepdims=True))
    group_p = jnp.max(gprob, axis=0, keepdims=True)
    grow = lax.broadcasted_iota(jnp.int32, gl.shape, 0).astype(F32)
    gidx = first(grow, gprob == group_p, n_groups)

    el = sum(jnp.where(gidx == float(g), lt[g * EXPERTS_PER_GROUP:(g + 1) * EXPERTS_PER_GROUP], 0.0)
             for g in range(n_groups))
    eexp = jnp.exp(el - jnp.max(el, axis=0, keepdims=True))
    eprob = eexp * _recip(jnp.sum(eexp, axis=0, keepdims=True))
    erow = lax.broadcasted_iota(jnp.int32, el.shape, 0).astype(F32)
    p1 = jnp.max(eprob, axis=0, keepdims=True)
    i1 = first(erow, eprob == p1, EXPERTS_PER_GROUP)
    rest = erow != i1
    p2 = jnp.max(jnp.where(rest, eprob, -1.0), axis=0, keepdims=True)
    i2 = first(erow, rest & (eprob == p2), EXPERTS_PER_GROUP)
    scale = group_p * _recip(p1 + p2)
    lo = jnp.minimum(i1, i2)
    hi = jnp.maximum(i1, i2)
    w_lo = jnp.where(i1 < i2, p1, p2) * scale
    w_hi = jnp.where(i1 < i2, p2, p1) * scale
    pair = lo * (2.0 * EXPERTS_PER_GROUP - 1.0 - lo) * 0.5 + (hi - lo - 1.0)
    bucket = gidx * float(PAIRS_PER_GROUP) + pair
    return jnp.concatenate([bucket, w_lo, w_hi, jnp.zeros((LANES - 3, lt.shape[1]), F32)], axis=0)


def _outproj_kernel(x_ref, a_ref, b_ref, c_ref, d_ref, mod_ref, gain_ref, w_ref, wr_ref, br_ref,
                    x1_ref, h2_ref, rt_ref, *, n_experts):
    d = x_ref.shape[-1]
    mix = jnp.concatenate([a_ref[...], b_ref[...], c_ref[...], d_ref[...]], axis=1)
    mod = mod_ref[0]
    x1 = x_ref[...] + mod[:, 2 * d:3 * d] * _dot(mix, w_ref[...])
    x1_ref[...] = x1
    h2 = _norm_modulate(x1, gain_ref[...], mod, 3)
    _store_token_tiles(h2_ref, h2)
    lt = lax.dot_general(wr_ref[...], h2, (((1,), (1,)), ((), ())), precision=HIGHEST,
                         preferred_element_type=F32) + br_ref[...]
    rt_ref[...] = _route(lt, n_experts).T


def _outproj(tokens, mixers, mod, gain, w_bf16, w_router_t, b_router_t, n_experts, batch, ctx_row):
    d = tokens.shape[1]
    n_out = mixers[0].shape[0]
    nblk = n_out // ROW_BLK
    out_bpb = nblk // batch
    tok_bpb = tokens.shape[0] // ROW_BLK // batch
    off = tok_bpb - out_bpb
    with_ctx = off == 0

    def mod_idx(i):
        b = i // out_bpb
        return ((jnp.where(i % out_bpb == 0, ctx_row, b) if with_ctx else b), 0, 0)

    row_spec = lambda w: pl.BlockSpec((ROW_BLK, w), lambda i: (i, 0))
    full = lambda a: pl.BlockSpec(a.shape, lambda i: (0,) * a.ndim)
    return pl.pallas_call(
        functools.partial(_outproj_kernel, n_experts=n_experts),
        grid=(nblk,),
        in_specs=[pl.BlockSpec((ROW_BLK, d), lambda i: ((i // out_bpb) * tok_bpb + i % out_bpb + off, 0)),
                  row_spec(GROUP_W), row_spec(GROUP_W), row_spec(GROUP_W), row_spec(GROUP_W),
                  pl.BlockSpec((1, 1, mod.shape[-1]), mod_idx),
                  pl.BlockSpec((1, d), lambda i: (0, 0)),
                  full(w_bf16), full(w_router_t), full(b_router_t)],
        out_specs=[row_spec(d),
                   pl.BlockSpec((ROW_BLK, TOKEN_SUB, LANES), lambda i: (i, 0, 0)),
                   row_spec(LANES)],
        out_shape=[jax.ShapeDtypeStruct((n_out, d), F32),
                   jax.ShapeDtypeStruct((n_out, TOKEN_SUB, LANES), F32),
                   jax.ShapeDtypeStruct((n_out, LANES), F32)],
        compiler_params=_params("parallel"),
        name="out_proj_router",
    )(tokens, *mixers, mod, gain.reshape(1, d), w_bf16, w_router_t, b_router_t)


def _store_token_tiles(ref, x):
    for j in range(TOKEN_SUB):
        ref[:, j, :] = x[:, j * LANES:(j + 1) * LANES]


def _load_token_tiles(ref):
    return jnp.concatenate([ref[:, j, :] for j in range(TOKEN_SUB)], axis=-1)


def _token_dmas(n, make_copy):
    def start(r, carry):
        make_copy(r).start()
        return carry

    def wait(r, carry):
        make_copy(r).wait()
        return carry

    lax.fori_loop(0, n, start, 0, unroll=8)
    lax.fori_loop(0, n, wait, 0, unroll=8)


def _pack_bf16_pair(a, b):
    au = lax.bitcast_convert_type(a.astype(BF16).astype(F32), jnp.uint32)
    bu = lax.bitcast_convert_type(b.astype(BF16).astype(F32), jnp.uint32)
    return au | (bu >> 16)


def _unpack_bf16_pair(u):
    a = lax.bitcast_convert_type(u & jnp.uint32(0xFFFF0000), F32)
    b = lax.bitcast_convert_type(u << 16, F32)
    return a, b


def _moe_plan(route, n_buckets):
    n = route.shape[0]
    nt = n // ROW_BLK
    n_tiles = nt + n_buckets
    bucket = route[:, 0].astype(jnp.int32)
    onehot = (bucket[:, None] == jnp.arange(n_buckets, dtype=jnp.int32)[None, :]).astype(F32)
    onehot = onehot.reshape(nt, ROW_BLK, n_buckets)
    tile_counts = jnp.sum(onehot, axis=1)
    before = jnp.cumsum(tile_counts, axis=0) - tile_counts
    earlier = (jnp.arange(ROW_BLK)[:, None] > jnp.arange(ROW_BLK)[None, :]).astype(F32)
    rank = jnp.einsum('ij,tjb->tib', earlier, onehot) + before[:, None, :]
    counts = jnp.sum(tile_counts, axis=0)
    tiles = jnp.ceil(counts / ROW_BLK)
    tend = jnp.cumsum(tiles)
    tstart = tend - tiles
    pos = jnp.sum(onehot * (rank + tstart * ROW_BLK), axis=-1).astype(jnp.int32).reshape(n)
    tile_id = jnp.arange(n_tiles, dtype=F32)
    tile_valid = (tile_id < tend[-1]).astype(jnp.int32)
    tile_bucket = jnp.sum((tile_id[:, None] >= tend[None, :]).astype(jnp.int32), axis=1)
    last_bucket = jnp.max(jnp.where(tiles > 0, jnp.arange(n_buckets, dtype=jnp.int32), 0))
    tile_bucket = jnp.minimum(tile_bucket, last_bucket)
    group = tile_bucket // PAIRS_PER_GROUP
    pair = tile_bucket % PAIRS_PER_GROUP
    lo = (pair >= 3).astype(jnp.int32) + (pair >= 5).astype(jnp.int32)
    hi = jnp.where(pair < 3, pair + 1, jnp.where(pair < 5, pair - 1, 3))
    return group * EXPERTS_PER_GROUP + lo, group * EXPERTS_PER_GROUP + hi, tile_valid, pos


def _moe_scatter_kernel(pos_ref, h_ref, hs_in, hs_out, sem):
    del hs_in
    _token_dmas(h_ref.shape[0], lambda r: pltpu.make_async_copy(h_ref.at[r], hs_out.at[pos_ref[0, 0, r]], sem))


def _moe_scatter(pos, h2, n_sorted, tm):
    n = h2.shape[0]
    return pl.pallas_call(
        _moe_scatter_kernel,
        grid=(n // tm,),
        in_specs=[pl.BlockSpec((1, 1, tm), lambda i: (i, 0, 0), memory_space=pltpu.SMEM),
                  pl.BlockSpec((tm, TOKEN_SUB, LANES), lambda i: (i, 0, 0)),
                  pl.BlockSpec(memory_space=pl.ANY)],
        out_specs=pl.BlockSpec(memory_space=pl.ANY),
        out_shape=jax.ShapeDtypeStruct((n_sorted, TOKEN_SUB, LANES), F32),
        scratch_shapes=[pltpu.SemaphoreType.DMA(())],
        input_output_aliases={2: 0},
        compiler_params=_params("arbitrary"),
        name="moe_scatter",
    )(pos.reshape(n // tm, 1, tm), h2, jnp.zeros((n_sorted, TOKEN_SUB, LANES), F32))


def _moe_expert_kernel(lo_ref, hi_ref, valid_ref, h_ref, wg_lo, wu_lo, wd_lo, wg_hi, wu_hi, wd_hi, y_ref):
    i = pl.program_id(0)

    @pl.when(valid_ref[i] == 0)
    def _():
        y_ref[...] = jnp.zeros_like(y_ref)

    @pl.when(valid_ref[i] == 1)
    def _():
        h = _load_token_tiles(h_ref).astype(BF16)

        def expert(wg, wu, wd):
            hidden = _silu(_dot(h, wg[0])) * _dot(h, wu[0])
            return _dot(hidden.astype(BF16), wd[0])

        _store_token_tiles(y_ref, _pack_bf16_pair(expert(wg_lo, wu_lo, wd_lo), expert(wg_hi, wu_hi, wd_hi)))


def _moe_experts(tile_lo, tile_hi, tile_valid, h_sorted, wg, wu, wd):
    n_tiles = tile_lo.shape[0]
    by_lo = lambda a: pl.BlockSpec((1,) + a.shape[1:], lambda i, lo, hi, v: (lo[i], 0, 0))
    by_hi = lambda a: pl.BlockSpec((1,) + a.shape[1:], lambda i, lo, hi, v: (hi[i], 0, 0))
    tile_spec = pl.BlockSpec((ROW_BLK, TOKEN_SUB, LANES), lambda i, lo, hi, v: (i, 0, 0))
    grid_spec = pltpu.PrefetchScalarGridSpec(
        num_scalar_prefetch=3,
        grid=(n_tiles,),
        in_specs=[tile_spec, by_lo(wg), by_lo(wu), by_lo(wd), by_hi(wg), by_hi(wu), by_hi(wd)],
        out_specs=tile_spec,
    )
    return pl.pallas_call(
        _moe_expert_kernel,
        grid_spec=grid_spec,
        out_shape=jax.ShapeDtypeStruct((n_tiles * ROW_BLK, TOKEN_SUB, LANES), jnp.uint32),
        compiler_params=_params("parallel"),
        name="moe_experts",
    )(tile_lo, tile_hi, tile_valid, h_sorted, wg, wu, wd, wg, wu, wd)


def _moe_combine_kernel(pos_ref, y_hbm, rt_ref, x1_ref, modc_ref, modb_ref, o_ref, buf, sem,
                        *, ctx_len, tiles_per_batch):
    d = x1_ref.shape[-1]
    _token_dmas(buf.shape[0], lambda r: pltpu.make_async_copy(y_hbm.at[pos_ref[0, 0, r]], buf.at[r], sem))
    y_lo, y_hi = _unpack_bf16_pair(_load_token_tiles(buf))
    rt = rt_ref[...]
    y = rt[:, 1:2] * y_lo + rt[:, 2:3] * y_hi
    gate = modb_ref[0][:, 5 * d:6 * d]
    if ctx_len:
        first = pl.program_id(0) % tiles_per_batch == 0
        row = lax.broadcasted_iota(jnp.int32, y.shape, 0)
        gate = jnp.where(first & (row < ctx_len), modc_ref[0][:, 5 * d:6 * d], gate)
    o_ref[...] = x1_ref[...] + gate * y


def _moe_combine(pos, y_sorted, route, x1, mod, batch, ctx_len, ctx_row, with_ctx, tm):
    n, d = x1.shape
    tiles_per_batch = n // batch // tm
    row_spec = lambda w: pl.BlockSpec((tm, w), lambda i: (i, 0))
    return pl.pallas_call(
        functools.partial(_moe_combine_kernel, ctx_len=ctx_len if with_ctx else 0, tiles_per_batch=tiles_per_batch),
        grid=(n // tm,),
        in_specs=[pl.BlockSpec((1, 1, tm), lambda i: (i, 0, 0), memory_space=pltpu.SMEM),
                  pl.BlockSpec(memory_space=pl.ANY),
                  row_spec(LANES), row_spec(d),
                  pl.BlockSpec((1, 1, mod.shape[-1]), lambda i: (ctx_row, 0, 0)),
                  pl.BlockSpec((1, 1, mod.shape[-1]), lambda i: (i // tiles_per_batch, 0, 0))],
        out_specs=row_spec(d),
        out_shape=jax.ShapeDtypeStruct((n, d), F32),
        scratch_shapes=[pltpu.VMEM((tm, TOKEN_SUB, LANES), jnp.uint32), pltpu.SemaphoreType.DMA(())],
        compiler_params=_params("arbitrary"),
        name="moe_combine",
    )(pos.reshape(n // tm, 1, tm), y_sorted, route, x1, mod, mod)


def kernel(x, c, ctx, c_ctx, ada_w, ada_b, norm_mix, norm_ffn, w_in, hgrn_lb_logits, hgrn_norm, conv_w,
           ret_decay_logit, ret_norm, q_norm, k_norm, w_out, router_group_w, router_group_b, router_expert_w,
           router_expert_b, expert_w_gate, expert_w_up, expert_w_down):
    batch, seq_len, d = x.shape
    ctx_len = ctx.shape[1]
    depth = ada_w.shape[0]
    n_experts = expert_w_gate.shape[1]
    n_groups = router_group_w.shape[-1]
    assert ctx_len == ROW_BLK and seq_len % ROW_BLK == 0 and batch + 1 <= MOD_ROWS
    assert n_experts + n_groups <= ROUTER_ROWS and n_experts == n_groups * EXPERTS_PER_GROUP
    assert d == TOKEN_SUB * LANES
    rows = ctx_len + seq_len
    bpb = rows // ROW_BLK
    ctx_row = batch

    cv = jnp.concatenate([c, c_ctx[None], jnp.zeros((MOD_ROWS - batch - 1, d), F32)], axis=0)
    mod_all = _modulation(cv, ada_w, ada_b)
    cos, sin_signed = _rope_tables(ctx_len, seq_len)
    tokens = jnp.concatenate([ctx, x], axis=1).reshape(batch * rows, d)

    for layer in range(depth):
        with_ctx = layer < depth - 1
        out_rows = rows if with_ctx else seq_len
        mod = mod_all[layer].reshape(MOD_ROWS, 1, 6 * d)
        pa, pb, pc, pd = _inproj(tokens, mod, norm_mix[layer], w_in[layer].astype(BF16), bpb, ctx_row)
        mixers = (
            _hgrn(pa, hgrn_lb_logits, hgrn_norm[layer], layer, batch, out_rows),
            _conv(pb, conv_w[layer], batch, ctx_len, out_rows),
            _retention(pc, ret_decay_logit[layer], ret_norm[layer], batch, out_rows),
            _attention(pd, cos, sin_signed, q_norm[layer], k_norm[layer], batch, ctx_len, out_rows),
        )
        pad = ROUTER_ROWS - n_experts - n_groups
        w_router_t = jnp.concatenate([router_expert_w[layer].T, router_group_w[layer].T, jnp.zeros((pad, d), F32)])
        b_router = jnp.concatenate([router_expert_b[layer], router_group_b[layer], jnp.zeros((pad,), F32)])
        b_router_t = jnp.broadcast_to(b_router[:, None], (ROUTER_ROWS, ROW_BLK))
        x1, h2, rt = _outproj(tokens, mixers, mod, norm_ffn[layer], w_out[layer].astype(BF16), w_router_t,
                              b_router_t, n_experts, batch, ctx_row)
        tile_lo, tile_hi, tile_valid, pos = _moe_plan(rt, n_groups * PAIRS_PER_GROUP)
        tm = max(t for t in range(ROW_BLK, 4 * ROW_BLK + 1, ROW_BLK) if out_rows % t == 0)
        h_sorted = _moe_scatter(pos, h2, tile_lo.shape[0] * ROW_BLK, tm)
        y_sorted = _moe_experts(tile_lo, tile_hi, tile_valid, h_sorted, expert_w_gate[layer].astype(BF16),
                                expert_w_up[layer].astype(BF16), expert_w_down[layer].astype(BF16))
        tokens = _moe_combine(pos, y_sorted, rt, x1, mod, batch, ctx_len, ctx_row, with_ctx, tm)
    return tokens.reshape(batch, seq_len, d)
```

```python
import functools

import jax
import jax.numpy as jnp
from jax import lax
from jax.experimental import pallas as pl
from jax.experimental.pallas import tpu as pltpu

F32 = jnp.float32
BF16 = jnp.bfloat16
HIGHEST = lax.Precision.HIGHEST

HEAD_DIM = 64
GROUP_W = 256
N_HEADS = GROUP_W // HEAD_DIM
KV_W = 128
GRID_W = 64
ROPE_THETA = 10000.0
EXPERTS_PER_GROUP = 4
NORM_EPS = 1e-6
ROW_BLK = 256
SUB_BLK = 8
LEVEL_ROWS = 128
LANES = 128
SUBLANES = 8
BF16_SUBLANES = 16
LOG2_E = 1.4426950408889634
MOD_ROWS = 24
ROUTER_ROWS = 32
ROUTE_ROWS = 8
PAIRS_PER_GROUP = EXPERTS_PER_GROUP * (EXPERTS_PER_GROUP - 1) // 2
TOKEN_SUB = 8
VMEM_LIMIT_BYTES = 56 * 1024 * 1024
PA_W, PB_W, PC_W, PD_W = 5 * GROUP_W, 3 * GROUP_W, 4 * GROUP_W, GROUP_W + 2 * KV_W


def _params(*semantics):
    return pltpu.CompilerParams(dimension_semantics=semantics, vmem_limit_bytes=VMEM_LIMIT_BYTES)


def _dot(a, b):
    return jnp.dot(a, b, preferred_element_type=F32)


def _dot_nt(a, b):
    return lax.dot_general(a, b, (((1,), (1,)), ((), ())), preferred_element_type=F32)


def _dot_tn(a, b):
    return lax.dot_general(a, b, (((0,), (0,)), ((), ())), preferred_element_type=F32)


def _recip(x):
    return pl.reciprocal(x, approx=True)


def _sigmoid(x):
    return _recip(1.0 + jnp.exp(-x))


def _silu(x):
    return x * _sigmoid(x)


def _log_sigmoid(x):
    return jnp.minimum(x, 0.0) - jnp.log(1.0 + jnp.exp(-jnp.abs(x)))


def _split_bf16(x, terms):
    parts = []
    rem = x
    for i in range(terms):
        p = rem.astype(BF16)
        parts.append(p)
        if i + 1 < terms:
            rem = rem - p.astype(F32)
    return parts


def _head_ones(width):
    r = lax.broadcasted_iota(jnp.int32, (width, width), 0) // HEAD_DIM
    c = lax.broadcasted_iota(jnp.int32, (width, width), 1) // HEAD_DIM
    return r == c


def _head_masks(width):
    lane_head = lax.broadcasted_iota(jnp.int32, (1, width), 1) // HEAD_DIM
    return [lane_head == h for h in range(width // HEAD_DIM)]


def _head_expand(x, hmasks):
    zero = jnp.zeros_like(x)
    return jnp.concatenate([jnp.where(hm, x, zero) for hm in hmasks], axis=0).astype(BF16)


def _head_mean_sq(x, ones_bf16):
    hi, lo = _split_bf16(x * x, 2)
    return (_dot(hi, ones_bf16) + _dot(lo, ones_bf16)) * (1.0 / HEAD_DIM)


def _rows8(v):
    return jnp.broadcast_to(v.reshape(1, -1), (SUBLANES, v.size))


def _tile_rows(m8, rows):
    n = m8.shape[-1]
    return jnp.broadcast_to(m8[None], (rows // SUBLANES, SUBLANES, n)).reshape(rows, n)


def _head_rms_norm(x, gain8, ones_bf16):
    return x * lax.rsqrt(_head_mean_sq(x, ones_bf16) + NORM_EPS) * _tile_rows(gain8, x.shape[0])


def _norm_modulate(x, gain8, mod8, idx):
    rows, d = x.shape
    ms = jnp.mean(x * x, axis=-1, keepdims=True)
    y = x * lax.rsqrt(ms + NORM_EPS) * _tile_rows(gain8, rows)
    shift = _tile_rows(mod8[:, idx * d:(idx + 1) * d], rows)
    scale = _tile_rows(mod8[:, (idx + 1) * d:(idx + 2) * d], rows)
    return y * (1.0 + scale) + shift


def _scan_rows(j, nchunk):
    rf = pl.multiple_of(j * ROW_BLK, ROW_BLK)
    rb = pl.multiple_of(jnp.where(j == 0, 0, nchunk - j) * ROW_BLK, ROW_BLK)
    return rf, rb


def _mod_kernel(cv_ref, w_ref, b_ref, o_ref):
    s = _silu(cv_ref[...])
    o_ref[0] = jnp.dot(s, w_ref[0], precision=HIGHEST, preferred_element_type=F32) + b_ref[0]


def _modulation(cv, ada_w, ada_b):
    depth, d, n = ada_w.shape
    tn = n // 4
    return pl.pallas_call(
        _mod_kernel,
        grid=(depth, n // tn),
        in_specs=[pl.BlockSpec((MOD_ROWS, d), lambda l, j: (0, 0)),
                  pl.BlockSpec((1, d, tn), lambda l, j: (l, 0, j)),
                  pl.BlockSpec((1, 1, tn), lambda l, j: (l, 0, j))],
        out_specs=pl.BlockSpec((1, MOD_ROWS, tn), lambda l, j: (l, 0, j)),
        out_shape=jax.ShapeDtypeStruct((depth, MOD_ROWS, n), F32),
        compiler_params=_params("parallel", "parallel"),
        name="adaln_mod",
    )(cv, ada_w, ada_b.reshape(depth, 1, n))


def _inproj_kernel(x_ref, mod_ref, gain_ref, w_ref, pa_ref, pb_ref, pc_ref, pd_ref):
    h = _norm_modulate(x_ref[...], gain_ref[...], mod_ref[0], 0)
    p = _dot(h.astype(BF16), w_ref[...])
    pa_ref[...] = p[:, 0:PA_W]
    pb_ref[...] = p[:, PA_W:PA_W + PB_W].astype(pb_ref.dtype)
    pc_ref[...] = p[:, PA_W + PB_W:PA_W + PB_W + PC_W].astype(pc_ref.dtype)
    pd_ref[...] = p[:, PA_W + PB_W + PC_W:].astype(pd_ref.dtype)


def _inproj(tokens, mod, gain, w_bf16, blocks_per_batch, ctx_row):
    n, d = tokens.shape
    nblk = n // ROW_BLK

    def mod_idx(i):
        return (jnp.where(i % blocks_per_batch == 0, ctx_row, i // blocks_per_batch), 0, 0)

    outs = ((PA_W, F32), (PB_W, BF16), (PC_W, BF16), (PD_W, BF16))
    return pl.pallas_call(
        _inproj_kernel,
        grid=(nblk,),
        in_specs=[pl.BlockSpec((ROW_BLK, d), lambda i: (i, 0)),
                  pl.BlockSpec((1, SUBLANES, mod.shape[-1]), mod_idx),
                  pl.BlockSpec((SUBLANES, d), lambda i: (0, 0)),
                  pl.BlockSpec(w_bf16.shape, lambda i: (0, 0))],
        out_specs=[pl.BlockSpec((ROW_BLK, w), lambda i: (i, 0)) for w, _ in outs],
        out_shape=[jax.ShapeDtypeStruct((n, w), dt) for w, dt in outs],
        compiler_params=_params("parallel"),
        name="in_proj",
    )(tokens, mod, _rows8(gain), w_bf16)


def _hgrn_prepare(blk, lb_row, z_col, anti, s_ref, scr, consts):
    tril, triu, bmask, hmasks, level_masks = consts
    c_scr, q_scr, k_scr, v_scr = scr
    w = GROUP_W
    c_len = blk.shape[0]
    q = blk[:, 0:w]
    v = blk[:, w:2 * w]
    z = blk[:, z_col * w:(z_col + 1) * w]
    f = lb_row + (1.0 - lb_row) * _sigmoid(z)
    g = jnp.log(f)
    k = 1.0 - f
    tri = triu if anti else tril
    c = sum(_dot(tri, part) for part in _split_bf16(g, 3))
    tot = c[0:1] if anti else c[c_len - 1:c_len]
    c_scr[...] = c
    q_scr[...] = q
    k_scr[...] = k
    v_scr[...] = v

    s_t = s_ref[...]
    o = _dot_nt((q * jnp.exp(c)).astype(BF16), s_t.astype(BF16))
    k_end = (k * jnp.exp(tot - c)).astype(BF16)
    s_ref[...] = s_t * jnp.exp(tot) + jnp.where(bmask, _dot_tn(v.astype(BF16), k_end), 0.0)

    v_exp = {}
    m = c_len // 2
    while m >= SUB_BLK:
        nb = c_len // (2 * m)
        mid = m if anti else m - 1
        refs = [jnp.broadcast_to(c[b * 2 * m + mid:b * 2 * m + mid + 1], (2 * m, w)) for b in range(nb)]
        ref = refs[0] if nb == 1 else jnp.concatenate(refs, axis=0)
        e = jnp.exp(-jnp.abs(c - ref))
        qt = q * e
        kt = k * e
        pieces = []
        if 2 * m >= LEVEL_ROWS:
            for b in range(nb):
                first, second = b * 2 * m, b * 2 * m + m
                q0, k0 = (first, second) if anti else (second, first)
                a = _dot_nt(qt[q0:q0 + m].astype(BF16), _head_expand(kt[k0:k0 + m], hmasks))
                ob = _dot(a.astype(BF16), _head_expand(v[k0:k0 + m], hmasks))
                zero = jnp.zeros((m, w), F32)
                pieces += [ob, zero] if anti else [zero, ob]
        else:
            for r0 in range(0, c_len, LEVEL_ROWS):
                if r0 not in v_exp:
                    v_exp[r0] = _head_expand(v[r0:r0 + LEVEL_ROWS], hmasks)
                a = _dot_nt(qt[r0:r0 + LEVEL_ROWS].astype(BF16), _head_expand(kt[r0:r0 + LEVEL_ROWS], hmasks))
                a = jnp.where(level_masks[(m, anti)], a, 0.0).astype(BF16)
                pieces.append(_dot(a, v_exp[r0]))
        o = o + jnp.concatenate(pieces, axis=0)
        m //= 2
    return o


def _hgrn_level_masks(c_len):
    masks = {}
    rg = LEVEL_ROWS
    m = LEVEL_ROWS // 4
    while m >= SUB_BLK:
        t = lax.broadcasted_iota(jnp.int32, (rg, N_HEADS * rg), 0)
        s = lax.broadcasted_iota(jnp.int32, (rg, N_HEADS * rg), 1) % rg
        same = (t // (2 * m)) == (s // (2 * m))
        t_late = (t // m) % 2 == 1
        s_late = (s // m) % 2 == 1
        masks[(m, False)] = same & t_late & ~s_late
        masks[(m, True)] = same & ~t_late & s_late
        m //= 2
    return masks


def _hgrn_diag_unit(scr, r, anti, ones_bf16):
    c_scr, q_scr, k_scr, v_scr = scr
    cs = c_scr[pl.ds(r, SUB_BLK), :]
    qs = q_scr[pl.ds(r, SUB_BLK), :]
    ks = k_scr[pl.ds(r, SUB_BLK), :]
    vs = v_scr[pl.ds(r, SUB_BLK), :]
    row = lax.broadcasted_iota(jnp.int32, cs.shape, 0)
    prods = []
    for s in range(SUB_BLK):
        valid = (row <= s) if anti else (row >= s)
        dec = jnp.exp(jnp.where(valid, cs - cs[s:s + 1], -jnp.inf))
        prods.append(qs * dec * ks[s:s + 1])
    scores = _dot(jnp.concatenate(prods, axis=0).astype(BF16), ones_bf16)
    od = scores[0:SUB_BLK] * vs[0:1]
    for s in range(1, SUB_BLK):
        od = od + scores[s * SUB_BLK:(s + 1) * SUB_BLK] * vs[s:s + 1]
    return od


def _hgrn_kernel(p_ref, lbl_ref, gain_ref, o_ref, acc, s_f, s_b, od_f, od_b, *scr, layer, out_off):
    w = GROUP_W
    c_len = ROW_BLK
    nchunk = p_ref.shape[0] // c_len
    depth = lbl_ref.shape[0]
    scr_f, scr_b = scr[:4], scr[4:]

    logits = [lbl_ref[l] for l in range(depth)]
    mx = functools.reduce(jnp.maximum, logits)
    exps = [jnp.exp(l - mx) for l in logits]
    lb = sum(exps[1:layer + 1], jnp.zeros_like(mx)) * _recip(sum(exps))

    ri = lax.broadcasted_iota(jnp.int32, (c_len, c_len), 0)
    ci = lax.broadcasted_iota(jnp.int32, (c_len, c_len), 1)
    tril = (ci <= ri).astype(BF16)
    triu = (ci >= ri).astype(BF16)
    bmask = _head_ones(w)
    ones_bf16 = bmask.astype(BF16)
    consts = (tril, triu, bmask, _head_masks(w), _hgrn_level_masks(c_len))

    acc[...] = jnp.zeros_like(acc)
    s_f[...] = jnp.zeros_like(s_f)
    s_b[...] = jnp.zeros_like(s_b)

    def step(j, carry):
        rf, rb = _scan_rows(j, nchunk)
        of = _hgrn_prepare(p_ref[pl.ds(rf, c_len), :], lb[0:1], 2, False, s_f, scr_f, consts)
        ob = _hgrn_prepare(p_ref[pl.ds(rb, c_len), :], lb[1:2], 3, True, s_b, scr_b, consts)

        def diag(i, c2):
            r = pl.multiple_of(i * SUB_BLK, SUB_BLK)
            od_f[pl.ds(r, SUB_BLK), :] = _hgrn_diag_unit(scr_f, r, False, ones_bf16)
            od_b[pl.ds(r, SUB_BLK), :] = _hgrn_diag_unit(scr_b, r, True, ones_bf16)
            return c2

        lax.fori_loop(0, c_len // SUB_BLK, diag, 0, unroll=4)
        acc[pl.ds(rf, c_len), :] += of + od_f[...]
        acc[pl.ds(rb, c_len), :] += ob + od_b[...]
        return carry

    lax.fori_loop(0, nchunk, step, 0)

    o = acc[out_off:, :]
    gate = p_ref[out_off:, 4 * w:5 * w]
    o_ref[...] = (_head_rms_norm(o, gain_ref[...], ones_bf16) * _silu(gate)).astype(o_ref.dtype)


def _hgrn(pa, lb_logits, gain, layer, batch, out_rows):
    n, width = pa.shape
    rows = n // batch
    w = GROUP_W
    scr = lambda r: pltpu.VMEM((r, w), F32)
    return pl.pallas_call(
        functools.partial(_hgrn_kernel, layer=layer, out_off=rows - out_rows),
        grid=(batch,),
        in_specs=[pl.BlockSpec((rows, width), lambda b: (b, 0)),
                  pl.BlockSpec(lb_logits.shape, lambda b: (0, 0, 0)),
                  pl.BlockSpec((SUBLANES, w), lambda b: (0, 0))],
        out_specs=pl.BlockSpec((out_rows, w), lambda b: (b, 0)),
        out_shape=jax.ShapeDtypeStruct((batch * out_rows, w), BF16),
        scratch_shapes=[scr(rows), scr(w), scr(w)] + [scr(ROW_BLK)] * 10,
        compiler_params=_params("parallel"),
        name="hgrn2_mixer",
    )(pa, lb_logits, _rows8(gain))


def _conv_kernel(p_ref, w_ref, o_ref, *, ctx_len, out_off):
    w = GROUP_W
    p = p_ref[...].astype(F32)
    u = p[:, w:2 * w] * p[:, 2 * w:3 * w]
    n = u.shape[0]
    row = lax.broadcasted_iota(jnp.int32, u.shape, 0)
    prev = jnp.where((row == 0) | (row == ctx_len), 0.0, pltpu.roll(u, 1, 0))
    nxt = jnp.where((row == ctx_len - 1) | (row == n - 1), 0.0, pltpu.roll(u, n - 1, 0))
    cw = w_ref[...]
    y = p[:, 0:w] * (cw[0:1] * prev + cw[1:2] * u + cw[2:3] * nxt)
    o_ref[...] = y[out_off:].astype(o_ref.dtype)


def _conv(pb, conv_w, batch, ctx_len, out_rows):
    n, width = pb.shape
    rows = n // batch
    return pl.pallas_call(
        functools.partial(_conv_kernel, ctx_len=ctx_len, out_off=rows - out_rows),
        grid=(batch,),
        in_specs=[pl.BlockSpec((rows, width), lambda b: (b, 0)),
                  pl.BlockSpec(conv_w.shape, lambda b: (0, 0))],
        out_specs=pl.BlockSpec((out_rows, GROUP_W), lambda b: (b, 0)),
        out_shape=jax.ShapeDtypeStruct((batch * out_rows, GROUP_W), BF16),
        compiler_params=_params("parallel"),
        name="conv_mixer",
    )(pb, conv_w)


def _ret_kernel(p_ref, dl_ref, gain_ref, o_ref, acc, s_f, s_b, d2_scr, *, out_off):
    w = GROUP_W
    c_len = ROW_BLK
    nchunk = p_ref.shape[0] // c_len
    hmasks = _head_masks(w)
    bmask = _head_ones(w)
    ones_bf16 = bmask.astype(BF16)

    lg = _log_sigmoid(dl_ref[...])
    lane_w = lambda r: sum(jnp.where(hm, lg[r:r + 1, h:h + 1], 0.0) for h, hm in enumerate(hmasks))
    lgf, lgb = lane_w(0), lane_w(1)
    t = lax.broadcasted_iota(jnp.int32, (c_len, w), 0).astype(F32)
    scale = HEAD_DIM ** -0.5
    qf_dec = jnp.exp(lgf * (t + 1.0)) * scale
    kf_dec = jnp.exp(lgf * (c_len - 1.0 - t))
    qb_dec = jnp.exp(lgb * (c_len - t)) * scale
    kb_dec = jnp.exp(lgb * t)
    tot_f = jnp.exp(lgf * float(c_len))
    tot_b = jnp.exp(lgb * float(c_len))

    col = lax.broadcasted_iota(jnp.int32, (c_len, N_HEADS * c_len), 1)
    row = lax.broadcasted_iota(jnp.int32, (c_len, N_HEADS * c_len), 0)
    col_head = lax.broadcasted_iota(jnp.int32, (1, N_HEADS * c_len), 1) // c_len
    col_lg = lambda r: sum(jnp.where(col_head == h, lg[r:r + 1, h:h + 1], 0.0) for h in range(N_HEADS))
    dist = (row - col % c_len).astype(F32)
    d2_scr[...] = (jnp.where(dist >= 0.0, jnp.exp(col_lg(0) * jnp.maximum(dist, 0.0)), 0.0)
                   + jnp.where(dist <= 0.0, jnp.exp(col_lg(1) * jnp.maximum(-dist, 0.0)), 0.0))

    acc[...] = jnp.zeros_like(acc)
    s_f[...] = jnp.zeros_like(s_f)
    s_b[...] = jnp.zeros_like(s_b)

    def inter(q, k, v, q_dec, k_dec, tot, s_ref):
        s_t = s_ref[...]
        o = _dot_nt((q * q_dec).astype(BF16), s_t.astype(BF16))
        s_ref[...] = s_t * tot + jnp.where(bmask, _dot_tn(v, (k * k_dec).astype(BF16)), 0.0)
        return o

    def step(j, carry):
        rf, rb = _scan_rows(j, nchunk)
        blk = p_ref[pl.ds(rf, c_len), :]
        q = blk[:, 0:w].astype(F32)
        k = blk[:, w:2 * w]
        v = blk[:, 2 * w:3 * w]
        a = _dot_nt((q * scale).astype(BF16), _head_expand(k, hmasks)) * d2_scr[...]
        o = _dot(a.astype(BF16), _head_expand(v, hmasks))
        o = o + inter(q, k.astype(F32), v, qf_dec, kf_dec, tot_f, s_f)
        acc[pl.ds(rf, c_len), :] += o
        blk = p_ref[pl.ds(rb, c_len), :]
        acc[pl.ds(rb, c_len), :] += inter(blk[:, 0:w].astype(F32), blk[:, w:2 * w].astype(F32),
                                          blk[:, 2 * w:3 * w], qb_dec, kb_dec, tot_b, s_b)
        return carry

    lax.fori_loop(0, nchunk, step, 0)

    o = acc[out_off:, :]
    gate = p_ref[out_off:, 3 * w:4 * w].astype(F32)
    o_ref[...] = (_head_rms_norm(o, gain_ref[...], ones_bf16) * _silu(gate)).astype(o_ref.dtype)


def _retention(pc, decay_logit, gain, batch, out_rows):
    n, width = pc.shape
    rows = n // batch
    w = GROUP_W
    dl = jnp.zeros((8, LANES), F32).at[:decay_logit.shape[0], :decay_logit.shape[1]].set(decay_logit)
    return pl.pallas_call(
        functools.partial(_ret_kernel, out_off=rows - out_rows),
        grid=(batch,),
        in_specs=[pl.BlockSpec((rows, width), lambda b: (b, 0)),
                  pl.BlockSpec((8, LANES), lambda b: (0, 0)),
                  pl.BlockSpec((SUBLANES, w), lambda b: (0, 0))],
        out_specs=pl.BlockSpec((out_rows, w), lambda b: (b, 0)),
        out_shape=jax.ShapeDtypeStruct((batch * out_rows, w), BF16),
        scratch_shapes=[pltpu.VMEM((rows, w), F32), pltpu.VMEM((w, w), F32), pltpu.VMEM((w, w), F32),
                        pltpu.VMEM((ROW_BLK, N_HEADS * ROW_BLK), F32)],
        compiler_params=_params("parallel"),
        name="retention_mixer",
    )(pc, dl, _rows8(gain))


def _rope(x, cos, sin_signed):
    n = x.shape[-1]
    lane = lax.broadcasted_iota(jnp.int32, x.shape, 1)
    swapped = jnp.where(lane % 2 == 0, pltpu.roll(x, n - 1, 1), pltpu.roll(x, 1, 1))
    return x * cos + swapped * sin_signed


def _attn_kernel(q_ref, k_ref, v_ref, cos_ref, sin_ref, qn_ref, kn_ref, o_ref, k_buf, vt_buf, s_scr, *, qb_off):
    j = pl.program_id(1)
    nkb = k_ref.shape[0] // ROW_BLK

    @pl.when(j == 0)
    def _():
        ones_kv = _head_ones(KV_W).astype(BF16)
        kn = _head_rms_norm(k_ref[...].astype(F32), kn_ref[...], ones_kv)
        k_buf[...] = _rope(kn, cos_ref[:, 0:KV_W], sin_ref[:, 0:KV_W]).astype(BF16)
        vt_buf[...] = v_ref[...].astype(F32).T.astype(BF16)

    qb = j + qb_off
    r0 = pl.multiple_of(qb * ROW_BLK, ROW_BLK)
    ones_q = _head_ones(GROUP_W).astype(BF16)
    qn = _head_rms_norm(q_ref[...].astype(F32), qn_ref[...], ones_q)
    qr = _rope(qn, cos_ref[pl.ds(r0, ROW_BLK), :], sin_ref[pl.ds(r0, ROW_BLK), :])
    qr = (qr * (HEAD_DIM ** -0.5 * LOG2_E)).astype(BF16)
    group = GROUP_W // KV_W
    ones_rows = jnp.ones((BF16_SUBLANES, ROW_BLK), BF16)
    outs = []
    for kv in range(KV_W // HEAD_DIM):
        ksl = slice(kv * HEAD_DIM, (kv + 1) * HEAD_DIM)
        q2 = jnp.concatenate([qr[:, (kv * group + g) * HEAD_DIM:(kv * group + g + 1) * HEAD_DIM]
                              for g in range(group)], axis=0)
        m = None
        for kb in range(nkb):
            s = _dot_nt(k_buf[kb * ROW_BLK:(kb + 1) * ROW_BLK, ksl], q2)
            if kb > 0:
                s = jnp.where(qb == 0, -jnp.inf, s)
            s_scr[kb] = s
            bm = jnp.max(s, axis=0, keepdims=True)
            m = bm if m is None else jnp.maximum(m, bm)
        acc = jnp.zeros((HEAD_DIM + BF16_SUBLANES, group * ROW_BLK), F32)
        for kb in range(nkb):
            p = jnp.exp2(s_scr[kb] - m).astype(BF16)
            vt = jnp.concatenate([vt_buf[ksl, kb * ROW_BLK:(kb + 1) * ROW_BLK], ones_rows], axis=0)
            acc = acc + _dot(vt, p)
        outs.append(acc[0:HEAD_DIM] * _recip(acc[HEAD_DIM:HEAD_DIM + 1]))
    o_t = jnp.concatenate(outs, axis=0).T
    for kv in range(KV_W // HEAD_DIM):
        for g in range(group):
            h = kv * group + g
            o_ref[:, h * HEAD_DIM:(h + 1) * HEAD_DIM] = (
                o_t[g * ROW_BLK:(g + 1) * ROW_BLK, kv * HEAD_DIM:(kv + 1) * HEAD_DIM].astype(o_ref.dtype))


def _attention(pd, cos, sin_signed, q_norm, k_norm, batch, out_rows):
    n, width = pd.shape
    rows = n // batch
    bpb = rows // ROW_BLK
    nqb = out_rows // ROW_BLK
    qb_off = bpb - nqb
    kcol = GROUP_W // KV_W
    tile = lambda g, reps: _rows8(jnp.tile(g, reps))
    return pl.pallas_call(
        functools.partial(_attn_kernel, qb_off=qb_off),
        grid=(batch, nqb),
        in_specs=[pl.BlockSpec((ROW_BLK, GROUP_W), lambda b, j: (b * bpb + j + qb_off, 0)),
                  pl.BlockSpec((rows, KV_W), lambda b, j: (b, kcol)),
                  pl.BlockSpec((rows, KV_W), lambda b, j: (b, kcol + 1)),
                  pl.BlockSpec((rows, GROUP_W), lambda b, j: (0, 0)),
                  pl.BlockSpec((rows, GROUP_W), lambda b, j: (0, 0)),
                  pl.BlockSpec((SUBLANES, GROUP_W), lambda b, j: (0, 0)),
                  pl.BlockSpec((SUBLANES, KV_W), lambda b, j: (0, 0))],
        out_specs=pl.BlockSpec((ROW_BLK, GROUP_W), lambda b, j: (b * nqb + j, 0)),
        out_shape=jax.ShapeDtypeStruct((batch * out_rows, GROUP_W), BF16),
        scratch_shapes=[pltpu.VMEM((rows, KV_W), BF16), pltpu.VMEM((KV_W, rows), BF16),
                        pltpu.VMEM((bpb, ROW_BLK, (GROUP_W // KV_W) * ROW_BLK), F32)],
        compiler_params=_params("parallel", "arbitrary"),
        name="attention_mixer",
    )(pd, pd, pd, cos, sin_signed, tile(q_norm, N_HEADS), tile(k_norm, KV_W // HEAD_DIM))


def _rope_tables(ctx_len, seq_len):
    t = jnp.arange(seq_len)
    row = (t // GRID_W).astype(F32)
    col = (t % GRID_W).astype(F32)
    n_freq = HEAD_DIM // 4
    inv_freq = ROPE_THETA ** (-jnp.arange(n_freq, dtype=F32) / n_freq)
    ang = jnp.concatenate([row[:, None] * inv_freq, col[:, None] * inv_freq], axis=-1)
    cos = jnp.repeat(jnp.cos(ang), 2, axis=-1)
    sin = jnp.stack([-jnp.sin(ang), jnp.sin(ang)], axis=-1).reshape(seq_len, HEAD_DIM)
    cos = jnp.concatenate([jnp.ones((ctx_len, HEAD_DIM), F32), cos], axis=0)
    sin = jnp.concatenate([jnp.zeros((ctx_len, HEAD_DIM), F32), sin], axis=0)
    return jnp.tile(cos, (1, N_HEADS)), jnp.tile(sin, (1, N_HEADS))


def _route(lt, n_experts):
    n_groups = n_experts // EXPERTS_PER_GROUP
    first = lambda x, hit, n: jnp.min(jnp.where(hit, x, float(n)), axis=0, keepdims=True)

    gl = lt[n_experts:n_experts + n_groups]
    gexp = jnp.exp(gl - jnp.max(gl, axis=0, keepdims=True))
    gprob = gexp * _recip(jnp.sum(gexp, axis=0, keepdims=True))
    group_p = jnp.max(gprob, axis=0, keepdims=True)
    grow = lax.broadcasted_iota(jnp.int32, gl.shape, 0).astype(F32)
    gidx = first(grow, gprob == group_p, n_groups)

    el = sum(jnp.where(gidx == float(g), lt[g * EXPERTS_PER_GROUP:(g + 1) * EXPERTS_PER_GROUP], 0.0)
             for g in range(n_groups))
    eexp = jnp.exp(el - jnp.max(el, axis=0, keepdims=True))
    eprob = eexp * _recip(jnp.sum(eexp, axis=0, keepdims=True))
    erow = lax.broadcasted_iota(jnp.int32, el.shape, 0).astype(F32)
    p1 = jnp.max(eprob, axis=0, keepdims=True)
    i1 = first(erow, eprob == p1, EXPERTS_PER_GROUP)
    rest = erow != i1
    p2 = jnp.max(jnp.where(rest, eprob, -1.0), axis=0, keepdims=True)
    i2 = first(erow, rest & (eprob == p2), EXPERTS_PER_GROUP)
    scale = group_p * _recip(p1 + p2)
    lo = jnp.minimum(i1, i2)
    hi = jnp.maximum(i1, i2)
    w_lo = jnp.where(i1 < i2, p1, p2) * scale
    w_hi = jnp.where(i1 < i2, p2, p1) * scale
    pair = lo * (2.0 * EXPERTS_PER_GROUP - 1.0 - lo) * 0.5 + (hi - lo - 1.0)
    bucket = gidx * float(PAIRS_PER_GROUP) + pair
    return jnp.concatenate([bucket, w_lo, w_hi, jnp.zeros((LANES - 3, lt.shape[1]), F32)], axis=0)


def _outproj_kernel(*refs, n_experts, chunks, tiles_per_batch, with_ctx):
    x_refs = refs[:chunks]
    (a_ref, b_ref, c_ref, d_ref, modc_ref, modb_ref, gain_ref, w_ref, wr_ref, br_ref,
     x1_ref, h2_ref, rt_ref) = refs[chunks:]
    d = x1_ref.shape[-1]
    for c in range(chunks):
        rs = slice(c * ROW_BLK, (c + 1) * ROW_BLK)
        mod = modb_ref[0]
        if with_ctx and c == 0:
            mod = jnp.where(pl.program_id(0) % tiles_per_batch == 0, modc_ref[0], mod)
        mix = jnp.concatenate([a_ref[rs, :], b_ref[rs, :], c_ref[rs, :], d_ref[rs, :]], axis=1)
        x1 = x_refs[c][...] + _tile_rows(mod[:, 2 * d:3 * d], ROW_BLK) * _dot(mix, w_ref[...])
        x1_ref[rs, :] = x1
        h2 = _norm_modulate(x1, gain_ref[...], mod, 3)
        _store_token_tiles(h2_ref, h2, c * ROW_BLK)
        lt = lax.dot_general(wr_ref[...], h2, (((1,), (1,)), ((), ())), precision=HIGHEST,
                             preferred_element_type=F32) + br_ref[...]
        rt_ref[rs, :] = _route(lt, n_experts).T


def _outproj(tokens, mixers, mod, gain, w_bf16, w_router_t, b_router_t, n_experts, batch, ctx_row):
    d = tokens.shape[1]
    n_out = mixers[0].shape[0]
    out_bpb = n_out // ROW_BLK // batch
    tok_bpb = tokens.shape[0] // ROW_BLK // batch
    off = tok_bpb - out_bpb
    chunks = max(c for c in (4, 3, 2, 1) if out_bpb % c == 0)
    tpb = out_bpb // chunks
    tr = chunks * ROW_BLK

    def x_spec(c):
        return pl.BlockSpec((ROW_BLK, d), lambda i: ((i // tpb) * tok_bpb + (i % tpb) * chunks + c + off, 0))

    row_spec = lambda w: pl.BlockSpec((tr, w), lambda i: (i, 0))
    full = lambda a: pl.BlockSpec(a.shape, lambda i: (0,) * a.ndim)
    mod_spec = lambda idx: pl.BlockSpec((1, SUBLANES, mod.shape[-1]), idx)
    return pl.pallas_call(
        functools.partial(_outproj_kernel, n_experts=n_experts, chunks=chunks, tiles_per_batch=tpb,
                          with_ctx=off == 0),
        grid=(n_out // tr,),
        in_specs=[x_spec(c) for c in range(chunks)] + [
            row_spec(GROUP_W), row_spec(GROUP_W), row_spec(GROUP_W), row_spec(GROUP_W),
            mod_spec(lambda i: (ctx_row, 0, 0)), mod_spec(lambda i: (i // tpb, 0, 0)),
            pl.BlockSpec((SUBLANES, d), lambda i: (0, 0)),
            full(w_bf16), full(w_router_t), full(b_router_t)],
        out_specs=[row_spec(d),
                   pl.BlockSpec((tr * TOKEN_SUB, LANES), lambda i: (i, 0)),
                   row_spec(LANES)],
        out_shape=[jax.ShapeDtypeStruct((n_out, d), F32),
                   jax.ShapeDtypeStruct((n_out * TOKEN_SUB, LANES), F32),
                   jax.ShapeDtypeStruct((n_out, LANES), F32)],
        compiler_params=_params("parallel"),
        name="out_proj_router",
    )(*([tokens] * chunks), *mixers, mod, mod, _rows8(gain), w_bf16, w_router_t, b_router_t)


def _store_token_tiles(ref, x, first_token=0):
    rows = x.shape[0]
    for j in range(TOKEN_SUB):
        ref[pl.ds(first_token * TOKEN_SUB + j, rows, stride=TOKEN_SUB), :] = x[:, j * LANES:(j + 1) * LANES]


def _load_token_tiles(ref):
    rows = ref.shape[0] // TOKEN_SUB
    return jnp.concatenate([ref[pl.ds(j, rows, stride=TOKEN_SUB), :] for j in range(TOKEN_SUB)], axis=-1)


def _token_rows(t):
    return pl.ds(pl.multiple_of(t * TOKEN_SUB, TOKEN_SUB), TOKEN_SUB)


def _token_dmas(n, make_copy):
    def start(r, carry):
        make_copy(r).start()
        return carry

    def wait(r, carry):
        make_copy(r).wait()
        return carry

    lax.fori_loop(0, n, start, 0, unroll=8)
    lax.fori_loop(0, n, wait, 0, unroll=8)


def _pack_bf16_pair(a, b):
    au = lax.bitcast_convert_type(a.astype(BF16).astype(F32), jnp.uint32)
    bu = lax.bitcast_convert_type(b.astype(BF16).astype(F32), jnp.uint32)
    return au | (bu >> 16)


def _unpack_bf16_pair(u):
    a = lax.bitcast_convert_type(u & jnp.uint32(0xFFFF0000), F32)
    b = lax.bitcast_convert_type(u << 16, F32)
    return a, b


def _moe_plan(route, n_buckets):
    n = route.shape[0]
    nt = n // ROW_BLK
    n_tiles = nt + n_buckets
    bucket = route[:, 0].astype(jnp.int32)
    onehot = (bucket[:, None] == jnp.arange(n_buckets, dtype=jnp.int32)[None, :]).astype(F32)
    onehot = onehot.reshape(nt, ROW_BLK, n_buckets)
    tile_counts = jnp.sum(onehot, axis=1)
    before = jnp.cumsum(tile_counts, axis=0) - tile_counts
    earlier = (jnp.arange(ROW_BLK)[:, None] > jnp.arange(ROW_BLK)[None, :]).astype(F32)
    rank = jnp.einsum('ij,tjb->tib', earlier, onehot) + before[:, None, :]
    counts = jnp.sum(tile_counts, axis=0)
    tiles = jnp.ceil(counts / ROW_BLK)
    tend = jnp.cumsum(tiles)
    tstart = tend - tiles
    pos = jnp.sum(onehot * (rank + tstart * ROW_BLK), axis=-1).astype(jnp.int32).reshape(n)
    tile_id = jnp.arange(n_tiles, dtype=F32)
    tile_valid = (tile_id < tend[-1]).astype(jnp.int32)
    tile_bucket = jnp.sum((tile_id[:, None] >= tend[None, :]).astype(jnp.int32), axis=1)
    last_bucket = jnp.max(jnp.where(tiles > 0, jnp.arange(n_buckets, dtype=jnp.int32), 0))
    tile_bucket = jnp.minimum(tile_bucket, last_bucket)
    group = tile_bucket // PAIRS_PER_GROUP
    pair = tile_bucket % PAIRS_PER_GROUP
    lo = (pair >= 3).astype(jnp.int32) + (pair >= 5).astype(jnp.int32)
    hi = jnp.where(pair < 3, pair + 1, jnp.where(pair < 5, pair - 1, 3))
    return group * EXPERTS_PER_GROUP + lo, group * EXPERTS_PER_GROUP + hi, tile_valid, pos


def _moe_scatter_kernel(pos_ref, h_hbm, hs_in, hs_out, sem, *, tm):
    del hs_in
    i = pl.program_id(0)
    base = i * tm
    copy = lambda r: pltpu.make_async_copy(h_hbm.at[_token_rows(base + r)],
                                           hs_out.at[_token_rows(pos_ref[0, 0, r])], sem)

    def start(r, carry):
        copy(r).start()
        return carry

    def wait(r, carry):
        copy(r).wait()
        return carry

    lax.fori_loop(0, tm, start, 0, unroll=8)

    @pl.when(i > 0)
    def _():
        lax.fori_loop(0, tm, wait, 0, unroll=8)

    @pl.when(i == pl.num_programs(0) - 1)
    def _():
        lax.fori_loop(0, tm, wait, 0, unroll=8)


def _moe_scatter(pos, h2, n_sorted, tm):
    n = h2.shape[0] // TOKEN_SUB
    return pl.pallas_call(
        functools.partial(_moe_scatter_kernel, tm=tm),
        grid=(n // tm,),
        in_specs=[pl.BlockSpec((1, 1, tm), lambda i: (i, 0, 0), memory_space=pltpu.SMEM),
                  pl.BlockSpec(memory_space=pl.ANY),
                  pl.BlockSpec(memory_space=pl.ANY)],
        out_specs=pl.BlockSpec(memory_space=pl.ANY),
        out_shape=jax.ShapeDtypeStruct((n_sorted * TOKEN_SUB, LANES), F32),
        scratch_shapes=[pltpu.SemaphoreType.DMA(())],
        input_output_aliases={2: 0},
        compiler_params=_params("arbitrary"),
        name="moe_scatter",
    )(pos.reshape(n // tm, 1, tm), h2, jnp.zeros((n_sorted * TOKEN_SUB, LANES), F32))


def _moe_expert_kernel(lo_ref, hi_ref, valid_ref, h_ref, wg_lo, wu_lo, wd_lo, wg_hi, wu_hi, wd_hi, y_ref):
    i = pl.program_id(0)

    @pl.when(valid_ref[i] == 0)
    def _():
        y_ref[...] = jnp.zeros_like(y_ref)

    @pl.when(valid_ref[i] == 1)
    def _():
        h = _load_token_tiles(h_ref).astype(BF16)

        def expert(wg, wu, wd):
            hidden = _silu(_dot(h, wg[0])) * _dot(h, wu[0])
            return _dot(hidden.astype(BF16), wd[0])

        _store_token_tiles(y_ref, _pack_bf16_pair(expert(wg_lo, wu_lo, wd_lo), expert(wg_hi, wu_hi, wd_hi)))


def _moe_experts(tile_lo, tile_hi, tile_valid, h_sorted, wg, wu, wd):
    n_tiles = tile_lo.shape[0]
    by_lo = lambda a: pl.BlockSpec((1,) + a.shape[1:], lambda i, lo, hi, v: (lo[i], 0, 0))
    by_hi = lambda a: pl.BlockSpec((1,) + a.shape[1:], lambda i, lo, hi, v: (hi[i], 0, 0))
    tile_spec = pl.BlockSpec((ROW_BLK * TOKEN_SUB, LANES), lambda i, lo, hi, v: (i, 0))
    grid_spec = pltpu.PrefetchScalarGridSpec(
        num_scalar_prefetch=3,
        grid=(n_tiles,),
        in_specs=[tile_spec, by_lo(wg), by_lo(wu), by_lo(wd), by_hi(wg), by_hi(wu), by_hi(wd)],
        out_specs=tile_spec,
    )
    return pl.pallas_call(
        _moe_expert_kernel,
        grid_spec=grid_spec,
        out_shape=jax.ShapeDtypeStruct((n_tiles * ROW_BLK * TOKEN_SUB, LANES), jnp.uint32),
        compiler_params=_params("parallel"),
        name="moe_experts",
    )(tile_lo, tile_hi, tile_valid, h_sorted, wg, wu, wd, wg, wu, wd)


def _moe_combine_kernel(pos_ref, y_hbm, rt_ref, x1_ref, modc_ref, modb_ref, o_ref, buf, sem,
                        *, ctx_len, tiles_per_batch):
    d = x1_ref.shape[-1]
    _token_dmas(buf.shape[0] // TOKEN_SUB,
                lambda r: pltpu.make_async_copy(y_hbm.at[_token_rows(pos_ref[0, 0, r])], buf.at[_token_rows(r)], sem))
    y_lo, y_hi = _unpack_bf16_pair(_load_token_tiles(buf))
    rt = rt_ref[...]
    y = rt[:, 1:2] * y_lo + rt[:, 2:3] * y_hi
    gate = _tile_rows(modb_ref[0][:, 5 * d:6 * d], y.shape[0])
    if ctx_len:
        first = pl.program_id(0) % tiles_per_batch == 0
        row = lax.broadcasted_iota(jnp.int32, y.shape, 0)
        gate = jnp.where(first & (row < ctx_len), _tile_rows(modc_ref[0][:, 5 * d:6 * d], y.shape[0]), gate)
    o_ref[...] = x1_ref[...] + gate * y


def _moe_combine(pos, y_sorted, route, x1, mod, batch, ctx_len, ctx_row, with_ctx, tm):
    n, d = x1.shape
    tiles_per_batch = n // batch // tm
    row_spec = lambda w: pl.BlockSpec((tm, w), lambda i: (i, 0))
    return pl.pallas_call(
        functools.partial(_moe_combine_kernel, ctx_len=ctx_len if with_ctx else 0, tiles_per_batch=tiles_per_batch),
        grid=(n // tm,),
        in_specs=[pl.BlockSpec((1, 1, tm), lambda i: (i, 0, 0), memory_space=pltpu.SMEM),
                  pl.BlockSpec(memory_space=pl.ANY),
                  row_spec(LANES), row_spec(d),
                  pl.BlockSpec((1, SUBLANES, mod.shape[-1]), lambda i: (ctx_row, 0, 0)),
                  pl.BlockSpec((1, SUBLANES, mod.shape[-1]), lambda i: (i // tiles_per_batch, 0, 0))],
        out_specs=row_spec(d),
        out_shape=jax.ShapeDtypeStruct((n, d), F32),
        scratch_shapes=[pltpu.VMEM((tm * TOKEN_SUB, LANES), jnp.uint32), pltpu.SemaphoreType.DMA(())],
        compiler_params=_params("arbitrary"),
        name="moe_combine",
    )(pos.reshape(n // tm, 1, tm), y_sorted, route, x1, mod, mod)


def kernel(x, c, ctx, c_ctx, ada_w, ada_b, norm_mix, norm_ffn, w_in, hgrn_lb_logits, hgrn_norm, conv_w,
           ret_decay_logit, ret_norm, q_norm, k_norm, w_out, router_group_w, router_group_b, router_expert_w,
           router_expert_b, expert_w_gate, expert_w_up, expert_w_down):
    batch, seq_len, d = x.shape
    ctx_len = ctx.shape[1]
    depth = ada_w.shape[0]
    n_experts = expert_w_gate.shape[1]
    n_groups = router_group_w.shape[-1]
    assert ctx_len == ROW_BLK and seq_len % ROW_BLK == 0 and batch + 1 <= MOD_ROWS
    assert n_experts + n_groups <= ROUTER_ROWS and n_experts == n_groups * EXPERTS_PER_GROUP
    assert d == TOKEN_SUB * LANES
    rows = ctx_len + seq_len
    bpb = rows // ROW_BLK
    ctx_row = batch

    cv = jnp.concatenate([c, c_ctx[None], jnp.zeros((MOD_ROWS - batch - 1, d), F32)], axis=0)
    mod_all = _modulation(cv, ada_w, ada_b)
    cos, sin_signed = _rope_tables(ctx_len, seq_len)
    tokens = jnp.concatenate([ctx, x], axis=1).reshape(batch * rows, d)

    for layer in range(depth):
        with_ctx = layer < depth - 1
        out_rows = rows if with_ctx else seq_len
        mod = jnp.broadcast_to(mod_all[layer][:, None, :], (MOD_ROWS, SUBLANES, 6 * d))
        pa, pb, pc, pd = _inproj(tokens, mod, norm_mix[layer], w_in[layer].astype(BF16), bpb, ctx_row)
        mixers = (
            _hgrn(pa, hgrn_lb_logits, hgrn_norm[layer], layer, batch, out_rows),
            _conv(pb, conv_w[layer], batch, ctx_len, out_rows),
            _retention(pc, ret_decay_logit[layer], ret_norm[layer], batch, out_rows),
            _attention(pd, cos, sin_signed, q_norm[layer], k_norm[layer], batch, out_rows),
        )
        pad = ROUTER_ROWS - n_experts - n_groups
        w_router_t = jnp.concatenate([router_expert_w[layer].T, router_group_w[layer].T, jnp.zeros((pad, d), F32)])
        b_router = jnp.concatenate([router_expert_b[layer], router_group_b[layer], jnp.zeros((pad,), F32)])
        b_router_t = jnp.broadcast_to(b_router[:, None], (ROUTER_ROWS, ROW_BLK))
        x1, h2, rt = _outproj(tokens, mixers, mod, norm_ffn[layer], w_out[layer].astype(BF16), w_router_t,
                              b_router_t, n_experts, batch, ctx_row)
        tile_lo, tile_hi, tile_valid, pos = _moe_plan(rt, n_groups * PAIRS_PER_GROUP)
        tm = max(t for t in range(ROW_BLK, 4 * ROW_BLK + 1, ROW_BLK) if out_rows % t == 0)
        h_sorted = _moe_scatter(pos, h2, tile_lo.shape[0] * ROW_BLK, tm)
        y_sorted = _moe_experts(tile_lo, tile_hi, tile_valid, h_sorted, expert_w_gate[layer].astype(BF16),
                                expert_w_up[layer].astype(BF16), expert_w_down[layer].astype(BF16))
        tokens = _moe_combine(pos, y_sorted, rt, x1, mod, batch, ctx_len, ctx_row, with_ctx, tm)
    return tokens.reshape(batch, seq_len, d)
```

```python
import functools

import jax
import jax.numpy as jnp
from jax import lax
from jax.experimental import pallas as pl
from jax.experimental.pallas import tpu as pltpu

F32 = jnp.float32
BF16 = jnp.bfloat16
HIGHEST = lax.Precision.HIGHEST

HEAD_DIM = 64
GROUP_W = 256
N_HEADS = GROUP_W // HEAD_DIM
KV_W = 128
GRID_W = 64
ROPE_THETA = 10000.0
EXPERTS_PER_GROUP = 4
NORM_EPS = 1e-6
ROW_BLK = 256
SUB_BLK = 8
LEVEL_ROWS = 128
LANES = 128
SUBLANES = 8
BF16_SUBLANES = 16
LOG2_E = 1.4426950408889634
MOD_ROWS = 24
ROUTER_ROWS = 32
ROUTE_ROWS = 8
PAIRS_PER_GROUP = EXPERTS_PER_GROUP * (EXPERTS_PER_GROUP - 1) // 2
TOKEN_SUB = 8
VMEM_LIMIT_BYTES = 56 * 1024 * 1024
PA_W, PB_W, PC_W, PD_W = 5 * GROUP_W, 3 * GROUP_W, 4 * GROUP_W, GROUP_W + 2 * KV_W


def _params(*semantics):
    return pltpu.CompilerParams(dimension_semantics=semantics, vmem_limit_bytes=VMEM_LIMIT_BYTES)


def _dot(a, b):
    return jnp.dot(a, b, preferred_element_type=F32)


def _dot_nt(a, b):
    return lax.dot_general(a, b, (((1,), (1,)), ((), ())), preferred_element_type=F32)


def _dot_tn(a, b):
    return lax.dot_general(a, b, (((0,), (0,)), ((), ())), preferred_element_type=F32)


def _recip(x):
    return pl.reciprocal(x, approx=True)


def _sigmoid(x):
    return _recip(1.0 + jnp.exp(-x))


def _silu(x):
    return x * _sigmoid(x)


def _log_sigmoid(x):
    return jnp.minimum(x, 0.0) - jnp.log(1.0 + jnp.exp(-jnp.abs(x)))


def _split_bf16(x, terms):
    parts = []
    rem = x
    for i in range(terms):
        p = rem.astype(BF16)
        parts.append(p)
        if i + 1 < terms:
            rem = rem - p.astype(F32)
    return parts


def _head_ones(width):
    r = lax.broadcasted_iota(jnp.int32, (width, width), 0) // HEAD_DIM
    c = lax.broadcasted_iota(jnp.int32, (width, width), 1) // HEAD_DIM
    return r == c


def _head_masks(width):
    lane_head = lax.broadcasted_iota(jnp.int32, (1, width), 1) // HEAD_DIM
    return [lane_head == h for h in range(width // HEAD_DIM)]


def _head_expand(x, hmasks):
    zero = jnp.zeros_like(x)
    return jnp.concatenate([jnp.where(hm, x, zero) for hm in hmasks], axis=0).astype(BF16)


def _head_mean_sq(x, ones_bf16):
    hi, lo = _split_bf16(x * x, 2)
    return (_dot(hi, ones_bf16) + _dot(lo, ones_bf16)) * (1.0 / HEAD_DIM)


def _rows8(v):
    return jnp.broadcast_to(v.reshape(1, -1), (SUBLANES, v.size))


def _tile_rows(m8, rows):
    n = m8.shape[-1]
    return jnp.broadcast_to(m8[None], (rows // SUBLANES, SUBLANES, n)).reshape(rows, n)


def _head_rms_norm(x, gain8, ones_bf16):
    return x * lax.rsqrt(_head_mean_sq(x, ones_bf16) + NORM_EPS) * _tile_rows(gain8, x.shape[0])


def _norm_modulate(x, gain8, mod8, idx):
    rows, d = x.shape
    ms = jnp.mean(x * x, axis=-1, keepdims=True)
    y = x * lax.rsqrt(ms + NORM_EPS) * _tile_rows(gain8, rows)
    shift = _tile_rows(mod8[:, idx * d:(idx + 1) * d], rows)
    scale = _tile_rows(mod8[:, (idx + 1) * d:(idx + 2) * d], rows)
    return y * (1.0 + scale) + shift


def _scan_rows(j, nchunk):
    rf = pl.multiple_of(j * ROW_BLK, ROW_BLK)
    rb = pl.multiple_of(jnp.where(j == 0, 0, nchunk - j) * ROW_BLK, ROW_BLK)
    return rf, rb


def _mod_kernel(cv_ref, w_ref, b_ref, o_ref):
    s = _silu(cv_ref[...])
    o_ref[0] = jnp.dot(s, w_ref[0], precision=HIGHEST, preferred_element_type=F32) + b_ref[0]


def _modulation(cv, ada_w, ada_b):
    depth, d, n = ada_w.shape
    tn = n // 4
    return pl.pallas_call(
        _mod_kernel,
        grid=(depth, n // tn),
        in_specs=[pl.BlockSpec((MOD_ROWS, d), lambda l, j: (0, 0)),
                  pl.BlockSpec((1, d, tn), lambda l, j: (l, 0, j)),
                  pl.BlockSpec((1, 1, tn), lambda l, j: (l, 0, j))],
        out_specs=pl.BlockSpec((1, MOD_ROWS, tn), lambda l, j: (l, 0, j)),
        out_shape=jax.ShapeDtypeStruct((depth, MOD_ROWS, n), F32),
        compiler_params=_params("parallel", "parallel"),
        name="adaln_mod",
    )(cv, ada_w, ada_b.reshape(depth, 1, n))


def _inproj_kernel(x_ref, mod_ref, gain_ref, w_ref, pa_ref, pb_ref, pc_ref, pd_ref):
    h = _norm_modulate(x_ref[...], gain_ref[...], mod_ref[0], 0)
    p = _dot(h.astype(BF16), w_ref[...])
    pa_ref[...] = p[:, 0:PA_W]
    pb_ref[...] = p[:, PA_W:PA_W + PB_W].astype(pb_ref.dtype)
    pc_ref[...] = p[:, PA_W + PB_W:PA_W + PB_W + PC_W].astype(pc_ref.dtype)
    pd_ref[...] = p[:, PA_W + PB_W + PC_W:].astype(pd_ref.dtype)


def _inproj(tokens, mod, gain, w_bf16, blocks_per_batch, ctx_row):
    n, d = tokens.shape
    nblk = n // ROW_BLK

    def mod_idx(i):
        return (jnp.where(i % blocks_per_batch == 0, ctx_row, i // blocks_per_batch), 0, 0)

    outs = ((PA_W, F32), (PB_W, BF16), (PC_W, BF16), (PD_W, BF16))
    return pl.pallas_call(
        _inproj_kernel,
        grid=(nblk,),
        in_specs=[pl.BlockSpec((ROW_BLK, d), lambda i: (i, 0)),
                  pl.BlockSpec((1, SUBLANES, mod.shape[-1]), mod_idx),
                  pl.BlockSpec((SUBLANES, d), lambda i: (0, 0)),
                  pl.BlockSpec(w_bf16.shape, lambda i: (0, 0))],
        out_specs=[pl.BlockSpec((ROW_BLK, w), lambda i: (i, 0)) for w, _ in outs],
        out_shape=[jax.ShapeDtypeStruct((n, w), dt) for w, dt in outs],
        compiler_params=_params("parallel"),
        name="in_proj",
    )(tokens, mod, _rows8(gain), w_bf16)


def _hgrn_prepare(blk, lb_row, z_col, anti, s_ref, scr, consts):
    tril, triu, bmask, hmasks, level_masks = consts
    c_scr, q_scr, k_scr, v_scr = scr
    w = GROUP_W
    c_len = blk.shape[0]
    q = blk[:, 0:w]
    v = blk[:, w:2 * w]
    z = blk[:, z_col * w:(z_col + 1) * w]
    f = lb_row + (1.0 - lb_row) * _sigmoid(z)
    g = jnp.log(f) * LOG2_E
    k = 1.0 - f
    tri = triu if anti else tril
    c = sum(_dot(tri, part) for part in _split_bf16(g, 3))
    tot = c[0:1] if anti else c[c_len - 1:c_len]
    c_scr[...] = c
    q_scr[...] = q
    k_scr[...] = k
    v_scr[...] = v

    s_t = s_ref[...]
    o = _dot_nt((q * jnp.exp2(c)).astype(BF16), s_t.astype(BF16))
    k_end = (k * jnp.exp2(tot - c)).astype(BF16)
    s_ref[...] = s_t * jnp.exp2(tot) + jnp.where(bmask, _dot_tn(v.astype(BF16), k_end), 0.0)

    v_exp = {}
    m = c_len // 2
    while m >= SUB_BLK:
        nb = c_len // (2 * m)
        mid = m if anti else m - 1
        refs = [jnp.broadcast_to(c[b * 2 * m + mid:b * 2 * m + mid + 1], (2 * m, w)) for b in range(nb)]
        ref = refs[0] if nb == 1 else jnp.concatenate(refs, axis=0)
        e = jnp.exp2(-jnp.abs(c - ref))
        qt = q * e
        kt = k * e
        pieces = []
        if 2 * m >= LEVEL_ROWS:
            for b in range(nb):
                first, second = b * 2 * m, b * 2 * m + m
                q0, k0 = (first, second) if anti else (second, first)
                a = _dot_nt(qt[q0:q0 + m].astype(BF16), _head_expand(kt[k0:k0 + m], hmasks))
                ob = _dot(a.astype(BF16), _head_expand(v[k0:k0 + m], hmasks))
                zero = jnp.zeros((m, w), F32)
                pieces += [ob, zero] if anti else [zero, ob]
        else:
            for r0 in range(0, c_len, LEVEL_ROWS):
                if r0 not in v_exp:
                    v_exp[r0] = _head_expand(v[r0:r0 + LEVEL_ROWS], hmasks)
                a = _dot_nt(qt[r0:r0 + LEVEL_ROWS].astype(BF16), _head_expand(kt[r0:r0 + LEVEL_ROWS], hmasks))
                a = jnp.where(level_masks[(m, anti)], a, 0.0).astype(BF16)
                pieces.append(_dot(a, v_exp[r0]))
        o = o + jnp.concatenate(pieces, axis=0)
        m //= 2
    return o


def _hgrn_level_masks(c_len):
    masks = {}
    rg = LEVEL_ROWS
    m = LEVEL_ROWS // 4
    while m >= SUB_BLK:
        t = lax.broadcasted_iota(jnp.int32, (rg, N_HEADS * rg), 0)
        s = lax.broadcasted_iota(jnp.int32, (rg, N_HEADS * rg), 1) % rg
        same = (t // (2 * m)) == (s // (2 * m))
        t_late = (t // m) % 2 == 1
        s_late = (s // m) % 2 == 1
        masks[(m, False)] = same & t_late & ~s_late
        masks[(m, True)] = same & ~t_late & s_late
        m //= 2
    return masks


def _hgrn_diag_unit(scr, r, anti, ones_bf16):
    c_scr, q_scr, k_scr, v_scr = scr
    cs = c_scr[pl.ds(r, SUB_BLK), :]
    qs = q_scr[pl.ds(r, SUB_BLK), :]
    ks = k_scr[pl.ds(r, SUB_BLK), :]
    vs = v_scr[pl.ds(r, SUB_BLK), :]
    row = lax.broadcasted_iota(jnp.int32, cs.shape, 0)
    prods = []
    for s in range(SUB_BLK):
        valid = (row <= s) if anti else (row >= s)
        dec = jnp.exp2(jnp.where(valid, cs - cs[s:s + 1], -jnp.inf))
        prods.append(qs * dec * ks[s:s + 1])
    scores = _dot(jnp.concatenate(prods, axis=0).astype(BF16), ones_bf16)
    od = scores[0:SUB_BLK] * vs[0:1]
    for s in range(1, SUB_BLK):
        od = od + scores[s * SUB_BLK:(s + 1) * SUB_BLK] * vs[s:s + 1]
    return od


def _hgrn_kernel(p_ref, lbl_ref, gain_ref, o_ref, acc, s_f, s_b, od_f, od_b, *scr, layer, out_off):
    w = GROUP_W
    c_len = ROW_BLK
    nchunk = p_ref.shape[0] // c_len
    depth = lbl_ref.shape[0]
    scr_f, scr_b = scr[:4], scr[4:]

    logits = [lbl_ref[l] for l in range(depth)]
    mx = functools.reduce(jnp.maximum, logits)
    exps = [jnp.exp(l - mx) for l in logits]
    lb = sum(exps[1:layer + 1], jnp.zeros_like(mx)) * _recip(sum(exps))

    ri = lax.broadcasted_iota(jnp.int32, (c_len, c_len), 0)
    ci = lax.broadcasted_iota(jnp.int32, (c_len, c_len), 1)
    tril = (ci <= ri).astype(BF16)
    triu = (ci >= ri).astype(BF16)
    bmask = _head_ones(w)
    ones_bf16 = bmask.astype(BF16)
    consts = (tril, triu, bmask, _head_masks(w), _hgrn_level_masks(c_len))

    acc[...] = jnp.zeros_like(acc)
    s_f[...] = jnp.zeros_like(s_f)
    s_b[...] = jnp.zeros_like(s_b)

    def step(j, carry):
        rf, rb = _scan_rows(j, nchunk)
        of = _hgrn_prepare(p_ref[pl.ds(rf, c_len), :], lb[0:1], 2, False, s_f, scr_f, consts)
        ob = _hgrn_prepare(p_ref[pl.ds(rb, c_len), :], lb[1:2], 3, True, s_b, scr_b, consts)

        def diag(i, c2):
            r = pl.multiple_of(i * SUB_BLK, SUB_BLK)
            od_f[pl.ds(r, SUB_BLK), :] = _hgrn_diag_unit(scr_f, r, False, ones_bf16)
            od_b[pl.ds(r, SUB_BLK), :] = _hgrn_diag_unit(scr_b, r, True, ones_bf16)
            return c2

        lax.fori_loop(0, c_len // SUB_BLK, diag, 0, unroll=4)
        acc[pl.ds(rf, c_len), :] += of + od_f[...]
        acc[pl.ds(rb, c_len), :] += ob + od_b[...]
        return carry

    lax.fori_loop(0, nchunk, step, 0)

    o = acc[out_off:, :]
    gate = p_ref[out_off:, 4 * w:5 * w]
    o_ref[...] = (_head_rms_norm(o, gain_ref[...], ones_bf16) * _silu(gate)).astype(o_ref.dtype)


def _hgrn(pa, lb_logits, gain, layer, batch, out_rows):
    n, width = pa.shape
    rows = n // batch
    w = GROUP_W
    scr = lambda r: pltpu.VMEM((r, w), F32)
    return pl.pallas_call(
        functools.partial(_hgrn_kernel, layer=layer, out_off=rows - out_rows),
        grid=(batch,),
        in_specs=[pl.BlockSpec((rows, width), lambda b: (b, 0)),
                  pl.BlockSpec(lb_logits.shape, lambda b: (0, 0, 0)),
                  pl.BlockSpec((SUBLANES, w), lambda b: (0, 0))],
        out_specs=pl.BlockSpec((out_rows, w), lambda b: (b, 0)),
        out_shape=jax.ShapeDtypeStruct((batch * out_rows, w), BF16),
        scratch_shapes=[scr(rows), scr(w), scr(w)] + [scr(ROW_BLK)] * 10,
        compiler_params=_params("parallel"),
        name="hgrn2_mixer",
    )(pa, lb_logits, _rows8(gain))


def _conv_kernel(p_ref, w_ref, o_ref, *, ctx_len, out_off):
    w = GROUP_W
    p = p_ref[...].astype(F32)
    u = p[:, w:2 * w] * p[:, 2 * w:3 * w]
    n = u.shape[0]
    row = lax.broadcasted_iota(jnp.int32, u.shape, 0)
    prev = jnp.where((row == 0) | (row == ctx_len), 0.0, pltpu.roll(u, 1, 0))
    nxt = jnp.where((row == ctx_len - 1) | (row == n - 1), 0.0, pltpu.roll(u, n - 1, 0))
    cw = w_ref[...]
    y = p[:, 0:w] * (cw[0:1] * prev + cw[1:2] * u + cw[2:3] * nxt)
    o_ref[...] = y[out_off:].astype(o_ref.dtype)


def _conv(pb, conv_w, batch, ctx_len, out_rows):
    n, width = pb.shape
    rows = n // batch
    return pl.pallas_call(
        functools.partial(_conv_kernel, ctx_len=ctx_len, out_off=rows - out_rows),
        grid=(batch,),
        in_specs=[pl.BlockSpec((rows, width), lambda b: (b, 0)),
                  pl.BlockSpec(conv_w.shape, lambda b: (0, 0))],
        out_specs=pl.BlockSpec((out_rows, GROUP_W), lambda b: (b, 0)),
        out_shape=jax.ShapeDtypeStruct((batch * out_rows, GROUP_W), BF16),
        compiler_params=_params("parallel"),
        name="conv_mixer",
    )(pb, conv_w)


def _ret_kernel(p_ref, dl_ref, gain_ref, o_ref, acc, s_f, s_b, d2_scr, *, out_off):
    w = GROUP_W
    c_len = ROW_BLK
    nchunk = p_ref.shape[0] // c_len
    hmasks = _head_masks(w)
    bmask = _head_ones(w)
    ones_bf16 = bmask.astype(BF16)

    lg = _log_sigmoid(dl_ref[...])
    lane_w = lambda r: sum(jnp.where(hm, lg[r:r + 1, h:h + 1], 0.0) for h, hm in enumerate(hmasks))
    lgf, lgb = lane_w(0), lane_w(1)
    t = lax.broadcasted_iota(jnp.int32, (c_len, w), 0).astype(F32)
    scale = HEAD_DIM ** -0.5
    qf_dec = jnp.exp(lgf * (t + 1.0)) * scale
    kf_dec = jnp.exp(lgf * (c_len - 1.0 - t))
    qb_dec = jnp.exp(lgb * (c_len - t)) * scale
    kb_dec = jnp.exp(lgb * t)
    tot_f = jnp.exp(lgf * float(c_len))
    tot_b = jnp.exp(lgb * float(c_len))

    col = lax.broadcasted_iota(jnp.int32, (c_len, N_HEADS * c_len), 1)
    row = lax.broadcasted_iota(jnp.int32, (c_len, N_HEADS * c_len), 0)
    col_head = lax.broadcasted_iota(jnp.int32, (1, N_HEADS * c_len), 1) // c_len
    col_lg = lambda r: sum(jnp.where(col_head == h, lg[r:r + 1, h:h + 1], 0.0) for h in range(N_HEADS))
    dist = (row - col % c_len).astype(F32)
    d2_scr[...] = (jnp.where(dist >= 0.0, jnp.exp(col_lg(0) * jnp.maximum(dist, 0.0)), 0.0)
                   + jnp.where(dist <= 0.0, jnp.exp(col_lg(1) * jnp.maximum(-dist, 0.0)), 0.0))

    acc[...] = jnp.zeros_like(acc)
    s_f[...] = jnp.zeros_like(s_f)
    s_b[...] = jnp.zeros_like(s_b)

    def inter(q, k, v, q_dec, k_dec, tot, s_ref):
        s_t = s_ref[...]
        o = _dot_nt((q * q_dec).astype(BF16), s_t.astype(BF16))
        s_ref[...] = s_t * tot + jnp.where(bmask, _dot_tn(v, (k * k_dec).astype(BF16)), 0.0)
        return o

    def step(j, carry):
        rf, rb = _scan_rows(j, nchunk)
        blk = p_ref[pl.ds(rf, c_len), :]
        q = blk[:, 0:w].astype(F32)
        k = blk[:, w:2 * w]
        v = blk[:, 2 * w:3 * w]
        a = _dot_nt((q * scale).astype(BF16), _head_expand(k, hmasks)) * d2_scr[...]
        o = _dot(a.astype(BF16), _head_expand(v, hmasks))
        o = o + inter(q, k.astype(F32), v, qf_dec, kf_dec, tot_f, s_f)
        acc[pl.ds(rf, c_len), :] += o
        blk = p_ref[pl.ds(rb, c_len), :]
        acc[pl.ds(rb, c_len), :] += inter(blk[:, 0:w].astype(F32), blk[:, w:2 * w].astype(F32),
                                          blk[:, 2 * w:3 * w], qb_dec, kb_dec, tot_b, s_b)
        return carry

    lax.fori_loop(0, nchunk, step, 0)

    o = acc[out_off:, :]
    gate = p_ref[out_off:, 3 * w:4 * w].astype(F32)
    o_ref[...] = (_head_rms_norm(o, gain_ref[...], ones_bf16) * _silu(gate)).astype(o_ref.dtype)


def _retention(pc, decay_logit, gain, batch, out_rows):
    n, width = pc.shape
    rows = n // batch
    w = GROUP_W
    dl = jnp.zeros((8, LANES), F32).at[:decay_logit.shape[0], :decay_logit.shape[1]].set(decay_logit)
    return pl.pallas_call(
        functools.partial(_ret_kernel, out_off=rows - out_rows),
        grid=(batch,),
        in_specs=[pl.BlockSpec((rows, width), lambda b: (b, 0)),
                  pl.BlockSpec((8, LANES), lambda b: (0, 0)),
                  pl.BlockSpec((SUBLANES, w), lambda b: (0, 0))],
        out_specs=pl.BlockSpec((out_rows, w), lambda b: (b, 0)),
        out_shape=jax.ShapeDtypeStruct((batch * out_rows, w), BF16),
        scratch_shapes=[pltpu.VMEM((rows, w), F32), pltpu.VMEM((w, w), F32), pltpu.VMEM((w, w), F32),
                        pltpu.VMEM((ROW_BLK, N_HEADS * ROW_BLK), F32)],
        compiler_params=_params("parallel"),
        name="retention_mixer",
    )(pc, dl, _rows8(gain))


def _rope(x, cos, sin_signed):
    n = x.shape[-1]
    lane = lax.broadcasted_iota(jnp.int32, x.shape, 1)
    swapped = jnp.where(lane % 2 == 0, pltpu.roll(x, n - 1, 1), pltpu.roll(x, 1, 1))
    return x * cos + swapped * sin_signed


def _attn_kernel(q_ref, k_ref, v_ref, cos_ref, sin_ref, qn_ref, kn_ref, o_ref, k_buf, vt_buf, s_scr, *, qb_off):
    j = pl.program_id(1)
    nkb = k_ref.shape[0] // ROW_BLK

    @pl.when(j == 0)
    def _():
        ones_kv = _head_ones(KV_W).astype(BF16)
        kn = _head_rms_norm(k_ref[...].astype(F32), kn_ref[...], ones_kv)
        k_buf[...] = _rope(kn, cos_ref[:, 0:KV_W], sin_ref[:, 0:KV_W]).astype(BF16)
        vt_buf[...] = v_ref[...].astype(F32).T.astype(BF16)

    qb = j + qb_off
    r0 = pl.multiple_of(qb * ROW_BLK, ROW_BLK)
    ones_q = _head_ones(GROUP_W).astype(BF16)
    qn = _head_rms_norm(q_ref[...].astype(F32), qn_ref[...], ones_q)
    qr = _rope(qn, cos_ref[pl.ds(r0, ROW_BLK), :], sin_ref[pl.ds(r0, ROW_BLK), :])
    qr = (qr * (HEAD_DIM ** -0.5 * LOG2_E)).astype(BF16)
    group = GROUP_W // KV_W
    ones_rows = jnp.ones((BF16_SUBLANES, ROW_BLK), BF16)
    outs = []
    for kv in range(KV_W // HEAD_DIM):
        ksl = slice(kv * HEAD_DIM, (kv + 1) * HEAD_DIM)
        q2 = jnp.concatenate([qr[:, (kv * group + g) * HEAD_DIM:(kv * group + g + 1) * HEAD_DIM]
                              for g in range(group)], axis=0)
        m = None
        for kb in range(nkb):
            s = _dot_nt(k_buf[kb * ROW_BLK:(kb + 1) * ROW_BLK, ksl], q2)
            if kb > 0:
                s = jnp.where(qb == 0, -jnp.inf, s)
            s_scr[kb] = s
            bm = jnp.max(s, axis=0, keepdims=True)
            m = bm if m is None else jnp.maximum(m, bm)
        acc = jnp.zeros((HEAD_DIM + BF16_SUBLANES, group * ROW_BLK), F32)
        for kb in range(nkb):
            p = jnp.exp2(s_scr[kb] - m).astype(BF16)
            vt = jnp.concatenate([vt_buf[ksl, kb * ROW_BLK:(kb + 1) * ROW_BLK], ones_rows], axis=0)
            acc = acc + _dot(vt, p)
        outs.append(acc[0:HEAD_DIM] * _recip(acc[HEAD_DIM:HEAD_DIM + 1]))
    o_t = jnp.concatenate(outs, axis=0).T
    for kv in range(KV_W // HEAD_DIM):
        for g in range(group):
            h = kv * group + g
            o_ref[:, h * HEAD_DIM:(h + 1) * HEAD_DIM] = (
                o_t[g * ROW_BLK:(g + 1) * ROW_BLK, kv * HEAD_DIM:(kv + 1) * HEAD_DIM].astype(o_ref.dtype))


def _attention(pd, cos, sin_signed, q_norm, k_norm, batch, out_rows):
    n, width = pd.shape
    rows = n // batch
    bpb = rows // ROW_BLK
    nqb = out_rows // ROW_BLK
    qb_off = bpb - nqb
    kcol = GROUP_W // KV_W
    tile = lambda g, reps: _rows8(jnp.tile(g, reps))
    return pl.pallas_call(
        functools.partial(_attn_kernel, qb_off=qb_off),
        grid=(batch, nqb),
        in_specs=[pl.BlockSpec((ROW_BLK, GROUP_W), lambda b, j: (b * bpb + j + qb_off, 0)),
                  pl.BlockSpec((rows, KV_W), lambda b, j: (b, kcol)),
                  pl.BlockSpec((rows, KV_W), lambda b, j: (b, kcol + 1)),
                  pl.BlockSpec((rows, GROUP_W), lambda b, j: (0, 0)),
                  pl.BlockSpec((rows, GROUP_W), lambda b, j: (0, 0)),
                  pl.BlockSpec((SUBLANES, GROUP_W), lambda b, j: (0, 0)),
                  pl.BlockSpec((SUBLANES, KV_W), lambda b, j: (0, 0))],
        out_specs=pl.BlockSpec((ROW_BLK, GROUP_W), lambda b, j: (b * nqb + j, 0)),
        out_shape=jax.ShapeDtypeStruct((batch * out_rows, GROUP_W), BF16),
        scratch_shapes=[pltpu.VMEM((rows, KV_W), BF16), pltpu.VMEM((KV_W, rows), BF16),
                        pltpu.VMEM((bpb, ROW_BLK, (GROUP_W // KV_W) * ROW_BLK), F32)],
        compiler_params=_params("parallel", "arbitrary"),
        name="attention_mixer",
    )(pd, pd, pd, cos, sin_signed, tile(q_norm, N_HEADS), tile(k_norm, KV_W // HEAD_DIM))


def _rope_tables(ctx_len, seq_len):
    t = jnp.arange(seq_len)
    row = (t // GRID_W).astype(F32)
    col = (t % GRID_W).astype(F32)
    n_freq = HEAD_DIM // 4
    inv_freq = ROPE_THETA ** (-jnp.arange(n_freq, dtype=F32) / n_freq)
    ang = jnp.concatenate([row[:, None] * inv_freq, col[:, None] * inv_freq], axis=-1)
    cos = jnp.repeat(jnp.cos(ang), 2, axis=-1)
    sin = jnp.stack([-jnp.sin(ang), jnp.sin(ang)], axis=-1).reshape(seq_len, HEAD_DIM)
    cos = jnp.concatenate([jnp.ones((ctx_len, HEAD_DIM), F32), cos], axis=0)
    sin = jnp.concatenate([jnp.zeros((ctx_len, HEAD_DIM), F32), sin], axis=0)
    return jnp.tile(cos, (1, N_HEADS)), jnp.tile(sin, (1, N_HEADS))


def _route(lt, n_experts):
    n_groups = n_experts // EXPERTS_PER_GROUP
    first = lambda x, hit, n: jnp.min(jnp.where(hit, x, float(n)), axis=0, keepdims=True)

    gl = lt[n_experts:n_experts + n_groups]
    gexp = jnp.exp(gl - jnp.max(gl, axis=0, keepdims=True))
    gprob = gexp * _recip(jnp.sum(gexp, axis=0, keepdims=True))
    group_p = jnp.max(gprob, axis=0, keepdims=True)
    grow = lax.broadcasted_iota(jnp.int32, gl.shape, 0).astype(F32)
    gidx = first(grow, gprob == group_p, n_groups)

    el = sum(jnp.where(gidx == float(g), lt[g * EXPERTS_PER_GROUP:(g + 1) * EXPERTS_PER_GROUP], 0.0)
             for g in range(n_groups))
    eexp = jnp.exp(el - jnp.max(el, axis=0, keepdims=True))
    eprob = eexp * _recip(jnp.sum(eexp, axis=0, keepdims=True))
    erow = lax.broadcasted_iota(jnp.int32, el.shape, 0).astype(F32)
    p1 = jnp.max(eprob, axis=0, keepdims=True)
    i1 = first(erow, eprob == p1, EXPERTS_PER_GROUP)
    rest = erow != i1
    p2 = jnp.max(jnp.where(rest, eprob, -1.0), axis=0, keepdims=True)
    i2 = first(erow, rest & (eprob == p2), EXPERTS_PER_GROUP)
    scale = group_p * _recip(p1 + p2)
    lo = jnp.minimum(i1, i2)
    hi = jnp.maximum(i1, i2)
    w_lo = jnp.where(i1 < i2, p1, p2) * scale
    w_hi = jnp.where(i1 < i2, p2, p1) * scale
    pair = lo * (2.0 * EXPERTS_PER_GROUP - 1.0 - lo) * 0.5 + (hi - lo - 1.0)
    bucket = gidx * float(PAIRS_PER_GROUP) + pair
    return jnp.concatenate([bucket, w_lo, w_hi, jnp.zeros((LANES - 3, lt.shape[1]), F32)], axis=0)


def _outproj_kernel(*refs, n_experts, chunks, tiles_per_batch, with_ctx):
    x_refs = refs[:chunks]
    (a_ref, b_ref, c_ref, d_ref, modc_ref, modb_ref, gain_ref, w_ref, wr_ref, br_ref,
     x1_ref, h2_ref, rt_ref) = refs[chunks:]
    d = x1_ref.shape[-1]
    for c in range(chunks):
        rs = slice(c * ROW_BLK, (c + 1) * ROW_BLK)
        mod = modb_ref[0]
        if with_ctx and c == 0:
            mod = jnp.where(pl.program_id(0) % tiles_per_batch == 0, modc_ref[0], mod)
        mix = jnp.concatenate([a_ref[rs, :], b_ref[rs, :], c_ref[rs, :], d_ref[rs, :]], axis=1)
        x1 = x_refs[c][...] + _tile_rows(mod[:, 2 * d:3 * d], ROW_BLK) * _dot(mix, w_ref[...])
        x1_ref[rs, :] = x1
        h2 = _norm_modulate(x1, gain_ref[...], mod, 3)
        _store_token_tiles(h2_ref, h2, c * ROW_BLK)
        lt = lax.dot_general(wr_ref[...], h2, (((1,), (1,)), ((), ())), precision=HIGHEST,
                             preferred_element_type=F32) + br_ref[...]
        rt_ref[rs, :] = _route(lt, n_experts).T


def _outproj(tokens, mixers, mod, gain, w_bf16, w_router_t, b_router_t, n_experts, batch, ctx_row):
    d = tokens.shape[1]
    n_out = mixers[0].shape[0]
    out_bpb = n_out // ROW_BLK // batch
    tok_bpb = tokens.shape[0] // ROW_BLK // batch
    off = tok_bpb - out_bpb
    chunks = max(c for c in (4, 3, 2, 1) if out_bpb % c == 0)
    tpb = out_bpb // chunks
    tr = chunks * ROW_BLK

    def x_spec(c):
        return pl.BlockSpec((ROW_BLK, d), lambda i: ((i // tpb) * tok_bpb + (i % tpb) * chunks + c + off, 0))

    row_spec = lambda w: pl.BlockSpec((tr, w), lambda i: (i, 0))
    full = lambda a: pl.BlockSpec(a.shape, lambda i: (0,) * a.ndim)
    mod_spec = lambda idx: pl.BlockSpec((1, SUBLANES, mod.shape[-1]), idx)
    return pl.pallas_call(
        functools.partial(_outproj_kernel, n_experts=n_experts, chunks=chunks, tiles_per_batch=tpb,
                          with_ctx=off == 0),
        grid=(n_out // tr,),
        in_specs=[x_spec(c) for c in range(chunks)] + [
            row_spec(GROUP_W), row_spec(GROUP_W), row_spec(GROUP_W), row_spec(GROUP_W),
            mod_spec(lambda i: (ctx_row, 0, 0)), mod_spec(lambda i: (i // tpb, 0, 0)),
            pl.BlockSpec((SUBLANES, d), lambda i: (0, 0)),
            full(w_bf16), full(w_router_t), full(b_router_t)],
        out_specs=[row_spec(d),
                   pl.BlockSpec((tr * TOKEN_SUB, LANES), lambda i: (i, 0)),
                   row_spec(LANES)],
        out_shape=[jax.ShapeDtypeStruct((n_out, d), F32),
                   jax.ShapeDtypeStruct((n_out * TOKEN_SUB, LANES), F32),
                   jax.ShapeDtypeStruct((n_out, LANES), F32)],
        compiler_params=_params("parallel"),
        name="out_proj_router",
    )(*([tokens] * chunks), *mixers, mod, mod, _rows8(gain), w_bf16, w_router_t, b_router_t)


def _store_token_tiles(ref, x, first_token=0):
    rows = x.shape[0]
    for j in range(TOKEN_SUB):
        ref[pl.ds(first_token * TOKEN_SUB + j, rows, stride=TOKEN_SUB), :] = x[:, j * LANES:(j + 1) * LANES]


def _load_token_tiles(ref, first_token=0, rows=None):
    rows = ref.shape[0] // TOKEN_SUB if rows is None else rows
    return jnp.concatenate([ref[pl.ds(first_token * TOKEN_SUB + j, rows, stride=TOKEN_SUB), :]
                            for j in range(TOKEN_SUB)], axis=-1)


def _token_rows(t):
    return pl.ds(pl.multiple_of(t * TOKEN_SUB, TOKEN_SUB), TOKEN_SUB)


def _token_dmas(n, make_copy):
    def start(r, carry):
        make_copy(r).start()
        return carry

    def wait(r, carry):
        make_copy(r).wait()
        return carry

    lax.fori_loop(0, n, start, 0, unroll=8)
    lax.fori_loop(0, n, wait, 0, unroll=8)


def _pack_bf16_pair(a, b):
    au = lax.bitcast_convert_type(a.astype(BF16).astype(F32), jnp.uint32)
    bu = lax.bitcast_convert_type(b.astype(BF16).astype(F32), jnp.uint32)
    return au | (bu >> 16)


def _unpack_bf16_pair(u):
    a = lax.bitcast_convert_type(u & jnp.uint32(0xFFFF0000), F32)
    b = lax.bitcast_convert_type(u << 16, F32)
    return a, b


def _moe_plan(route, n_buckets):
    n = route.shape[0]
    nt = n // ROW_BLK
    n_tiles = nt + n_buckets
    bucket = route[:, 0].astype(jnp.int32)
    onehot = (bucket[:, None] == jnp.arange(n_buckets, dtype=jnp.int32)[None, :]).astype(F32)
    onehot = onehot.reshape(nt, ROW_BLK, n_buckets)
    tile_counts = jnp.sum(onehot, axis=1)
    before = jnp.cumsum(tile_counts, axis=0) - tile_counts
    earlier = (jnp.arange(ROW_BLK)[:, None] > jnp.arange(ROW_BLK)[None, :]).astype(F32)
    rank = jnp.einsum('ij,tjb->tib', earlier, onehot) + before[:, None, :]
    counts = jnp.sum(tile_counts, axis=0)
    tiles = jnp.ceil(counts / ROW_BLK)
    tend = jnp.cumsum(tiles)
    tstart = tend - tiles
    pos = jnp.sum(onehot * (rank + tstart * ROW_BLK), axis=-1).astype(jnp.int32).reshape(n)
    tile_id = jnp.arange(n_tiles, dtype=F32)
    tile_valid = (tile_id < tend[-1]).astype(jnp.int32)
    tile_bucket = jnp.sum((tile_id[:, None] >= tend[None, :]).astype(jnp.int32), axis=1)
    last_bucket = jnp.max(jnp.where(tiles > 0, jnp.arange(n_buckets, dtype=jnp.int32), 0))
    tile_bucket = jnp.minimum(tile_bucket, last_bucket)
    group = tile_bucket // PAIRS_PER_GROUP
    pair = tile_bucket % PAIRS_PER_GROUP
    lo = (pair >= 3).astype(jnp.int32) + (pair >= 5).astype(jnp.int32)
    hi = jnp.where(pair < 3, pair + 1, jnp.where(pair < 5, pair - 1, 3))
    return group * EXPERTS_PER_GROUP + lo, group * EXPERTS_PER_GROUP + hi, tile_valid, pos


def _moe_scatter_kernel(pos_ref, h_ref, hs_in, hs_out, sem):
    del hs_in
    _token_dmas(h_ref.shape[0] // TOKEN_SUB,
                lambda r: pltpu.make_async_copy(h_ref.at[_token_rows(r)], hs_out.at[_token_rows(pos_ref[0, 0, r])],
                                                sem))


def _moe_scatter(pos, h2, n_sorted, tm):
    n = h2.shape[0] // TOKEN_SUB
    return pl.pallas_call(
        _moe_scatter_kernel,
        grid=(n // tm,),
        in_specs=[pl.BlockSpec((1, 1, tm), lambda i: (i, 0, 0), memory_space=pltpu.SMEM),
                  pl.BlockSpec((tm * TOKEN_SUB, LANES), lambda i: (i, 0)),
                  pl.BlockSpec(memory_space=pl.ANY)],
        out_specs=pl.BlockSpec(memory_space=pl.ANY),
        out_shape=jax.ShapeDtypeStruct((n_sorted * TOKEN_SUB, LANES), F32),
        scratch_shapes=[pltpu.SemaphoreType.DMA(())],
        input_output_aliases={2: 0},
        compiler_params=_params("arbitrary"),
        name="moe_scatter",
    )(pos.reshape(n // tm, 1, tm), h2, jnp.zeros((n_sorted * TOKEN_SUB, LANES), F32))


def _moe_expert_kernel(lo_ref, hi_ref, valid_ref, h_ref, wg_lo, wu_lo, wd_lo, wg_hi, wu_hi, wd_hi, y_ref):
    i = pl.program_id(0)

    @pl.when(valid_ref[i] == 0)
    def _():
        y_ref[...] = jnp.zeros_like(y_ref)

    @pl.when(valid_ref[i] == 1)
    def _():
        h = _load_token_tiles(h_ref).astype(BF16)

        def expert(wg, wu, wd):
            hidden = _silu(_dot(h, wg[0])) * _dot(h, wu[0])
            return _dot(hidden.astype(BF16), wd[0])

        _store_token_tiles(y_ref, _pack_bf16_pair(expert(wg_lo, wu_lo, wd_lo), expert(wg_hi, wu_hi, wd_hi)))


def _moe_experts(tile_lo, tile_hi, tile_valid, h_sorted, wg, wu, wd):
    n_tiles = tile_lo.shape[0]
    by_lo = lambda a: pl.BlockSpec((1,) + a.shape[1:], lambda i, lo, hi, v: (lo[i], 0, 0))
    by_hi = lambda a: pl.BlockSpec((1,) + a.shape[1:], lambda i, lo, hi, v: (hi[i], 0, 0))
    tile_spec = pl.BlockSpec((ROW_BLK * TOKEN_SUB, LANES), lambda i, lo, hi, v: (i, 0))
    grid_spec = pltpu.PrefetchScalarGridSpec(
        num_scalar_prefetch=3,
        grid=(n_tiles,),
        in_specs=[tile_spec, by_lo(wg), by_lo(wu), by_lo(wd), by_hi(wg), by_hi(wu), by_hi(wd)],
        out_specs=tile_spec,
    )
    return pl.pallas_call(
        _moe_expert_kernel,
        grid_spec=grid_spec,
        out_shape=jax.ShapeDtypeStruct((n_tiles * ROW_BLK * TOKEN_SUB, LANES), jnp.uint32),
        compiler_params=_params("parallel"),
        name="moe_experts",
    )(tile_lo, tile_hi, tile_valid, h_sorted, wg, wu, wd, wg, wu, wd)


def _moe_combine_kernel(pos_ref, pos_next_ref, y_hbm, rt_ref, x1_ref, modc_ref, modb_ref, o_ref, buf, sem,
                        *, ctx_len, tiles_per_batch):
    d = x1_ref.shape[-1]
    tm = x1_ref.shape[0]
    i = pl.program_id(0)
    half = i % 2

    def copies(idx_ref, h, op):
        def body(r, carry):
            op(pltpu.make_async_copy(y_hbm.at[_token_rows(idx_ref[0, 0, r])], buf.at[_token_rows(h * tm + r)],
                                     sem.at[h]))
            return carry
        lax.fori_loop(0, tm, body, 0, unroll=8)

    @pl.when(i == 0)
    def _():
        copies(pos_ref, 0, lambda cp: cp.start())

    @pl.when(i + 1 < pl.num_programs(0))
    def _():
        copies(pos_next_ref, 1 - half, lambda cp: cp.start())

    copies(pos_ref, half, lambda cp: cp.wait())
    y_lo, y_hi = _unpack_bf16_pair(_load_token_tiles(buf, half * tm, tm))
    rt = rt_ref[...]
    y = rt[:, 1:2] * y_lo + rt[:, 2:3] * y_hi
    gate = _tile_rows(modb_ref[0][:, 5 * d:6 * d], y.shape[0])
    if ctx_len:
        first = pl.program_id(0) % tiles_per_batch == 0
        row = lax.broadcasted_iota(jnp.int32, y.shape, 0)
        gate = jnp.where(first & (row < ctx_len), _tile_rows(modc_ref[0][:, 5 * d:6 * d], y.shape[0]), gate)
    o_ref[...] = x1_ref[...] + gate * y


def _moe_combine(pos, y_sorted, route, x1, mod, batch, ctx_len, ctx_row, with_ctx, tm):
    n, d = x1.shape
    tiles_per_batch = n // batch // tm
    n_tiles = n // tm
    row_spec = lambda w: pl.BlockSpec((tm, w), lambda i: (i, 0))
    pos = pos.reshape(n_tiles, 1, tm)
    return pl.pallas_call(
        functools.partial(_moe_combine_kernel, ctx_len=ctx_len if with_ctx else 0, tiles_per_batch=tiles_per_batch),
        grid=(n_tiles,),
        in_specs=[pl.BlockSpec((1, 1, tm), lambda i: (i, 0, 0), memory_space=pltpu.SMEM),
                  pl.BlockSpec((1, 1, tm), lambda i: (jnp.minimum(i + 1, n_tiles - 1), 0, 0),
                               memory_space=pltpu.SMEM),
                  pl.BlockSpec(memory_space=pl.ANY),
                  row_spec(LANES), row_spec(d),
                  pl.BlockSpec((1, SUBLANES, mod.shape[-1]), lambda i: (ctx_row, 0, 0)),
                  pl.BlockSpec((1, SUBLANES, mod.shape[-1]), lambda i: (i // tiles_per_batch, 0, 0))],
        out_specs=row_spec(d),
        out_shape=jax.ShapeDtypeStruct((n, d), F32),
        scratch_shapes=[pltpu.VMEM((2 * tm * TOKEN_SUB, LANES), jnp.uint32), pltpu.SemaphoreType.DMA((2,))],
        compiler_params=_params("arbitrary"),
        name="moe_combine",
    )(pos, pos, y_sorted, route, x1, mod, mod)


def kernel(x, c, ctx, c_ctx, ada_w, ada_b, norm_mix, norm_ffn, w_in, hgrn_lb_logits, hgrn_norm, conv_w,
           ret_decay_logit, ret_norm, q_norm, k_norm, w_out, router_group_w, router_group_b, router_expert_w,
           router_expert_b, expert_w_gate, expert_w_up, expert_w_down):
    batch, seq_len, d = x.shape
    ctx_len = ctx.shape[1]
    depth = ada_w.shape[0]
    n_experts = expert_w_gate.shape[1]
    n_groups = router_group_w.shape[-1]
    assert ctx_len == ROW_BLK and seq_len % ROW_BLK == 0 and batch + 1 <= MOD_ROWS
    assert n_experts + n_groups <= ROUTER_ROWS and n_experts == n_groups * EXPERTS_PER_GROUP
    assert d == TOKEN_SUB * LANES
    rows = ctx_len + seq_len
    bpb = rows // ROW_BLK
    ctx_row = batch

    cv = jnp.concatenate([c, c_ctx[None], jnp.zeros((MOD_ROWS - batch - 1, d), F32)], axis=0)
    mod_all = _modulation(cv, ada_w, ada_b)
    cos, sin_signed = _rope_tables(ctx_len, seq_len)
    tokens = jnp.concatenate([ctx, x], axis=1).reshape(batch * rows, d)

    for layer in range(depth):
        with_ctx = layer < depth - 1
        out_rows = rows if with_ctx else seq_len
        mod = jnp.broadcast_to(mod_all[layer][:, None, :], (MOD_ROWS, SUBLANES, 6 * d))
        pa, pb, pc, pd = _inproj(tokens, mod, norm_mix[layer], w_in[layer].astype(BF16), bpb, ctx_row)
        mixers = (
            _hgrn(pa, hgrn_lb_logits, hgrn_norm[layer], layer, batch, out_rows),
            _conv(pb, conv_w[layer], batch, ctx_len, out_rows),
            _retention(pc, ret_decay_logit[layer], ret_norm[layer], batch, out_rows),
            _attention(pd, cos, sin_signed, q_norm[layer], k_norm[layer], batch, out_rows),
        )
        pad = ROUTER_ROWS - n_experts - n_groups
        w_router_t = jnp.concatenate([router_expert_w[layer].T, router_group_w[layer].T, jnp.zeros((pad, d), F32)])
        b_router = jnp.concatenate([router_expert_b[layer], router_group_b[layer], jnp.zeros((pad,), F32)])
        b_router_t = jnp.broadcast_to(b_router[:, None], (ROUTER_ROWS, ROW_BLK))
        x1, h2, rt = _outproj(tokens, mixers, mod, norm_ffn[layer], w_out[layer].astype(BF16), w_router_t,
                              b_router_t, n_experts, batch, ctx_row)
        tile_lo, tile_hi, tile_valid, pos = _moe_plan(rt, n_groups * PAIRS_PER_GROUP)
        tm = max(t for t in range(ROW_BLK, 4 * ROW_BLK + 1, ROW_BLK) if out_rows % t == 0)
        h_sorted = _moe_scatter(pos, h2, tile_lo.shape[0] * ROW_BLK, tm)
        y_sorted = _moe_experts(tile_lo, tile_hi, tile_valid, h_sorted, expert_w_gate[layer].astype(BF16),
                                expert_w_up[layer].astype(BF16), expert_w_down[layer].astype(BF16))
        tokens = _moe_combine(pos, y_sorted, rt, x1, mod, batch, ctx_len, ctx_row, with_ctx, tm)
    return tokens.reshape(batch, seq_len, d)
```

```python
import functools

import jax
import jax.numpy as jnp
from jax import lax
from jax.experimental import pallas as pl
from jax.experimental.pallas import tpu as pltpu

F32 = jnp.float32
BF16 = jnp.bfloat16
HIGHEST = lax.Precision.HIGHEST

HEAD_DIM = 64
GROUP_W = 256
N_HEADS = GROUP_W // HEAD_DIM
KV_W = 128
GRID_W = 64
ROPE_THETA = 10000.0
EXPERTS_PER_GROUP = 4
NORM_EPS = 1e-6
ROW_BLK = 256
SUB_BLK = 8
LEVEL_ROWS = 128
KEY_BLK = 256
LANES = 128
SUBLANES = 8
BF16_SUBLANES = 16
LOG2_E = 1.4426950408889634
MOD_ROWS = 24
ROUTER_ROWS = 32
ROUTE_ROWS = 8
PAIRS_PER_GROUP = EXPERTS_PER_GROUP * (EXPERTS_PER_GROUP - 1) // 2
TOKEN_SUB = 8
VMEM_LIMIT_BYTES = 56 * 1024 * 1024
PA_W, PB_W, PC_W, PD_W = 5 * GROUP_W, 3 * GROUP_W, 4 * GROUP_W, GROUP_W + 2 * KV_W


def _params(*semantics):
    return pltpu.CompilerParams(dimension_semantics=semantics, vmem_limit_bytes=VMEM_LIMIT_BYTES)


def _dot(a, b):
    return jnp.dot(a, b, preferred_element_type=F32)


def _dot_nt(a, b):
    return lax.dot_general(a, b, (((1,), (1,)), ((), ())), preferred_element_type=F32)


def _dot_tn(a, b):
    return lax.dot_general(a, b, (((0,), (0,)), ((), ())), preferred_element_type=F32)


def _recip(x):
    return pl.reciprocal(x, approx=True)


def _sigmoid(x):
    return _recip(1.0 + jnp.exp(-x))


def _silu(x):
    return x * _sigmoid(x)


def _log_sigmoid(x):
    return jnp.minimum(x, 0.0) - jnp.log(1.0 + jnp.exp(-jnp.abs(x)))


def _split_bf16(x, terms):
    parts = []
    rem = x
    for i in range(terms):
        p = rem.astype(BF16)
        parts.append(p)
        if i + 1 < terms:
            rem = rem - p.astype(F32)
    return parts


def _head_ones(width):
    r = lax.broadcasted_iota(jnp.int32, (width, width), 0) // HEAD_DIM
    c = lax.broadcasted_iota(jnp.int32, (width, width), 1) // HEAD_DIM
    return r == c


def _head_masks(width):
    lane_head = lax.broadcasted_iota(jnp.int32, (1, width), 1) // HEAD_DIM
    return [lane_head == h for h in range(width // HEAD_DIM)]


def _head_expand(x, hmasks):
    zero = jnp.zeros_like(x)
    return jnp.concatenate([jnp.where(hm, x, zero) for hm in hmasks], axis=0).astype(BF16)


def _head_mean_sq(x, ones_bf16):
    hi, lo = _split_bf16(x * x, 2)
    return (_dot(hi, ones_bf16) + _dot(lo, ones_bf16)) * (1.0 / HEAD_DIM)


def _rows8(v):
    return jnp.broadcast_to(v.reshape(1, -1), (SUBLANES, v.size))


def _tile_rows(m8, rows):
    n = m8.shape[-1]
    return jnp.broadcast_to(m8[None], (rows // SUBLANES, SUBLANES, n)).reshape(rows, n)


def _head_rms_norm(x, gain8, ones_bf16):
    return x * lax.rsqrt(_head_mean_sq(x, ones_bf16) + NORM_EPS) * _tile_rows(gain8, x.shape[0])


def _norm_modulate(x, gain8, mod8, idx):
    rows, d = x.shape
    ms = jnp.mean(x * x, axis=-1, keepdims=True)
    y = x * lax.rsqrt(ms + NORM_EPS) * _tile_rows(gain8, rows)
    shift = _tile_rows(mod8[:, idx * d:(idx + 1) * d], rows)
    scale = _tile_rows(mod8[:, (idx + 1) * d:(idx + 2) * d], rows)
    return y * (1.0 + scale) + shift


def _scan_rows(j, nchunk):
    rf = pl.multiple_of(j * ROW_BLK, ROW_BLK)
    rb = pl.multiple_of(jnp.where(j == 0, 0, nchunk - j) * ROW_BLK, ROW_BLK)
    return rf, rb


def _mod_kernel(cv_ref, w_ref, b_ref, o_ref):
    s = _silu(cv_ref[...])
    o_ref[0] = jnp.dot(s, w_ref[0], precision=HIGHEST, preferred_element_type=F32) + b_ref[0]


def _modulation(cv, ada_w, ada_b):
    depth, d, n = ada_w.shape
    tn = n // 4
    return pl.pallas_call(
        _mod_kernel,
        grid=(depth, n // tn),
        in_specs=[pl.BlockSpec((MOD_ROWS, d), lambda l, j: (0, 0)),
                  pl.BlockSpec((1, d, tn), lambda l, j: (l, 0, j)),
                  pl.BlockSpec((1, 1, tn), lambda l, j: (l, 0, j))],
        out_specs=pl.BlockSpec((1, MOD_ROWS, tn), lambda l, j: (l, 0, j)),
        out_shape=jax.ShapeDtypeStruct((depth, MOD_ROWS, n), F32),
        compiler_params=_params("parallel", "parallel"),
        name="adaln_mod",
    )(cv, ada_w, ada_b.reshape(depth, 1, n))


def _inproj_kernel(x_ref, mod_ref, gain_ref, w_ref, pa_ref, pb_ref, pc_ref, pd_ref):
    h = _norm_modulate(x_ref[...], gain_ref[...], mod_ref[0], 0)
    p = _dot(h.astype(BF16), w_ref[...])
    pa_ref[...] = p[:, 0:PA_W]
    pb_ref[...] = p[:, PA_W:PA_W + PB_W].astype(pb_ref.dtype)
    pc_ref[...] = p[:, PA_W + PB_W:PA_W + PB_W + PC_W].astype(pc_ref.dtype)
    pd_ref[...] = p[:, PA_W + PB_W + PC_W:].astype(pd_ref.dtype)


def _inproj(tokens, mod, gain, w_bf16, blocks_per_batch, ctx_row):
    n, d = tokens.shape
    nblk = n // ROW_BLK

    def mod_idx(i):
        return (jnp.where(i % blocks_per_batch == 0, ctx_row, i // blocks_per_batch), 0, 0)

    outs = ((PA_W, F32), (PB_W, BF16), (PC_W, BF16), (PD_W, BF16))
    return pl.pallas_call(
        _inproj_kernel,
        grid=(nblk,),
        in_specs=[pl.BlockSpec((ROW_BLK, d), lambda i: (i, 0)),
                  pl.BlockSpec((1, SUBLANES, mod.shape[-1]), mod_idx),
                  pl.BlockSpec((SUBLANES, d), lambda i: (0, 0)),
                  pl.BlockSpec(w_bf16.shape, lambda i: (0, 0))],
        out_specs=[pl.BlockSpec((ROW_BLK, w), lambda i: (i, 0)) for w, _ in outs],
        out_shape=[jax.ShapeDtypeStruct((n, w), dt) for w, dt in outs],
        compiler_params=_params("parallel"),
        name="in_proj",
    )(tokens, mod, _rows8(gain), w_bf16)


def _hgrn_prepare(blk, lb_row, z_col, anti, s_ref, consts):
    tril, triu, bmask, hmasks, level_masks = consts
    w = GROUP_W
    c_len = blk.shape[0]
    q = blk[:, 0:w]
    v = blk[:, w:2 * w]
    z = blk[:, z_col * w:(z_col + 1) * w]
    f = lb_row + (1.0 - lb_row) * _sigmoid(z)
    g = jnp.log(f) * LOG2_E
    k = 1.0 - f
    tri = triu if anti else tril
    c = sum(_dot(tri, part) for part in _split_bf16(g, 3))
    tot = c[0:1] if anti else c[c_len - 1:c_len]

    s_t = s_ref[...]
    o = _dot_nt((q * jnp.exp2(c)).astype(BF16), s_t.astype(BF16))
    k_end = (k * jnp.exp2(tot - c)).astype(BF16)
    s_ref[...] = s_t * jnp.exp2(tot) + jnp.where(bmask, _dot_tn(v.astype(BF16), k_end), 0.0)

    v_exp = {}
    m = c_len // 2
    while m >= SUB_BLK:
        nb = c_len // (2 * m)
        mid = m if anti else m - 1
        refs = [jnp.broadcast_to(c[b * 2 * m + mid:b * 2 * m + mid + 1], (2 * m, w)) for b in range(nb)]
        ref = refs[0] if nb == 1 else jnp.concatenate(refs, axis=0)
        e = jnp.exp2(-jnp.abs(c - ref))
        qt = q * e
        kt = k * e
        pieces = []
        if 2 * m >= LEVEL_ROWS:
            for b in range(nb):
                first, second = b * 2 * m, b * 2 * m + m
                q0, k0 = (first, second) if anti else (second, first)
                a = _dot_nt(qt[q0:q0 + m].astype(BF16), _head_expand(kt[k0:k0 + m], hmasks))
                ob = _dot(a.astype(BF16), _head_expand(v[k0:k0 + m], hmasks))
                zero = jnp.zeros((m, w), F32)
                pieces += [ob, zero] if anti else [zero, ob]
        else:
            for r0 in range(0, c_len, LEVEL_ROWS):
                if r0 not in v_exp:
                    v_exp[r0] = _head_expand(v[r0:r0 + LEVEL_ROWS], hmasks)
                a = _dot_nt(qt[r0:r0 + LEVEL_ROWS].astype(BF16), _head_expand(kt[r0:r0 + LEVEL_ROWS], hmasks))
                a = jnp.where(level_masks[(m, anti)], a, 0.0).astype(BF16)
                pieces.append(_dot(a, v_exp[r0]))
        o = o + jnp.concatenate(pieces, axis=0)
        m //= 2
    return o, (c, q, k, v)


def _hgrn_level_masks(c_len):
    masks = {}
    rg = LEVEL_ROWS
    m = LEVEL_ROWS // 4
    while m >= SUB_BLK:
        t = lax.broadcasted_iota(jnp.int32, (rg, N_HEADS * rg), 0)
        s = lax.broadcasted_iota(jnp.int32, (rg, N_HEADS * rg), 1) % rg
        same = (t // (2 * m)) == (s // (2 * m))
        t_late = (t // m) % 2 == 1
        s_late = (s // m) % 2 == 1
        masks[(m, False)] = same & t_late & ~s_late
        masks[(m, True)] = same & ~t_late & s_late
        m //= 2
    return masks


def _hgrn_diag_unit(cqkv, r, anti, ones_bf16):
    cs, qs, ks, vs = (a[r:r + SUB_BLK] for a in cqkv)
    row = lax.broadcasted_iota(jnp.int32, cs.shape, 0)
    prods = []
    for s in range(SUB_BLK):
        valid = (row <= s) if anti else (row >= s)
        dec = jnp.exp2(jnp.where(valid, cs - cs[s:s + 1], -jnp.inf))
        prods.append(qs * dec * ks[s:s + 1])
    scores = _dot(jnp.concatenate(prods, axis=0).astype(BF16), ones_bf16)
    od = scores[0:SUB_BLK] * vs[0:1]
    for s in range(1, SUB_BLK):
        od = od + scores[s * SUB_BLK:(s + 1) * SUB_BLK] * vs[s:s + 1]
    return od


def _hgrn_kernel(p_ref, lbl_ref, gain_ref, o_ref, acc, s_f, s_b, *, layer, out_off):
    w = GROUP_W
    c_len = ROW_BLK
    nchunk = p_ref.shape[0] // c_len
    depth = lbl_ref.shape[0]

    logits = [lbl_ref[l] for l in range(depth)]
    mx = functools.reduce(jnp.maximum, logits)
    exps = [jnp.exp(l - mx) for l in logits]
    lb = sum(exps[1:layer + 1], jnp.zeros_like(mx)) * _recip(sum(exps))

    ri = lax.broadcasted_iota(jnp.int32, (c_len, c_len), 0)
    ci = lax.broadcasted_iota(jnp.int32, (c_len, c_len), 1)
    tril = (ci <= ri).astype(BF16)
    triu = (ci >= ri).astype(BF16)
    bmask = _head_ones(w)
    ones_bf16 = bmask.astype(BF16)
    consts = (tril, triu, bmask, _head_masks(w), _hgrn_level_masks(c_len))

    acc[...] = jnp.zeros_like(acc)
    s_f[...] = jnp.zeros_like(s_f)
    s_b[...] = jnp.zeros_like(s_b)

    def step(j, carry):
        rf, rb = _scan_rows(j, nchunk)
        of, cf = _hgrn_prepare(p_ref[pl.ds(rf, c_len), :], lb[0:1], 2, False, s_f, consts)
        ob, cb = _hgrn_prepare(p_ref[pl.ds(rb, c_len), :], lb[1:2], 3, True, s_b, consts)
        starts = range(0, c_len, SUB_BLK)
        od_f = jnp.concatenate([_hgrn_diag_unit(cf, r, False, ones_bf16) for r in starts], axis=0)
        od_b = jnp.concatenate([_hgrn_diag_unit(cb, r, True, ones_bf16) for r in starts], axis=0)
        acc[pl.ds(rf, c_len), :] += of + od_f
        acc[pl.ds(rb, c_len), :] += ob + od_b
        return carry

    lax.fori_loop(0, nchunk, step, 0)

    o = acc[out_off:, :]
    gate = p_ref[out_off:, 4 * w:5 * w]
    o_ref[...] = (_head_rms_norm(o, gain_ref[...], ones_bf16) * _silu(gate)).astype(o_ref.dtype)


def _hgrn(pa, lb_logits, gain, layer, batch, out_rows):
    n, width = pa.shape
    rows = n // batch
    w = GROUP_W
    scr = lambda r: pltpu.VMEM((r, w), F32)
    return pl.pallas_call(
        functools.partial(_hgrn_kernel, layer=layer, out_off=rows - out_rows),
        grid=(batch,),
        in_specs=[pl.BlockSpec((rows, width), lambda b: (b, 0)),
                  pl.BlockSpec(lb_logits.shape, lambda b: (0, 0, 0)),
                  pl.BlockSpec((SUBLANES, w), lambda b: (0, 0))],
        out_specs=pl.BlockSpec((out_rows, w), lambda b: (b, 0)),
        out_shape=jax.ShapeDtypeStruct((batch * out_rows, w), BF16),
        scratch_shapes=[scr(rows), scr(w), scr(w)],
        compiler_params=_params("parallel"),
        name="hgrn2_mixer",
    )(pa, lb_logits, _rows8(gain))


def _conv_kernel(p_ref, w_ref, o_ref, *, ctx_len, out_off):
    w = GROUP_W
    p = p_ref[...].astype(F32)
    u = p[:, w:2 * w] * p[:, 2 * w:3 * w]
    n = u.shape[0]
    row = lax.broadcasted_iota(jnp.int32, u.shape, 0)
    prev = jnp.where((row == 0) | (row == ctx_len), 0.0, pltpu.roll(u, 1, 0))
    nxt = jnp.where((row == ctx_len - 1) | (row == n - 1), 0.0, pltpu.roll(u, n - 1, 0))
    cw = w_ref[...]
    y = p[:, 0:w] * (cw[0:1] * prev + cw[1:2] * u + cw[2:3] * nxt)
    o_ref[...] = y[out_off:].astype(o_ref.dtype)


def _conv(pb, conv_w, batch, ctx_len, out_rows):
    n, width = pb.shape
    rows = n // batch
    return pl.pallas_call(
        functools.partial(_conv_kernel, ctx_len=ctx_len, out_off=rows - out_rows),
        grid=(batch,),
        in_specs=[pl.BlockSpec((rows, width), lambda b: (b, 0)),
                  pl.BlockSpec(conv_w.shape, lambda b: (0, 0))],
        out_specs=pl.BlockSpec((out_rows, GROUP_W), lambda b: (b, 0)),
        out_shape=jax.ShapeDtypeStruct((batch * out_rows, GROUP_W), BF16),
        compiler_params=_params("parallel"),
        name="conv_mixer",
    )(pb, conv_w)


def _ret_kernel(p_ref, dl_ref, gain_ref, o_ref, acc, s_f, s_b, d2_scr, *, out_off):
    w = GROUP_W
    c_len = ROW_BLK
    nchunk = p_ref.shape[0] // c_len
    hmasks = _head_masks(w)
    bmask = _head_ones(w)
    ones_bf16 = bmask.astype(BF16)

    lg = _log_sigmoid(dl_ref[...])
    lane_w = lambda r: sum(jnp.where(hm, lg[r:r + 1, h:h + 1], 0.0) for h, hm in enumerate(hmasks))
    lgf, lgb = lane_w(0), lane_w(1)
    t = lax.broadcasted_iota(jnp.int32, (c_len, w), 0).astype(F32)
    scale = HEAD_DIM ** -0.5
    qf_dec = jnp.exp(lgf * (t + 1.0)) * scale
    kf_dec = jnp.exp(lgf * (c_len - 1.0 - t))
    qb_dec = jnp.exp(lgb * (c_len - t)) * scale
    kb_dec = jnp.exp(lgb * t)
    tot_f = jnp.exp(lgf * float(c_len))
    tot_b = jnp.exp(lgb * float(c_len))

    col = lax.broadcasted_iota(jnp.int32, (c_len, N_HEADS * c_len), 1)
    row = lax.broadcasted_iota(jnp.int32, (c_len, N_HEADS * c_len), 0)
    col_head = lax.broadcasted_iota(jnp.int32, (1, N_HEADS * c_len), 1) // c_len
    col_lg = lambda r: sum(jnp.where(col_head == h, lg[r:r + 1, h:h + 1], 0.0) for h in range(N_HEADS))
    dist = (row - col % c_len).astype(F32)
    d2_scr[...] = (jnp.where(dist >= 0.0, jnp.exp(col_lg(0) * jnp.maximum(dist, 0.0)), 0.0)
                   + jnp.where(dist <= 0.0, jnp.exp(col_lg(1) * jnp.maximum(-dist, 0.0)), 0.0))

    acc[...] = jnp.zeros_like(acc)
    s_f[...] = jnp.zeros_like(s_f)
    s_b[...] = jnp.zeros_like(s_b)

    def inter(q, k, v, q_dec, k_dec, tot, s_ref):
        s_t = s_ref[...]
        o = _dot_nt((q * q_dec).astype(BF16), s_t.astype(BF16))
        s_ref[...] = s_t * tot + jnp.where(bmask, _dot_tn(v, (k * k_dec).astype(BF16)), 0.0)
        return o

    def step(j, carry):
        rf, rb = _scan_rows(j, nchunk)
        blk = p_ref[pl.ds(rf, c_len), :]
        q = blk[:, 0:w].astype(F32)
        k = blk[:, w:2 * w]
        v = blk[:, 2 * w:3 * w]
        a = _dot_nt((q * scale).astype(BF16), _head_expand(k, hmasks)) * d2_scr[...]
        o = _dot(a.astype(BF16), _head_expand(v, hmasks))
        o = o + inter(q, k.astype(F32), v, qf_dec, kf_dec, tot_f, s_f)
        acc[pl.ds(rf, c_len), :] += o
        blk = p_ref[pl.ds(rb, c_len), :]
        acc[pl.ds(rb, c_len), :] += inter(blk[:, 0:w].astype(F32), blk[:, w:2 * w].astype(F32),
                                          blk[:, 2 * w:3 * w], qb_dec, kb_dec, tot_b, s_b)
        return carry

    lax.fori_loop(0, nchunk, step, 0)

    o = acc[out_off:, :]
    gate = p_ref[out_off:, 3 * w:4 * w].astype(F32)
    o_ref[...] = (_head_rms_norm(o, gain_ref[...], ones_bf16) * _silu(gate)).astype(o_ref.dtype)


def _retention(pc, decay_logit, gain, batch, out_rows):
    n, width = pc.shape
    rows = n // batch
    w = GROUP_W
    dl = jnp.zeros((8, LANES), F32).at[:decay_logit.shape[0], :decay_logit.shape[1]].set(decay_logit)
    return pl.pallas_call(
        functools.partial(_ret_kernel, out_off=rows - out_rows),
        grid=(batch,),
        in_specs=[pl.BlockSpec((rows, width), lambda b: (b, 0)),
                  pl.BlockSpec((8, LANES), lambda b: (0, 0)),
                  pl.BlockSpec((SUBLANES, w), lambda b: (0, 0))],
        out_specs=pl.BlockSpec((out_rows, w), lambda b: (b, 0)),
        out_shape=jax.ShapeDtypeStruct((batch * out_rows, w), BF16),
        scratch_shapes=[pltpu.VMEM((rows, w), F32), pltpu.VMEM((w, w), F32), pltpu.VMEM((w, w), F32),
                        pltpu.VMEM((ROW_BLK, N_HEADS * ROW_BLK), F32)],
        compiler_params=_params("parallel"),
        name="retention_mixer",
    )(pc, dl, _rows8(gain))


def _rope(x, cos, sin_signed):
    n = x.shape[-1]
    lane = lax.broadcasted_iota(jnp.int32, x.shape, 1)
    swapped = jnp.where(lane % 2 == 0, pltpu.roll(x, n - 1, 1), pltpu.roll(x, 1, 1))
    return x * cos + swapped * sin_signed


def _attn_kernel(q_ref, k_ref, v_ref, cos_ref, sin_ref, qn_ref, kn_ref, o_ref, k_buf, vt_buf, s_scr, *, qb_off):
    j = pl.program_id(1)
    nkb = k_ref.shape[0] // KEY_BLK

    @pl.when(j == 0)
    def _():
        ones_kv = _head_ones(KV_W).astype(BF16)
        kn = _head_rms_norm(k_ref[...].astype(F32), kn_ref[...], ones_kv)
        key_rows = k_ref.shape[0]
        k_buf[...] = _rope(kn, cos_ref[0:key_rows, 0:KV_W], sin_ref[0:key_rows, 0:KV_W]).astype(BF16)
        vt_buf[...] = v_ref[...].astype(F32).T.astype(BF16)

    qb = j + qb_off
    r0 = pl.multiple_of(qb * ROW_BLK, ROW_BLK)
    ones_q = _head_ones(GROUP_W).astype(BF16)
    qn = _head_rms_norm(q_ref[...].astype(F32), qn_ref[...], ones_q)
    qr = _rope(qn, cos_ref[pl.ds(r0, ROW_BLK), :], sin_ref[pl.ds(r0, ROW_BLK), :])
    qr = (qr * (HEAD_DIM ** -0.5 * LOG2_E)).astype(BF16)
    group = GROUP_W // KV_W
    ones_rows = jnp.ones((BF16_SUBLANES, KEY_BLK), BF16)
    n_kv = KV_W // HEAD_DIM
    ksl = lambda kv: slice(kv * HEAD_DIM, (kv + 1) * HEAD_DIM)
    q2 = [jnp.concatenate([qr[:, (kv * group + g) * HEAD_DIM:(kv * group + g + 1) * HEAD_DIM]
                           for g in range(group)], axis=0) for kv in range(n_kv)]

    def scores(kv, kb, m):
        s = _dot_nt(k_buf[kb * KEY_BLK:(kb + 1) * KEY_BLK, ksl(kv)], q2[kv])
        s_scr[kv, kb] = s
        bm = jnp.max(s, axis=0, keepdims=True)
        return bm if m is None else jnp.maximum(m, bm)

    def values(kv, kb, m, acc):
        p = jnp.exp2(s_scr[kv, kb] - m).astype(BF16)
        vt = jnp.concatenate([vt_buf[ksl(kv), kb * KEY_BLK:(kb + 1) * KEY_BLK], ones_rows], axis=0)
        return acc + _dot(vt, p)

    zero = jnp.zeros((HEAD_DIM + BF16_SUBLANES, group * ROW_BLK), F32)
    m_cur = None
    for kb in range(nkb):
        m_cur = scores(0, kb, m_cur)
    outs = []
    for kv in range(n_kv):
        m_next, acc = None, zero
        for kb in range(nkb):
            if kv + 1 < n_kv:
                m_next = scores(kv + 1, kb, m_next)
            acc = values(kv, kb, m_cur, acc)
        outs.append(acc[0:HEAD_DIM] * _recip(acc[HEAD_DIM:HEAD_DIM + 1]))
        m_cur = m_next
    o_t = jnp.concatenate(outs, axis=0).T
    for kv in range(n_kv):
        for g in range(group):
            h = kv * group + g
            o_ref[:, h * HEAD_DIM:(h + 1) * HEAD_DIM] = (
                o_t[g * ROW_BLK:(g + 1) * ROW_BLK, kv * HEAD_DIM:(kv + 1) * HEAD_DIM].astype(o_ref.dtype))


def _attention(pd, cos, sin_signed, q_norm, k_norm, batch, q_first, n_q, key_blks):
    n, width = pd.shape
    rows = n // batch
    bpb = rows // ROW_BLK
    assert bpb % key_blks == 0
    key_rows = key_blks * ROW_BLK
    kcol = GROUP_W // KV_W
    tile = lambda g, reps: _rows8(jnp.tile(g, reps))
    kv_spec = lambda col: pl.BlockSpec((key_rows, KV_W), lambda b, j: (b * (bpb // key_blks), col))
    return pl.pallas_call(
        functools.partial(_attn_kernel, qb_off=q_first),
        grid=(batch, n_q),
        in_specs=[pl.BlockSpec((ROW_BLK, GROUP_W), lambda b, j: (b * bpb + j + q_first, 0)),
                  kv_spec(kcol), kv_spec(kcol + 1),
                  pl.BlockSpec((rows, GROUP_W), lambda b, j: (0, 0)),
                  pl.BlockSpec((rows, GROUP_W), lambda b, j: (0, 0)),
                  pl.BlockSpec((SUBLANES, GROUP_W), lambda b, j: (0, 0)),
                  pl.BlockSpec((SUBLANES, KV_W), lambda b, j: (0, 0))],
        out_specs=pl.BlockSpec((ROW_BLK, GROUP_W), lambda b, j: (b * n_q + j, 0)),
        out_shape=jax.ShapeDtypeStruct((batch * n_q * ROW_BLK, GROUP_W), BF16),
        scratch_shapes=[pltpu.VMEM((key_rows, KV_W), BF16), pltpu.VMEM((KV_W, key_rows), BF16),
                        pltpu.VMEM((KV_W // HEAD_DIM, key_rows // KEY_BLK, KEY_BLK, (GROUP_W // KV_W) * ROW_BLK),
                                   F32)],
        compiler_params=_params("parallel", "arbitrary"),
        name="attention_mixer",
    )(pd, pd, pd, cos, sin_signed, tile(q_norm, N_HEADS), tile(k_norm, KV_W // HEAD_DIM))


def _attention_mixer(pd, cos, sin_signed, q_norm, k_norm, batch, ctx_len, with_ctx):
    rows = pd.shape[0] // batch
    bpb = rows // ROW_BLK
    cb = ctx_len // ROW_BLK
    gx = _attention(pd, cos, sin_signed, q_norm, k_norm, batch, cb, bpb - cb, bpb)
    if not with_ctx:
        return gx
    gc = _attention(pd, cos, sin_signed, q_norm, k_norm, batch, 0, cb, cb)
    return jnp.concatenate([gc.reshape(batch, ctx_len, GROUP_W), gx.reshape(batch, rows - ctx_len, GROUP_W)],
                           axis=1).reshape(batch * rows, GROUP_W)


def _rope_tables(ctx_len, seq_len):
    t = jnp.arange(seq_len)
    row = (t // GRID_W).astype(F32)
    col = (t % GRID_W).astype(F32)
    n_freq = HEAD_DIM // 4
    inv_freq = ROPE_THETA ** (-jnp.arange(n_freq, dtype=F32) / n_freq)
    ang = jnp.concatenate([row[:, None] * inv_freq, col[:, None] * inv_freq], axis=-1)
    cos = jnp.repeat(jnp.cos(ang), 2, axis=-1)
    sin = jnp.stack([-jnp.sin(ang), jnp.sin(ang)], axis=-1).reshape(seq_len, HEAD_DIM)
    cos = jnp.concatenate([jnp.ones((ctx_len, HEAD_DIM), F32), cos], axis=0)
    sin = jnp.concatenate([jnp.zeros((ctx_len, HEAD_DIM), F32), sin], axis=0)
    return jnp.tile(cos, (1, N_HEADS)), jnp.tile(sin, (1, N_HEADS))


def _route(lt, n_experts):
    n_groups = n_experts // EXPERTS_PER_GROUP
    first = lambda x, hit, n: jnp.min(jnp.where(hit, x, float(n)), axis=0, keepdims=True)

    gl = lt[n_experts:n_experts + n_groups]
    gexp = jnp.exp(gl - jnp.max(gl, axis=0, keepdims=True))
    gprob = gexp * _recip(jnp.sum(gexp, axis=0, keepdims=True))
    group_p = jnp.max(gprob, axis=0, keepdims=True)
    grow = lax.broadcasted_iota(jnp.int32, gl.shape, 0).astype(F32)
    gidx = first(grow, gprob == group_p, n_groups)

    el = sum(jnp.where(gidx == float(g), lt[g * EXPERTS_PER_GROUP:(g + 1) * EXPERTS_PER_GROUP], 0.0)
             for g in range(n_groups))
    eexp = jnp.exp(el - jnp.max(el, axis=0, keepdims=True))
    eprob = eexp * _recip(jnp.sum(eexp, axis=0, keepdims=True))
    erow = lax.broadcasted_iota(jnp.int32, el.shape, 0).astype(F32)
    p1 = jnp.max(eprob, axis=0, keepdims=True)
    i1 = first(erow, eprob == p1, EXPERTS_PER_GROUP)
    rest = erow != i1
    p2 = jnp.max(jnp.where(rest, eprob, -1.0), axis=0, keepdims=True)
    i2 = first(erow, rest & (eprob == p2), EXPERTS_PER_GROUP)
    scale = group_p * _recip(p1 + p2)
    lo = jnp.minimum(i1, i2)
    hi = jnp.maximum(i1, i2)
    w_lo = jnp.where(i1 < i2, p1, p2) * scale
    w_hi = jnp.where(i1 < i2, p2, p1) * scale
    pair = lo * (2.0 * EXPERTS_PER_GROUP - 1.0 - lo) * 0.5 + (hi - lo - 1.0)
    bucket = gidx * float(PAIRS_PER_GROUP) + pair
    return jnp.concatenate([bucket, w_lo, w_hi, jnp.zeros((LANES - 3, lt.shape[1]), F32)], axis=0)


def _outproj_kernel(*refs, n_experts, chunks, tiles_per_batch, with_ctx):
    x_refs = refs[:chunks]
    (a_ref, b_ref, c_ref, d_ref, modc_ref, modb_ref, gain_ref, w_ref, wr_ref, br_ref,
     x1_ref, h2_ref, rt_ref) = refs[chunks:]
    d = x1_ref.shape[-1]
    for c in range(chunks):
        rs = slice(c * ROW_BLK, (c + 1) * ROW_BLK)
        mod = modb_ref[0]
        if with_ctx and c == 0:
            mod = jnp.where(pl.program_id(0) % tiles_per_batch == 0, modc_ref[0], mod)
        mix = jnp.concatenate([a_ref[rs, :], b_ref[rs, :], c_ref[rs, :], d_ref[rs, :]], axis=1)
        x1 = x_refs[c][...] + _tile_rows(mod[:, 2 * d:3 * d], ROW_BLK) * _dot(mix, w_ref[...])
        x1_ref[rs, :] = x1
        h2 = _norm_modulate(x1, gain_ref[...], mod, 3)
        _store_token_tiles(h2_ref, h2, c * ROW_BLK)
        lt = lax.dot_general(wr_ref[...], h2, (((1,), (1,)), ((), ())), precision=HIGHEST,
                             preferred_element_type=F32) + br_ref[...]
        rt_ref[rs, :] = _route(lt, n_experts).T


def _outproj(tokens, mixers, mod, gain, w_bf16, w_router_t, b_router_t, n_experts, batch, ctx_row):
    d = tokens.shape[1]
    n_out = mixers[0].shape[0]
    out_bpb = n_out // ROW_BLK // batch
    tok_bpb = tokens.shape[0] // ROW_BLK // batch
    off = tok_bpb - out_bpb
    chunks = max(c for c in (4, 3, 2, 1) if out_bpb % c == 0)
    tpb = out_bpb // chunks
    tr = chunks * ROW_BLK

    def x_spec(c):
        return pl.BlockSpec((ROW_BLK, d), lambda i: ((i // tpb) * tok_bpb + (i % tpb) * chunks + c + off, 0))

    row_spec = lambda w: pl.BlockSpec((tr, w), lambda i: (i, 0))
    full = lambda a: pl.BlockSpec(a.shape, lambda i: (0,) * a.ndim)
    mod_spec = lambda idx: pl.BlockSpec((1, SUBLANES, mod.shape[-1]), idx)
    return pl.pallas_call(
        functools.partial(_outproj_kernel, n_experts=n_experts, chunks=chunks, tiles_per_batch=tpb,
                          with_ctx=off == 0),
        grid=(n_out // tr,),
        in_specs=[x_spec(c) for c in range(chunks)] + [
            row_spec(GROUP_W), row_spec(GROUP_W), row_spec(GROUP_W), row_spec(GROUP_W),
            mod_spec(lambda i: (ctx_row, 0, 0)), mod_spec(lambda i: (i // tpb, 0, 0)),
            pl.BlockSpec((SUBLANES, d), lambda i: (0, 0)),
            full(w_bf16), full(w_router_t), full(b_router_t)],
        out_specs=[row_spec(d),
                   pl.BlockSpec((tr * TOKEN_SUB, LANES), lambda i: (i, 0)),
                   row_spec(LANES)],
        out_shape=[jax.ShapeDtypeStruct((n_out, d), F32),
                   jax.ShapeDtypeStruct((n_out * TOKEN_SUB, LANES), F32),
                   jax.ShapeDtypeStruct((n_out, LANES), F32)],
        compiler_params=_params("parallel"),
        name="out_proj_router",
    )(*([tokens] * chunks), *mixers, mod, mod, _rows8(gain), w_bf16, w_router_t, b_router_t)


def _store_token_tiles(ref, x, first_token=0):
    rows = x.shape[0]
    for j in range(TOKEN_SUB):
        ref[pl.ds(first_token * TOKEN_SUB + j, rows, stride=TOKEN_SUB), :] = x[:, j * LANES:(j + 1) * LANES]


def _load_token_tiles(ref, first_token=0, rows=None):
    rows = ref.shape[0] // TOKEN_SUB if rows is None else rows
    return jnp.concatenate([ref[pl.ds(first_token * TOKEN_SUB + j, rows, stride=TOKEN_SUB), :]
                            for j in range(TOKEN_SUB)], axis=-1)


def _token_rows(t):
    return pl.ds(pl.multiple_of(t * TOKEN_SUB, TOKEN_SUB), TOKEN_SUB)


def _token_dmas(n, make_copy):
    def start(r, carry):
        make_copy(r).start()
        return carry

    def wait(r, carry):
        make_copy(r).wait()
        return carry

    lax.fori_loop(0, n, start, 0, unroll=8)
    lax.fori_loop(0, n, wait, 0, unroll=8)


def _pack_bf16_pair(a, b):
    au = lax.bitcast_convert_type(a.astype(BF16).astype(F32), jnp.uint32)
    bu = lax.bitcast_convert_type(b.astype(BF16).astype(F32), jnp.uint32)
    return au | (bu >> 16)


def _unpack_bf16_pair(u):
    a = lax.bitcast_convert_type(u & jnp.uint32(0xFFFF0000), F32)
    b = lax.bitcast_convert_type(u << 16, F32)
    return a, b


def _moe_plan(route, n_buckets):
    n = route.shape[0]
    nt = n // ROW_BLK
    n_tiles = nt + n_buckets
    bucket = route[:, 0].astype(jnp.int32)
    onehot = (bucket[:, None] == jnp.arange(n_buckets, dtype=jnp.int32)[None, :]).astype(F32)
    onehot = onehot.reshape(nt, ROW_BLK, n_buckets)
    tile_counts = jnp.sum(onehot, axis=1)
    before = jnp.cumsum(tile_counts, axis=0) - tile_counts
    earlier = (jnp.arange(ROW_BLK)[:, None] > jnp.arange(ROW_BLK)[None, :]).astype(F32)
    rank = jnp.einsum('ij,tjb->tib', earlier, onehot) + before[:, None, :]
    counts = jnp.sum(tile_counts, axis=0)
    tiles = jnp.ceil(counts / ROW_BLK)
    tend = jnp.cumsum(tiles)
    tstart = tend - tiles
    pos = jnp.sum(onehot * (rank + tstart * ROW_BLK), axis=-1).astype(jnp.int32).reshape(n)
    tile_id = jnp.arange(n_tiles, dtype=F32)
    tile_valid = (tile_id < tend[-1]).astype(jnp.int32)
    tile_bucket = jnp.sum((tile_id[:, None] >= tend[None, :]).astype(jnp.int32), axis=1)
    last_bucket = jnp.max(jnp.where(tiles > 0, jnp.arange(n_buckets, dtype=jnp.int32), 0))
    tile_bucket = jnp.minimum(tile_bucket, last_bucket)
    group = tile_bucket // PAIRS_PER_GROUP
    pair = tile_bucket % PAIRS_PER_GROUP
    lo = (pair >= 3).astype(jnp.int32) + (pair >= 5).astype(jnp.int32)
    hi = jnp.where(pair < 3, pair + 1, jnp.where(pair < 5, pair - 1, 3))
    return group * EXPERTS_PER_GROUP + lo, group * EXPERTS_PER_GROUP + hi, tile_valid, pos


def _moe_scatter_kernel(pos_ref, h_ref, hs_in, hs_out, sem):
    del hs_in
    _token_dmas(h_ref.shape[0] // TOKEN_SUB,
                lambda r: pltpu.make_async_copy(h_ref.at[_token_rows(r)], hs_out.at[_token_rows(pos_ref[0, 0, r])],
                                                sem))


def _moe_scatter(pos, h2, n_sorted, tm):
    n = h2.shape[0] // TOKEN_SUB
    return pl.pallas_call(
        _moe_scatter_kernel,
        grid=(n // tm,),
        in_specs=[pl.BlockSpec((1, 1, tm), lambda i: (i, 0, 0), memory_space=pltpu.SMEM),
                  pl.BlockSpec((tm * TOKEN_SUB, LANES), lambda i: (i, 0)),
                  pl.BlockSpec(memory_space=pl.ANY)],
        out_specs=pl.BlockSpec(memory_space=pl.ANY),
        out_shape=jax.ShapeDtypeStruct((n_sorted * TOKEN_SUB, LANES), F32),
        scratch_shapes=[pltpu.SemaphoreType.DMA(())],
        input_output_aliases={2: 0},
        compiler_params=_params("arbitrary"),
        name="moe_scatter",
    )(pos.reshape(n // tm, 1, tm), h2, jnp.zeros((n_sorted * TOKEN_SUB, LANES), F32))


def _moe_expert_kernel(lo_ref, hi_ref, valid_ref, h_ref, wg_lo, wu_lo, wd_lo, wg_hi, wu_hi, wd_hi, y_ref):
    i = pl.program_id(0)

    @pl.when(valid_ref[i] == 0)
    def _():
        y_ref[...] = jnp.zeros_like(y_ref)

    @pl.when(valid_ref[i] == 1)
    def _():
        h = _load_token_tiles(h_ref).astype(BF16)

        def expert(wg, wu, wd):
            hidden = _silu(_dot(h, wg[0])) * _dot(h, wu[0])
            return _dot(hidden.astype(BF16), wd[0])

        _store_token_tiles(y_ref, _pack_bf16_pair(expert(wg_lo, wu_lo, wd_lo), expert(wg_hi, wu_hi, wd_hi)))


def _moe_experts(tile_lo, tile_hi, tile_valid, h_sorted, wg, wu, wd):
    n_tiles = tile_lo.shape[0]
    by_lo = lambda a: pl.BlockSpec((1,) + a.shape[1:], lambda i, lo, hi, v: (lo[i], 0, 0))
    by_hi = lambda a: pl.BlockSpec((1,) + a.shape[1:], lambda i, lo, hi, v: (hi[i], 0, 0))
    tile_spec = pl.BlockSpec((ROW_BLK * TOKEN_SUB, LANES), lambda i, lo, hi, v: (i, 0))
    grid_spec = pltpu.PrefetchScalarGridSpec(
        num_scalar_prefetch=3,
        grid=(n_tiles,),
        in_specs=[tile_spec, by_lo(wg), by_lo(wu), by_lo(wd), by_hi(wg), by_hi(wu), by_hi(wd)],
        out_specs=tile_spec,
    )
    return pl.pallas_call(
        _moe_expert_kernel,
        grid_spec=grid_spec,
        out_shape=jax.ShapeDtypeStruct((n_tiles * ROW_BLK * TOKEN_SUB, LANES), jnp.uint32),
        compiler_params=_params("parallel"),
        name="moe_experts",
    )(tile_lo, tile_hi, tile_valid, h_sorted, wg, wu, wd, wg, wu, wd)


def _moe_combine_kernel(pos_ref, pos_next_ref, y_hbm, rt_ref, x1_ref, modc_ref, modb_ref, o_ref, buf, sem,
                        *, ctx_len, tiles_per_batch):
    d = x1_ref.shape[-1]
    tm = x1_ref.shape[0]
    i = pl.program_id(0)
    half = i % 2

    def copies(idx_ref, h, op):
        def body(r, carry):
            op(pltpu.make_async_copy(y_hbm.at[_token_rows(idx_ref[0, 0, r])], buf.at[_token_rows(h * tm + r)],
                                     sem.at[h]))
            return carry
        lax.fori_loop(0, tm, body, 0, unroll=8)

    @pl.when(i == 0)
    def _():
        copies(pos_ref, 0, lambda cp: cp.start())

    @pl.when(i + 1 < pl.num_programs(0))
    def _():
        copies(pos_next_ref, 1 - half, lambda cp: cp.start())

    copies(pos_ref, half, lambda cp: cp.wait())
    y_lo, y_hi = _unpack_bf16_pair(_load_token_tiles(buf, half * tm, tm))
    rt = rt_ref[...]
    y = rt[:, 1:2] * y_lo + rt[:, 2:3] * y_hi
    gate = _tile_rows(modb_ref[0][:, 5 * d:6 * d], y.shape[0])
    if ctx_len:
        first = pl.program_id(0) % tiles_per_batch == 0
        row = lax.broadcasted_iota(jnp.int32, y.shape, 0)
        gate = jnp.where(first & (row < ctx_len), _tile_rows(modc_ref[0][:, 5 * d:6 * d], y.shape[0]), gate)
    o_ref[...] = x1_ref[...] + gate * y


def _moe_combine(pos, y_sorted, route, x1, mod, batch, ctx_len, ctx_row, with_ctx, tm):
    n, d = x1.shape
    tiles_per_batch = n // batch // tm
    n_tiles = n // tm
    row_spec = lambda w: pl.BlockSpec((tm, w), lambda i: (i, 0))
    pos = pos.reshape(n_tiles, 1, tm)
    return pl.pallas_call(
        functools.partial(_moe_combine_kernel, ctx_len=ctx_len if with_ctx else 0, tiles_per_batch=tiles_per_batch),
        grid=(n_tiles,),
        in_specs=[pl.BlockSpec((1, 1, tm), lambda i: (i, 0, 0), memory_space=pltpu.SMEM),
                  pl.BlockSpec((1, 1, tm), lambda i: (jnp.minimum(i + 1, n_tiles - 1), 0, 0),
                               memory_space=pltpu.SMEM),
                  pl.BlockSpec(memory_space=pl.ANY),
                  row_spec(LANES), row_spec(d),
                  pl.BlockSpec((1, SUBLANES, mod.shape[-1]), lambda i: (ctx_row, 0, 0)),
                  pl.BlockSpec((1, SUBLANES, mod.shape[-1]), lambda i: (i // tiles_per_batch, 0, 0))],
        out_specs=row_spec(d),
        out_shape=jax.ShapeDtypeStruct((n, d), F32),
        scratch_shapes=[pltpu.VMEM((2 * tm * TOKEN_SUB, LANES), jnp.uint32), pltpu.SemaphoreType.DMA((2,))],
        compiler_params=_params("arbitrary"),
        name="moe_combine",
    )(pos, pos, y_sorted, route, x1, mod, mod)


def kernel(x, c, ctx, c_ctx, ada_w, ada_b, norm_mix, norm_ffn, w_in, hgrn_lb_logits, hgrn_norm, conv_w,
           ret_decay_logit, ret_norm, q_norm, k_norm, w_out, router_group_w, router_group_b, router_expert_w,
           router_expert_b, expert_w_gate, expert_w_up, expert_w_down):
    batch, seq_len, d = x.shape
    ctx_len = ctx.shape[1]
    depth = ada_w.shape[0]
    n_experts = expert_w_gate.shape[1]
    n_groups = router_group_w.shape[-1]
    assert ctx_len == ROW_BLK and seq_len % ROW_BLK == 0 and batch + 1 <= MOD_ROWS
    assert n_experts + n_groups <= ROUTER_ROWS and n_experts == n_groups * EXPERTS_PER_GROUP
    assert d == TOKEN_SUB * LANES
    rows = ctx_len + seq_len
    bpb = rows // ROW_BLK
    ctx_row = batch

    cv = jnp.concatenate([c, c_ctx[None], jnp.zeros((MOD_ROWS - batch - 1, d), F32)], axis=0)
    mod_all = _modulation(cv, ada_w, ada_b)
    cos, sin_signed = _rope_tables(ctx_len, seq_len)
    tokens = jnp.concatenate([ctx, x], axis=1).reshape(batch * rows, d)

    for layer in range(depth):
        with_ctx = layer < depth - 1
        out_rows = rows if with_ctx else seq_len
        mod = jnp.broadcast_to(mod_all[layer][:, None, :], (MOD_ROWS, SUBLANES, 6 * d))
        pa, pb, pc, pd = _inproj(tokens, mod, norm_mix[layer], w_in[layer].astype(BF16), bpb, ctx_row)
        mixers = (
            _hgrn(pa, hgrn_lb_logits, hgrn_norm[layer], layer, batch, out_rows),
            _conv(pb, conv_w[layer], batch, ctx_len, out_rows),
            _retention(pc, ret_decay_logit[layer], ret_norm[layer], batch, out_rows),
            _attention_mixer(pd, cos, sin_signed, q_norm[layer], k_norm[layer], batch, ctx_len, with_ctx),
        )
        pad = ROUTER_ROWS - n_experts - n_groups
        w_router_t = jnp.concatenate([router_expert_w[layer].T, router_group_w[layer].T, jnp.zeros((pad, d), F32)])
        b_router = jnp.concatenate([router_expert_b[layer], router_group_b[layer], jnp.zeros((pad,), F32)])
        b_router_t = jnp.broadcast_to(b_router[:, None], (ROUTER_ROWS, ROW_BLK))
        x1, h2, rt = _outproj(tokens, mixers, mod, norm_ffn[layer], w_out[layer].astype(BF16), w_router_t,
                              b_router_t, n_experts, batch, ctx_row)
        tile_lo, tile_hi, tile_valid, pos = _moe_plan(rt, n_groups * PAIRS_PER_GROUP)
        tm = max(t for t in range(ROW_BLK, 4 * ROW_BLK + 1, ROW_BLK) if out_rows % t == 0)
        h_sorted = _moe_scatter(pos, h2, tile_lo.shape[0] * ROW_BLK, tm)
        y_sorted = _moe_experts(tile_lo, tile_hi, tile_valid, h_sorted, expert_w_gate[layer].astype(BF16),
                                expert_w_up[layer].astype(BF16), expert_w_down[layer].astype(BF16))
        tokens = _moe_combine(pos, y_sorted, rt, x1, mod, batch, ctx_len, ctx_row, with_ctx, tm)
    return tokens.reshape(batch, seq_len, d)
```

```python
import functools

import jax
import jax.numpy as jnp
from jax import lax
from jax.experimental import pallas as pl
from jax.experimental.pallas import tpu as pltpu

F32 = jnp.float32
BF16 = jnp.bfloat16
HIGHEST = lax.Precision.HIGHEST

HEAD_DIM = 64
GROUP_W = 256
N_HEADS = GROUP_W // HEAD_DIM
KV_W = 128
GRID_W = 64
ROPE_THETA = 10000.0
EXPERTS_PER_GROUP = 4
NORM_EPS = 1e-6
ROW_BLK = 256
SUB_BLK = 8
LEVEL_ROWS = 128
KEY_BLK = 256
LANES = 128
SUBLANES = 8
BF16_SUBLANES = 16
LOG2_E = 1.4426950408889634
MOD_ROWS = 24
ROUTER_ROWS = 32
ROUTE_ROWS = 8
PAIRS_PER_GROUP = EXPERTS_PER_GROUP * (EXPERTS_PER_GROUP - 1) // 2
TOKEN_SUB = 8
VMEM_LIMIT_BYTES = 56 * 1024 * 1024
PA_W, PB_W, PC_W, PD_W = 5 * GROUP_W, 3 * GROUP_W, 4 * GROUP_W, GROUP_W + 2 * KV_W


def _params(*semantics):
    return pltpu.CompilerParams(dimension_semantics=semantics, vmem_limit_bytes=VMEM_LIMIT_BYTES)


def _dot(a, b):
    return jnp.dot(a, b, preferred_element_type=F32)


def _dot_nt(a, b):
    return lax.dot_general(a, b, (((1,), (1,)), ((), ())), preferred_element_type=F32)


def _dot_tn(a, b):
    return lax.dot_general(a, b, (((0,), (0,)), ((), ())), preferred_element_type=F32)


def _recip(x):
    return pl.reciprocal(x, approx=True)


def _sigmoid(x):
    return _recip(1.0 + jnp.exp(-x))


def _silu(x):
    return x * _sigmoid(x)


def _log_sigmoid(x):
    return jnp.minimum(x, 0.0) - jnp.log(1.0 + jnp.exp(-jnp.abs(x)))


def _split_bf16(x, terms):
    parts = []
    rem = x
    for i in range(terms):
        p = rem.astype(BF16)
        parts.append(p)
        if i + 1 < terms:
            rem = rem - p.astype(F32)
    return parts


def _head_ones(width):
    r = lax.broadcasted_iota(jnp.int32, (width, width), 0) // HEAD_DIM
    c = lax.broadcasted_iota(jnp.int32, (width, width), 1) // HEAD_DIM
    return r == c


def _head_masks(width):
    lane_head = lax.broadcasted_iota(jnp.int32, (1, width), 1) // HEAD_DIM
    return [lane_head == h for h in range(width // HEAD_DIM)]


def _head_expand(x, hmasks):
    zero = jnp.zeros_like(x)
    return jnp.concatenate([jnp.where(hm, x, zero) for hm in hmasks], axis=0).astype(BF16)


def _head_mean_sq(x, ones_bf16):
    hi, lo = _split_bf16(x * x, 2)
    return (_dot(hi, ones_bf16) + _dot(lo, ones_bf16)) * (1.0 / HEAD_DIM)


def _rows8(v):
    return jnp.broadcast_to(v.reshape(1, -1), (SUBLANES, v.size))


def _tile_rows(m8, rows):
    n = m8.shape[-1]
    return jnp.broadcast_to(m8[None], (rows // SUBLANES, SUBLANES, n)).reshape(rows, n)


def _head_rms_norm(x, gain8, ones_bf16):
    return x * lax.rsqrt(_head_mean_sq(x, ones_bf16) + NORM_EPS) * _tile_rows(gain8, x.shape[0])


def _norm_modulate(x, gain8, mod8, idx):
    rows, d = x.shape
    ms = jnp.mean(x * x, axis=-1, keepdims=True)
    y = x * lax.rsqrt(ms + NORM_EPS) * _tile_rows(gain8, rows)
    shift = _tile_rows(mod8[:, idx * d:(idx + 1) * d], rows)
    scale = _tile_rows(mod8[:, (idx + 1) * d:(idx + 2) * d], rows)
    return y * (1.0 + scale) + shift


def _scan_rows(j, nchunk):
    rf = pl.multiple_of(j * ROW_BLK, ROW_BLK)
    rb = pl.multiple_of(jnp.where(j == 0, 0, nchunk - j) * ROW_BLK, ROW_BLK)
    return rf, rb


def _mod_kernel(cv_ref, w_ref, b_ref, o_ref):
    s = _silu(cv_ref[...])
    o_ref[0] = jnp.dot(s, w_ref[0], precision=HIGHEST, preferred_element_type=F32) + b_ref[0]


def _modulation(cv, ada_w, ada_b):
    depth, d, n = ada_w.shape
    tn = n // 4
    return pl.pallas_call(
        _mod_kernel,
        grid=(depth, n // tn),
        in_specs=[pl.BlockSpec((MOD_ROWS, d), lambda l, j: (0, 0)),
                  pl.BlockSpec((1, d, tn), lambda l, j: (l, 0, j)),
                  pl.BlockSpec((1, 1, tn), lambda l, j: (l, 0, j))],
        out_specs=pl.BlockSpec((1, MOD_ROWS, tn), lambda l, j: (l, 0, j)),
        out_shape=jax.ShapeDtypeStruct((depth, MOD_ROWS, n), F32),
        compiler_params=_params("parallel", "parallel"),
        name="adaln_mod",
    )(cv, ada_w, ada_b.reshape(depth, 1, n))


def _inproj_kernel(x_ref, mod_ref, gain_ref, w_ref, pa_ref, pb_ref, pc_ref, pd_ref):
    h = _norm_modulate(x_ref[...], gain_ref[...], mod_ref[0], 0)
    p = _dot(h.astype(BF16), w_ref[...])
    pa_ref[...] = p[:, 0:PA_W]
    pb_ref[...] = p[:, PA_W:PA_W + PB_W].astype(pb_ref.dtype)
    pc_ref[...] = p[:, PA_W + PB_W:PA_W + PB_W + PC_W].astype(pc_ref.dtype)
    pd_ref[...] = p[:, PA_W + PB_W + PC_W:].astype(pd_ref.dtype)


def _inproj(tokens, mod, gain, w_bf16, blocks_per_batch, ctx_row):
    n, d = tokens.shape
    nblk = n // ROW_BLK

    def mod_idx(i):
        return (jnp.where(i % blocks_per_batch == 0, ctx_row, i // blocks_per_batch), 0, 0)

    outs = ((PA_W, F32), (PB_W, BF16), (PC_W, BF16), (PD_W, BF16))
    return pl.pallas_call(
        _inproj_kernel,
        grid=(nblk,),
        in_specs=[pl.BlockSpec((ROW_BLK, d), lambda i: (i, 0)),
                  pl.BlockSpec((1, SUBLANES, mod.shape[-1]), mod_idx),
                  pl.BlockSpec((SUBLANES, d), lambda i: (0, 0)),
                  pl.BlockSpec(w_bf16.shape, lambda i: (0, 0))],
        out_specs=[pl.BlockSpec((ROW_BLK, w), lambda i: (i, 0)) for w, _ in outs],
        out_shape=[jax.ShapeDtypeStruct((n, w), dt) for w, dt in outs],
        compiler_params=_params("parallel"),
        name="in_proj",
    )(tokens, mod, _rows8(gain), w_bf16)


def _hgrn_prepare(blk, lb_row, z_col, anti, s_ref, consts):
    tril, triu, bmask, hmasks, level_masks = consts
    w = GROUP_W
    c_len = blk.shape[0]
    q = blk[:, 0:w]
    v = blk[:, w:2 * w]
    z = blk[:, z_col * w:(z_col + 1) * w]
    f = lb_row + (1.0 - lb_row) * _sigmoid(z)
    g = jnp.log(f) * LOG2_E
    k = 1.0 - f
    tri = triu if anti else tril
    c = sum(_dot(tri, part) for part in _split_bf16(g, 3))
    tot = c[0:1] if anti else c[c_len - 1:c_len]

    s_t = s_ref[...]
    o = _dot_nt((q * jnp.exp2(c)).astype(BF16), s_t.astype(BF16))
    k_end = (k * jnp.exp2(tot - c)).astype(BF16)
    s_ref[...] = s_t * jnp.exp2(tot) + jnp.where(bmask, _dot_tn(v.astype(BF16), k_end), 0.0)

    jobs = []
    m = c_len // 2
    while m >= SUB_BLK:
        nb = c_len // (2 * m)
        mid = m if anti else m - 1
        refs = [jnp.broadcast_to(c[b * 2 * m + mid:b * 2 * m + mid + 1], (2 * m, w)) for b in range(nb)]
        ref = refs[0] if nb == 1 else jnp.concatenate(refs, axis=0)
        e = jnp.exp2(-jnp.abs(c - ref))
        qt = q * e
        kt = k * e
        if 2 * m >= LEVEL_ROWS:
            for b in range(nb):
                first, second = b * 2 * m, b * 2 * m + m
                q0, k0 = (first, second) if anti else (second, first)
                jobs.append((m, q0, k0, m, _dot_nt(qt[q0:q0 + m].astype(BF16), _head_expand(kt[k0:k0 + m], hmasks))))
        else:
            for r0 in range(0, c_len, LEVEL_ROWS):
                a = _dot_nt(qt[r0:r0 + LEVEL_ROWS].astype(BF16), _head_expand(kt[r0:r0 + LEVEL_ROWS], hmasks))
                jobs.append((m, r0, r0, LEVEL_ROWS, jnp.where(level_masks[(m, anti)], a, 0.0)))
        m //= 2
    v_exp = {}
    pieces = {}
    for m, q0, k0, n, a in jobs:
        if (k0, n) not in v_exp:
            v_exp[(k0, n)] = _head_expand(v[k0:k0 + n], hmasks)
        pieces.setdefault(m, {})[q0] = _dot(a.astype(BF16), v_exp[(k0, n)])
    for m, level in pieces.items():
        n = next(iter(level.values())).shape[0]
        zero = jnp.zeros((n, w), F32)
        o = o + jnp.concatenate([level.get(r0, zero) for r0 in range(0, c_len, n)], axis=0)
    return o, (c, q, k, v)


def _hgrn_level_masks(c_len):
    masks = {}
    rg = LEVEL_ROWS
    m = LEVEL_ROWS // 4
    while m >= SUB_BLK:
        t = lax.broadcasted_iota(jnp.int32, (rg, N_HEADS * rg), 0)
        s = lax.broadcasted_iota(jnp.int32, (rg, N_HEADS * rg), 1) % rg
        same = (t // (2 * m)) == (s // (2 * m))
        t_late = (t // m) % 2 == 1
        s_late = (s // m) % 2 == 1
        masks[(m, False)] = same & t_late & ~s_late
        masks[(m, True)] = same & ~t_late & s_late
        m //= 2
    return masks


def _hgrn_diag_unit(cqkv, r, anti, ones_bf16):
    cs, qs, ks, vs = (a[r:r + SUB_BLK] for a in cqkv)
    row = lax.broadcasted_iota(jnp.int32, cs.shape, 0)
    prods = []
    for s in range(SUB_BLK):
        valid = (row <= s) if anti else (row >= s)
        dec = jnp.exp2(jnp.where(valid, cs - cs[s:s + 1], -jnp.inf))
        prods.append(qs * dec * ks[s:s + 1])
    scores = _dot(jnp.concatenate(prods, axis=0).astype(BF16), ones_bf16)
    od = scores[0:SUB_BLK] * vs[0:1]
    for s in range(1, SUB_BLK):
        od = od + scores[s * SUB_BLK:(s + 1) * SUB_BLK] * vs[s:s + 1]
    return od


def _hgrn_kernel(p_ref, lbl_ref, gain_ref, o_ref, acc, s_f, s_b, *, layer, out_off):
    w = GROUP_W
    c_len = ROW_BLK
    nchunk = p_ref.shape[0] // c_len
    depth = lbl_ref.shape[0]

    logits = [lbl_ref[l] for l in range(depth)]
    mx = functools.reduce(jnp.maximum, logits)
    exps = [jnp.exp(l - mx) for l in logits]
    lb = sum(exps[1:layer + 1], jnp.zeros_like(mx)) * _recip(sum(exps))

    ri = lax.broadcasted_iota(jnp.int32, (c_len, c_len), 0)
    ci = lax.broadcasted_iota(jnp.int32, (c_len, c_len), 1)
    tril = (ci <= ri).astype(BF16)
    triu = (ci >= ri).astype(BF16)
    bmask = _head_ones(w)
    ones_bf16 = bmask.astype(BF16)
    consts = (tril, triu, bmask, _head_masks(w), _hgrn_level_masks(c_len))

    acc[...] = jnp.zeros_like(acc)
    s_f[...] = jnp.zeros_like(s_f)
    s_b[...] = jnp.zeros_like(s_b)

    def step(j, carry):
        rf, rb = _scan_rows(j, nchunk)
        of, cf = _hgrn_prepare(p_ref[pl.ds(rf, c_len), :], lb[0:1], 2, False, s_f, consts)
        ob, cb = _hgrn_prepare(p_ref[pl.ds(rb, c_len), :], lb[1:2], 3, True, s_b, consts)
        starts = range(0, c_len, SUB_BLK)
        od_f = jnp.concatenate([_hgrn_diag_unit(cf, r, False, ones_bf16) for r in starts], axis=0)
        od_b = jnp.concatenate([_hgrn_diag_unit(cb, r, True, ones_bf16) for r in starts], axis=0)
        acc[pl.ds(rf, c_len), :] += of + od_f
        acc[pl.ds(rb, c_len), :] += ob + od_b
        return carry

    lax.fori_loop(0, nchunk, step, 0)

    o = acc[out_off:, :]
    gate = p_ref[out_off:, 4 * w:5 * w]
    o_ref[...] = (_head_rms_norm(o, gain_ref[...], ones_bf16) * _silu(gate)).astype(o_ref.dtype)


def _hgrn(pa, lb_logits, gain, layer, batch, out_rows):
    n, width = pa.shape
    rows = n // batch
    w = GROUP_W
    scr = lambda r: pltpu.VMEM((r, w), F32)
    return pl.pallas_call(
        functools.partial(_hgrn_kernel, layer=layer, out_off=rows - out_rows),
        grid=(batch,),
        in_specs=[pl.BlockSpec((rows, width), lambda b: (b, 0)),
                  pl.BlockSpec(lb_logits.shape, lambda b: (0, 0, 0)),
                  pl.BlockSpec((SUBLANES, w), lambda b: (0, 0))],
        out_specs=pl.BlockSpec((out_rows, w), lambda b: (b, 0)),
        out_shape=jax.ShapeDtypeStruct((batch * out_rows, w), BF16),
        scratch_shapes=[scr(rows), scr(w), scr(w)],
        compiler_params=_params("parallel"),
        name="hgrn2_mixer",
    )(pa, lb_logits, _rows8(gain))


def _conv_kernel(p_ref, w_ref, o_ref, *, ctx_len, out_off):
    w = GROUP_W
    p = p_ref[...].astype(F32)
    u = p[:, w:2 * w] * p[:, 2 * w:3 * w]
    n = u.shape[0]
    row = lax.broadcasted_iota(jnp.int32, u.shape, 0)
    prev = jnp.where((row == 0) | (row == ctx_len), 0.0, pltpu.roll(u, 1, 0))
    nxt = jnp.where((row == ctx_len - 1) | (row == n - 1), 0.0, pltpu.roll(u, n - 1, 0))
    cw = w_ref[...]
    y = p[:, 0:w] * (cw[0:1] * prev + cw[1:2] * u + cw[2:3] * nxt)
    o_ref[...] = y[out_off:].astype(o_ref.dtype)


def _conv(pb, conv_w, batch, ctx_len, out_rows):
    n, width = pb.shape
    rows = n // batch
    return pl.pallas_call(
        functools.partial(_conv_kernel, ctx_len=ctx_len, out_off=rows - out_rows),
        grid=(batch,),
        in_specs=[pl.BlockSpec((rows, width), lambda b: (b, 0)),
                  pl.BlockSpec(conv_w.shape, lambda b: (0, 0))],
        out_specs=pl.BlockSpec((out_rows, GROUP_W), lambda b: (b, 0)),
        out_shape=jax.ShapeDtypeStruct((batch * out_rows, GROUP_W), BF16),
        compiler_params=_params("parallel"),
        name="conv_mixer",
    )(pb, conv_w)


def _ret_kernel(p_ref, dl_ref, gain_ref, o_ref, acc, s_f, s_b, d2_scr, *, out_off):
    w = GROUP_W
    c_len = ROW_BLK
    nchunk = p_ref.shape[0] // c_len
    hmasks = _head_masks(w)
    bmask = _head_ones(w)
    ones_bf16 = bmask.astype(BF16)

    lg = _log_sigmoid(dl_ref[...])
    lane_w = lambda r: sum(jnp.where(hm, lg[r:r + 1, h:h + 1], 0.0) for h, hm in enumerate(hmasks))
    lgf, lgb = lane_w(0), lane_w(1)
    t = lax.broadcasted_iota(jnp.int32, (c_len, w), 0).astype(F32)
    scale = HEAD_DIM ** -0.5
    qf_dec = jnp.exp(lgf * (t + 1.0)) * scale
    kf_dec = jnp.exp(lgf * (c_len - 1.0 - t))
    qb_dec = jnp.exp(lgb * (c_len - t)) * scale
    kb_dec = jnp.exp(lgb * t)
    tot_f = jnp.exp(lgf * float(c_len))
    tot_b = jnp.exp(lgb * float(c_len))

    col = lax.broadcasted_iota(jnp.int32, (c_len, N_HEADS * c_len), 1)
    row = lax.broadcasted_iota(jnp.int32, (c_len, N_HEADS * c_len), 0)
    col_head = lax.broadcasted_iota(jnp.int32, (1, N_HEADS * c_len), 1) // c_len
    col_lg = lambda r: sum(jnp.where(col_head == h, lg[r:r + 1, h:h + 1], 0.0) for h in range(N_HEADS))
    dist = (row - col % c_len).astype(F32)
    d2_scr[...] = (jnp.where(dist >= 0.0, jnp.exp(col_lg(0) * jnp.maximum(dist, 0.0)), 0.0)
                   + jnp.where(dist <= 0.0, jnp.exp(col_lg(1) * jnp.maximum(-dist, 0.0)), 0.0))

    acc[...] = jnp.zeros_like(acc)
    s_f[...] = jnp.zeros_like(s_f)
    s_b[...] = jnp.zeros_like(s_b)

    def inter(q, k, v, q_dec, k_dec, tot, s_ref):
        s_t = s_ref[...]
        o = _dot_nt((q * q_dec).astype(BF16), s_t.astype(BF16))
        s_ref[...] = s_t * tot + jnp.where(bmask, _dot_tn(v, (k * k_dec).astype(BF16)), 0.0)
        return o

    def step(j, carry):
        rf, rb = _scan_rows(j, nchunk)
        blk = p_ref[pl.ds(rf, c_len), :]
        q = blk[:, 0:w].astype(F32)
        k = blk[:, w:2 * w]
        v = blk[:, 2 * w:3 * w]
        a = _dot_nt((q * scale).astype(BF16), _head_expand(k, hmasks)) * d2_scr[...]
        o = _dot(a.astype(BF16), _head_expand(v, hmasks))
        o = o + inter(q, k.astype(F32), v, qf_dec, kf_dec, tot_f, s_f)
        acc[pl.ds(rf, c_len), :] += o
        blk = p_ref[pl.ds(rb, c_len), :]
        acc[pl.ds(rb, c_len), :] += inter(blk[:, 0:w].astype(F32), blk[:, w:2 * w].astype(F32),
                                          blk[:, 2 * w:3 * w], qb_dec, kb_dec, tot_b, s_b)
        return carry

    lax.fori_loop(0, nchunk, step, 0)

    o = acc[out_off:, :]
    gate = p_ref[out_off:, 3 * w:4 * w].astype(F32)
    o_ref[...] = (_head_rms_norm(o, gain_ref[...], ones_bf16) * _silu(gate)).astype(o_ref.dtype)


def _retention(pc, decay_logit, gain, batch, out_rows):
    n, width = pc.shape
    rows = n // batch
    w = GROUP_W
    dl = jnp.zeros((8, LANES), F32).at[:decay_logit.shape[0], :decay_logit.shape[1]].set(decay_logit)
    return pl.pallas_call(
        functools.partial(_ret_kernel, out_off=rows - out_rows),
        grid=(batch,),
        in_specs=[pl.BlockSpec((rows, width), lambda b: (b, 0)),
                  pl.BlockSpec((8, LANES), lambda b: (0, 0)),
                  pl.BlockSpec((SUBLANES, w), lambda b: (0, 0))],
        out_specs=pl.BlockSpec((out_rows, w), lambda b: (b, 0)),
        out_shape=jax.ShapeDtypeStruct((batch * out_rows, w), BF16),
        scratch_shapes=[pltpu.VMEM((rows, w), F32), pltpu.VMEM((w, w), F32), pltpu.VMEM((w, w), F32),
                        pltpu.VMEM((ROW_BLK, N_HEADS * ROW_BLK), F32)],
        compiler_params=_params("parallel"),
        name="retention_mixer",
    )(pc, dl, _rows8(gain))


def _rope(x, cos, sin_signed):
    n = x.shape[-1]
    lane = lax.broadcasted_iota(jnp.int32, x.shape, 1)
    swapped = jnp.where(lane % 2 == 0, pltpu.roll(x, n - 1, 1), pltpu.roll(x, 1, 1))
    return x * cos + swapped * sin_signed


def _attn_kernel(q_ref, k_ref, v_ref, cos_ref, sin_ref, qn_ref, kn_ref, o_ref, k_buf, vt_buf, s_scr, *, qb_off):
    j = pl.program_id(1)
    nkb = k_ref.shape[0] // KEY_BLK

    @pl.when(j == 0)
    def _():
        ones_kv = _head_ones(KV_W).astype(BF16)
        kn = _head_rms_norm(k_ref[...].astype(F32), kn_ref[...], ones_kv)
        key_rows = k_ref.shape[0]
        k_buf[...] = _rope(kn, cos_ref[0:key_rows, 0:KV_W], sin_ref[0:key_rows, 0:KV_W]).astype(BF16)
        vt_buf[...] = v_ref[...].astype(F32).T.astype(BF16)

    qb = j + qb_off
    r0 = pl.multiple_of(qb * ROW_BLK, ROW_BLK)
    ones_q = _head_ones(GROUP_W).astype(BF16)
    qn = _head_rms_norm(q_ref[...].astype(F32), qn_ref[...], ones_q)
    qr = _rope(qn, cos_ref[pl.ds(r0, ROW_BLK), :], sin_ref[pl.ds(r0, ROW_BLK), :])
    qr = (qr * (HEAD_DIM ** -0.5 * LOG2_E)).astype(BF16)
    group = GROUP_W // KV_W
    ones_rows = jnp.ones((BF16_SUBLANES, KEY_BLK), BF16)
    n_kv = KV_W // HEAD_DIM
    ksl = lambda kv: slice(kv * HEAD_DIM, (kv + 1) * HEAD_DIM)
    q2 = [jnp.concatenate([qr[:, (kv * group + g) * HEAD_DIM:(kv * group + g + 1) * HEAD_DIM]
                           for g in range(group)], axis=0) for kv in range(n_kv)]

    def scores(kv, kb, m):
        s = _dot_nt(k_buf[kb * KEY_BLK:(kb + 1) * KEY_BLK, ksl(kv)], q2[kv])
        s_scr[kv, kb] = s
        bm = jnp.max(s, axis=0, keepdims=True)
        return bm if m is None else jnp.maximum(m, bm)

    def values(kv, kb, m, acc):
        p = jnp.exp2(s_scr[kv, kb] - m).astype(BF16)
        vt = jnp.concatenate([vt_buf[ksl(kv), kb * KEY_BLK:(kb + 1) * KEY_BLK], ones_rows], axis=0)
        return acc + _dot(vt, p)

    zero = jnp.zeros((HEAD_DIM + BF16_SUBLANES, group * ROW_BLK), F32)
    m_cur = None
    for kb in range(nkb):
        m_cur = scores(0, kb, m_cur)
    outs = []
    for kv in range(n_kv):
        m_next, acc = None, zero
        for kb in range(nkb):
            if kv + 1 < n_kv:
                m_next = scores(kv + 1, kb, m_next)
            acc = values(kv, kb, m_cur, acc)
        outs.append(acc[0:HEAD_DIM] * _recip(acc[HEAD_DIM:HEAD_DIM + 1]))
        m_cur = m_next
    o_t = jnp.concatenate(outs, axis=0).T
    for kv in range(n_kv):
        for g in range(group):
            h = kv * group + g
            o_ref[:, h * HEAD_DIM:(h + 1) * HEAD_DIM] = (
                o_t[g * ROW_BLK:(g + 1) * ROW_BLK, kv * HEAD_DIM:(kv + 1) * HEAD_DIM].astype(o_ref.dtype))


def _attention(pd, cos, sin_signed, q_norm, k_norm, batch, q_first, n_q, key_blks):
    n, width = pd.shape
    rows = n // batch
    bpb = rows // ROW_BLK
    assert bpb % key_blks == 0
    key_rows = key_blks * ROW_BLK
    kcol = GROUP_W // KV_W
    tile = lambda g, reps: _rows8(jnp.tile(g, reps))
    kv_spec = lambda col: pl.BlockSpec((key_rows, KV_W), lambda b, j: (b * (bpb // key_blks), col))
    return pl.pallas_call(
        functools.partial(_attn_kernel, qb_off=q_first),
        grid=(batch, n_q),
        in_specs=[pl.BlockSpec((ROW_BLK, GROUP_W), lambda b, j: (b * bpb + j + q_first, 0)),
                  kv_spec(kcol), kv_spec(kcol + 1),
                  pl.BlockSpec((rows, GROUP_W), lambda b, j: (0, 0)),
                  pl.BlockSpec((rows, GROUP_W), lambda b, j: (0, 0)),
                  pl.BlockSpec((SUBLANES, GROUP_W), lambda b, j: (0, 0)),
                  pl.BlockSpec((SUBLANES, KV_W), lambda b, j: (0, 0))],
        out_specs=pl.BlockSpec((ROW_BLK, GROUP_W), lambda b, j: (b * n_q + j, 0)),
        out_shape=jax.ShapeDtypeStruct((batch * n_q * ROW_BLK, GROUP_W), BF16),
        scratch_shapes=[pltpu.VMEM((key_rows, KV_W), BF16), pltpu.VMEM((KV_W, key_rows), BF16),
                        pltpu.VMEM((KV_W // HEAD_DIM, key_rows // KEY_BLK, KEY_BLK, (GROUP_W // KV_W) * ROW_BLK),
                                   F32)],
        compiler_params=_params("parallel", "arbitrary"),
        name="attention_mixer",
    )(pd, pd, pd, cos, sin_signed, tile(q_norm, N_HEADS), tile(k_norm, KV_W // HEAD_DIM))


def _attention_mixer(pd, cos, sin_signed, q_norm, k_norm, batch, ctx_len, with_ctx):
    rows = pd.shape[0] // batch
    bpb = rows // ROW_BLK
    cb = ctx_len // ROW_BLK
    gx = _attention(pd, cos, sin_signed, q_norm, k_norm, batch, cb, bpb - cb, bpb)
    if not with_ctx:
        return gx
    gc = _attention(pd, cos, sin_signed, q_norm, k_norm, batch, 0, cb, cb)
    return jnp.concatenate([gc.reshape(batch, ctx_len, GROUP_W), gx.reshape(batch, rows - ctx_len, GROUP_W)],
                           axis=1).reshape(batch * rows, GROUP_W)


def _rope_tables(ctx_len, seq_len):
    t = jnp.arange(seq_len)
    row = (t // GRID_W).astype(F32)
    col = (t % GRID_W).astype(F32)
    n_freq = HEAD_DIM // 4
    inv_freq = ROPE_THETA ** (-jnp.arange(n_freq, dtype=F32) / n_freq)
    ang = jnp.concatenate([row[:, None] * inv_freq, col[:, None] * inv_freq], axis=-1)
    cos = jnp.repeat(jnp.cos(ang), 2, axis=-1)
    sin = jnp.stack([-jnp.sin(ang), jnp.sin(ang)], axis=-1).reshape(seq_len, HEAD_DIM)
    cos = jnp.concatenate([jnp.ones((ctx_len, HEAD_DIM), F32), cos], axis=0)
    sin = jnp.concatenate([jnp.zeros((ctx_len, HEAD_DIM), F32), sin], axis=0)
    return jnp.tile(cos, (1, N_HEADS)), jnp.tile(sin, (1, N_HEADS))


def _route(lt, n_experts):
    n_groups = n_experts // EXPERTS_PER_GROUP
    first = lambda x, hit, n: jnp.min(jnp.where(hit, x, float(n)), axis=0, keepdims=True)

    gl = lt[n_experts:n_experts + n_groups]
    gexp = jnp.exp(gl - jnp.max(gl, axis=0, keepdims=True))
    gprob = gexp * _recip(jnp.sum(gexp, axis=0, keepdims=True))
    group_p = jnp.max(gprob, axis=0, keepdims=True)
    grow = lax.broadcasted_iota(jnp.int32, gl.shape, 0).astype(F32)
    gidx = first(grow, gprob == group_p, n_groups)

    el = sum(jnp.where(gidx == float(g), lt[g * EXPERTS_PER_GROUP:(g + 1) * EXPERTS_PER_GROUP], 0.0)
             for g in range(n_groups))
    eexp = jnp.exp(el - jnp.max(el, axis=0, keepdims=True))
    eprob = eexp * _recip(jnp.sum(eexp, axis=0, keepdims=True))
    erow = lax.broadcasted_iota(jnp.int32, el.shape, 0).astype(F32)
    p1 = jnp.max(eprob, axis=0, keepdims=True)
    i1 = first(erow, eprob == p1, EXPERTS_PER_GROUP)
    rest = erow != i1
    p2 = jnp.max(jnp.where(rest, eprob, -1.0), axis=0, keepdims=True)
    i2 = first(erow, rest & (eprob == p2), EXPERTS_PER_GROUP)
    scale = group_p * _recip(p1 + p2)
    lo = jnp.minimum(i1, i2)
    hi = jnp.maximum(i1, i2)
    w_lo = jnp.where(i1 < i2, p1, p2) * scale
    w_hi = jnp.where(i1 < i2, p2, p1) * scale
    pair = lo * (2.0 * EXPERTS_PER_GROUP - 1.0 - lo) * 0.5 + (hi - lo - 1.0)
    bucket = gidx * float(PAIRS_PER_GROUP) + pair
    return jnp.concatenate([bucket, w_lo, w_hi, jnp.zeros((LANES - 3, lt.shape[1]), F32)], axis=0)


def _outproj_kernel(*refs, n_experts, chunks, tiles_per_batch, with_ctx):
    x_refs = refs[:chunks]
    (a_ref, b_ref, c_ref, d_ref, modc_ref, modb_ref, gain_ref, w_ref, wr_ref, br_ref,
     x1_ref, h2_ref, rt_ref) = refs[chunks:]
    d = x1_ref.shape[-1]
    rows = [slice(c * ROW_BLK, (c + 1) * ROW_BLK) for c in range(chunks)]
    mods = [modb_ref[0]] * chunks
    if with_ctx:
        mods[0] = jnp.where(pl.program_id(0) % tiles_per_batch == 0, modc_ref[0], mods[0])
    projs = [_dot(jnp.concatenate([a_ref[rs, :], b_ref[rs, :], c_ref[rs, :], d_ref[rs, :]], axis=1), w_ref[...])
             for rs in rows]
    h2s = []
    for c, rs in enumerate(rows):
        x1 = x_refs[c][...] + _tile_rows(mods[c][:, 2 * d:3 * d], ROW_BLK) * projs[c]
        x1_ref[rs, :] = x1
        h2s.append(_norm_modulate(x1, gain_ref[...], mods[c], 3))
        _store_token_tiles(h2_ref, h2s[c], c * ROW_BLK)
    lts = [lax.dot_general(wr_ref[...], h2, (((1,), (1,)), ((), ())), precision=HIGHEST,
                           preferred_element_type=F32) + br_ref[...] for h2 in h2s]
    for rs, lt in zip(rows, lts):
        rt_ref[rs, :] = _route(lt, n_experts).T


def _outproj(tokens, mixers, mod, gain, w_bf16, w_router_t, b_router_t, n_experts, batch, ctx_row):
    d = tokens.shape[1]
    n_out = mixers[0].shape[0]
    out_bpb = n_out // ROW_BLK // batch
    tok_bpb = tokens.shape[0] // ROW_BLK // batch
    off = tok_bpb - out_bpb
    chunks = max(c for c in (4, 3, 2, 1) if out_bpb % c == 0)
    tpb = out_bpb // chunks
    tr = chunks * ROW_BLK

    def x_spec(c):
        return pl.BlockSpec((ROW_BLK, d), lambda i: ((i // tpb) * tok_bpb + (i % tpb) * chunks + c + off, 0))

    row_spec = lambda w: pl.BlockSpec((tr, w), lambda i: (i, 0))
    full = lambda a: pl.BlockSpec(a.shape, lambda i: (0,) * a.ndim)
    mod_spec = lambda idx: pl.BlockSpec((1, SUBLANES, mod.shape[-1]), idx)
    return pl.pallas_call(
        functools.partial(_outproj_kernel, n_experts=n_experts, chunks=chunks, tiles_per_batch=tpb,
                          with_ctx=off == 0),
        grid=(n_out // tr,),
        in_specs=[x_spec(c) for c in range(chunks)] + [
            row_spec(GROUP_W), row_spec(GROUP_W), row_spec(GROUP_W), row_spec(GROUP_W),
            mod_spec(lambda i: (ctx_row, 0, 0)), mod_spec(lambda i: (i // tpb, 0, 0)),
            pl.BlockSpec((SUBLANES, d), lambda i: (0, 0)),
            full(w_bf16), full(w_router_t), full(b_router_t)],
        out_specs=[row_spec(d),
                   pl.BlockSpec((tr * TOKEN_SUB, LANES), lambda i: (i, 0)),
                   row_spec(LANES)],
        out_shape=[jax.ShapeDtypeStruct((n_out, d), F32),
                   jax.ShapeDtypeStruct((n_out * TOKEN_SUB, LANES), F32),
                   jax.ShapeDtypeStruct((n_out, LANES), F32)],
        compiler_params=_params("parallel"),
        name="out_proj_router",
    )(*([tokens] * chunks), *mixers, mod, mod, _rows8(gain), w_bf16, w_router_t, b_router_t)


def _store_token_tiles(ref, x, first_token=0):
    rows = x.shape[0]
    for j in range(TOKEN_SUB):
        ref[pl.ds(first_token * TOKEN_SUB + j, rows, stride=TOKEN_SUB), :] = x[:, j * LANES:(j + 1) * LANES]


def _load_token_tiles(ref, first_token=0, rows=None):
    rows = ref.shape[0] // TOKEN_SUB if rows is None else rows
    return jnp.concatenate([ref[pl.ds(first_token * TOKEN_SUB + j, rows, stride=TOKEN_SUB), :]
                            for j in range(TOKEN_SUB)], axis=-1)


def _token_rows(t):
    return pl.ds(pl.multiple_of(t * TOKEN_SUB, TOKEN_SUB), TOKEN_SUB)


def _token_dmas(n, make_copy):
    def start(r, carry):
        make_copy(r).start()
        return carry

    def wait(r, carry):
        make_copy(r).wait()
        return carry

    lax.fori_loop(0, n, start, 0, unroll=8)
    lax.fori_loop(0, n, wait, 0, unroll=8)


def _pack_bf16_pair(a, b):
    au = lax.bitcast_convert_type(a.astype(BF16).astype(F32), jnp.uint32)
    bu = lax.bitcast_convert_type(b.astype(BF16).astype(F32), jnp.uint32)
    return au | (bu >> 16)


def _unpack_bf16_pair(u):
    a = lax.bitcast_convert_type(u & jnp.uint32(0xFFFF0000), F32)
    b = lax.bitcast_convert_type(u << 16, F32)
    return a, b


def _moe_plan(route, n_buckets):
    n = route.shape[0]
    nt = n // ROW_BLK
    n_tiles = nt + n_buckets
    bucket = route[:, 0].astype(jnp.int32)
    onehot = (bucket[:, None] == jnp.arange(n_buckets, dtype=jnp.int32)[None, :]).astype(F32)
    onehot = onehot.reshape(nt, ROW_BLK, n_buckets)
    tile_counts = jnp.sum(onehot, axis=1)
    before = jnp.cumsum(tile_counts, axis=0) - tile_counts
    earlier = (jnp.arange(ROW_BLK)[:, None] > jnp.arange(ROW_BLK)[None, :]).astype(F32)
    rank = jnp.einsum('ij,tjb->tib', earlier, onehot) + before[:, None, :]
    counts = jnp.sum(tile_counts, axis=0)
    tiles = jnp.ceil(counts / ROW_BLK)
    tend = jnp.cumsum(tiles)
    tstart = tend - tiles
    pos = jnp.sum(onehot * (rank + tstart * ROW_BLK), axis=-1).astype(jnp.int32).reshape(n)
    tile_id = jnp.arange(n_tiles, dtype=F32)
    tile_valid = (tile_id < tend[-1]).astype(jnp.int32)
    tile_bucket = jnp.sum((tile_id[:, None] >= tend[None, :]).astype(jnp.int32), axis=1)
    last_bucket = jnp.max(jnp.where(tiles > 0, jnp.arange(n_buckets, dtype=jnp.int32), 0))
    tile_bucket = jnp.minimum(tile_bucket, last_bucket)
    group = tile_bucket // PAIRS_PER_GROUP
    pair = tile_bucket % PAIRS_PER_GROUP
    lo = (pair >= 3).astype(jnp.int32) + (pair >= 5).astype(jnp.int32)
    hi = jnp.where(pair < 3, pair + 1, jnp.where(pair < 5, pair - 1, 3))
    return group * EXPERTS_PER_GROUP + lo, group * EXPERTS_PER_GROUP + hi, tile_valid, pos


def _moe_scatter_kernel(pos_ref, h_ref, hs_in, hs_out, sem):
    del hs_in
    _token_dmas(h_ref.shape[0] // TOKEN_SUB,
                lambda r: pltpu.make_async_copy(h_ref.at[_token_rows(r)], hs_out.at[_token_rows(pos_ref[0, 0, r])],
                                                sem))


def _moe_scatter(pos, h2, n_sorted, tm):
    n = h2.shape[0] // TOKEN_SUB
    return pl.pallas_call(
        _moe_scatter_kernel,
        grid=(n // tm,),
        in_specs=[pl.BlockSpec((1, 1, tm), lambda i: (i, 0, 0), memory_space=pltpu.SMEM),
                  pl.BlockSpec((tm * TOKEN_SUB, LANES), lambda i: (i, 0)),
                  pl.BlockSpec(memory_space=pl.ANY)],
        out_specs=pl.BlockSpec(memory_space=pl.ANY),
        out_shape=jax.ShapeDtypeStruct((n_sorted * TOKEN_SUB, LANES), F32),
        scratch_shapes=[pltpu.SemaphoreType.DMA(())],
        input_output_aliases={2: 0},
        compiler_params=_params("arbitrary"),
        name="moe_scatter",
    )(pos.reshape(n // tm, 1, tm), h2, jnp.zeros((n_sorted * TOKEN_SUB, LANES), F32))


def _moe_expert_kernel(lo_ref, hi_ref, valid_ref, h_ref, wg_lo, wu_lo, wd_lo, wg_hi, wu_hi, wd_hi, y_ref):
    i = pl.program_id(0)

    @pl.when(valid_ref[i] == 0)
    def _():
        y_ref[...] = jnp.zeros_like(y_ref)

    @pl.when(valid_ref[i] == 1)
    def _():
        h = _load_token_tiles(h_ref).astype(BF16)

        g_lo, u_lo = _dot(h, wg_lo[0]), _dot(h, wu_lo[0])
        g_hi, u_hi = _dot(h, wg_hi[0]), _dot(h, wu_hi[0])
        y_lo = _dot((_silu(g_lo) * u_lo).astype(BF16), wd_lo[0])
        y_hi = _dot((_silu(g_hi) * u_hi).astype(BF16), wd_hi[0])
        _store_token_tiles(y_ref, _pack_bf16_pair(y_lo, y_hi))


def _moe_experts(tile_lo, tile_hi, tile_valid, h_sorted, wg, wu, wd):
    n_tiles = tile_lo.shape[0]
    by_lo = lambda a: pl.BlockSpec((1,) + a.shape[1:], lambda i, lo, hi, v: (lo[i], 0, 0))
    by_hi = lambda a: pl.BlockSpec((1,) + a.shape[1:], lambda i, lo, hi, v: (hi[i], 0, 0))
    tile_spec = pl.BlockSpec((ROW_BLK * TOKEN_SUB, LANES), lambda i, lo, hi, v: (i, 0))
    grid_spec = pltpu.PrefetchScalarGridSpec(
        num_scalar_prefetch=3,
        grid=(n_tiles,),
        in_specs=[tile_spec, by_lo(wg), by_lo(wu), by_lo(wd), by_hi(wg), by_hi(wu), by_hi(wd)],
        out_specs=tile_spec,
    )
    return pl.pallas_call(
        _moe_expert_kernel,
        grid_spec=grid_spec,
        out_shape=jax.ShapeDtypeStruct((n_tiles * ROW_BLK * TOKEN_SUB, LANES), jnp.uint32),
        compiler_params=_params("parallel"),
        name="moe_experts",
    )(tile_lo, tile_hi, tile_valid, h_sorted, wg, wu, wd, wg, wu, wd)


def _moe_combine_kernel(pos_ref, pos_next_ref, y_hbm, rt_ref, x1_ref, modc_ref, modb_ref, o_ref, buf, sem,
                        *, ctx_len, tiles_per_batch):
    d = x1_ref.shape[-1]
    tm = x1_ref.shape[0]
    i = pl.program_id(0)
    half = i % 2

    def copies(idx_ref, h, op):
        def body(r, carry):
            op(pltpu.make_async_copy(y_hbm.at[_token_rows(idx_ref[0, 0, r])], buf.at[_token_rows(h * tm + r)],
                                     sem.at[h]))
            return carry
        lax.fori_loop(0, tm, body, 0, unroll=8)

    @pl.when(i == 0)
    def _():
        copies(pos_ref, 0, lambda cp: cp.start())

    @pl.when(i + 1 < pl.num_programs(0))
    def _():
        copies(pos_next_ref, 1 - half, lambda cp: cp.start())

    copies(pos_ref, half, lambda cp: cp.wait())
    y_lo, y_hi = _unpack_bf16_pair(_load_token_tiles(buf, half * tm, tm))
    rt = rt_ref[...]
    y = rt[:, 1:2] * y_lo + rt[:, 2:3] * y_hi
    gate = _tile_rows(modb_ref[0][:, 5 * d:6 * d], y.shape[0])
    if ctx_len:
        first = pl.program_id(0) % tiles_per_batch == 0
        row = lax.broadcasted_iota(jnp.int32, y.shape, 0)
        gate = jnp.where(first & (row < ctx_len), _tile_rows(modc_ref[0][:, 5 * d:6 * d], y.shape[0]), gate)
    o_ref[...] = x1_ref[...] + gate * y


def _moe_combine(pos, y_sorted, route, x1, mod, batch, ctx_len, ctx_row, with_ctx, tm):
    n, d = x1.shape
    tiles_per_batch = n // batch // tm
    n_tiles = n // tm
    row_spec = lambda w: pl.BlockSpec((tm, w), lambda i: (i, 0))
    pos = pos.reshape(n_tiles, 1, tm)
    return pl.pallas_call(
        functools.partial(_moe_combine_kernel, ctx_len=ctx_len if with_ctx else 0, tiles_per_batch=tiles_per_batch),
        grid=(n_tiles,),
        in_specs=[pl.BlockSpec((1, 1, tm), lambda i: (i, 0, 0), memory_space=pltpu.SMEM),
                  pl.BlockSpec((1, 1, tm), lambda i: (jnp.minimum(i + 1, n_tiles - 1), 0, 0),
                               memory_space=pltpu.SMEM),
                  pl.BlockSpec(memory_space=pl.ANY),
                  row_spec(LANES), row_spec(d),
                  pl.BlockSpec((1, SUBLANES, mod.shape[-1]), lambda i: (ctx_row, 0, 0)),
                  pl.BlockSpec((1, SUBLANES, mod.shape[-1]), lambda i: (i // tiles_per_batch, 0, 0))],
        out_specs=row_spec(d),
        out_shape=jax.ShapeDtypeStruct((n, d), F32),
        scratch_shapes=[pltpu.VMEM((2 * tm * TOKEN_SUB, LANES), jnp.uint32), pltpu.SemaphoreType.DMA((2,))],
        compiler_params=_params("arbitrary"),
        name="moe_combine",
    )(pos, pos, y_sorted, route, x1, mod, mod)


def kernel(x, c, ctx, c_ctx, ada_w, ada_b, norm_mix, norm_ffn, w_in, hgrn_lb_logits, hgrn_norm, conv_w,
           ret_decay_logit, ret_norm, q_norm, k_norm, w_out, router_group_w, router_group_b, router_expert_w,
           router_expert_b, expert_w_gate, expert_w_up, expert_w_down):
    batch, seq_len, d = x.shape
    ctx_len = ctx.shape[1]
    depth = ada_w.shape[0]
    n_experts = expert_w_gate.shape[1]
    n_groups = router_group_w.shape[-1]
    assert ctx_len == ROW_BLK and seq_len % ROW_BLK == 0 and batch + 1 <= MOD_ROWS
    assert n_experts + n_groups <= ROUTER_ROWS and n_experts == n_groups * EXPERTS_PER_GROUP
    assert d == TOKEN_SUB * LANES
    rows = ctx_len + seq_len
    bpb = rows // ROW_BLK
    ctx_row = batch

    cv = jnp.concatenate([c, c_ctx[None], jnp.zeros((MOD_ROWS - batch - 1, d), F32)], axis=0)
    mod_all = _modulation(cv, ada_w, ada_b)
    cos, sin_signed = _rope_tables(ctx_len, seq_len)
    tokens = jnp.concatenate([ctx, x], axis=1).reshape(batch * rows, d)

    for layer in range(depth):
        with_ctx = layer < depth - 1
        out_rows = rows if with_ctx else seq_len
        mod = jnp.broadcast_to(mod_all[layer][:, None, :], (MOD_ROWS, SUBLANES, 6 * d))
        pa, pb, pc, pd = _inproj(tokens, mod, norm_mix[layer], w_in[layer].astype(BF16), bpb, ctx_row)
        mixers = (
            _hgrn(pa, hgrn_lb_logits, hgrn_norm[layer], layer, batch, out_rows),
            _conv(pb, conv_w[layer], batch, ctx_len, out_rows),
            _retention(pc, ret_decay_logit[layer], ret_norm[layer], batch, out_rows),
            _attention_mixer(pd, cos, sin_signed, q_norm[layer], k_norm[layer], batch, ctx_len, with_ctx),
        )
        pad = ROUTER_ROWS - n_experts - n_groups
        w_router_t = jnp.concatenate([router_expert_w[layer].T, router_group_w[layer].T, jnp.zeros((pad, d), F32)])
        b_router = jnp.concatenate([router_expert_b[layer], router_group_b[layer], jnp.zeros((pad,), F32)])
        b_router_t = jnp.broadcast_to(b_router[:, None], (ROUTER_ROWS, ROW_BLK))
        x1, h2, rt = _outproj(tokens, mixers, mod, norm_ffn[layer], w_out[layer].astype(BF16), w_router_t,
                              b_router_t, n_experts, batch, ctx_row)
        tile_lo, tile_hi, tile_valid, pos = _moe_plan(rt, n_groups * PAIRS_PER_GROUP)
        tm = max(t for t in range(ROW_BLK, 4 * ROW_BLK + 1, ROW_BLK) if out_rows % t == 0)
        h_sorted = _moe_scatter(pos, h2, tile_lo.shape[0] * ROW_BLK, tm)
        y_sorted = _moe_experts(tile_lo, tile_hi, tile_valid, h_sorted, expert_w_gate[layer].astype(BF16),
                                expert_w_up[layer].astype(BF16), expert_w_down[layer].astype(BF16))
        tokens = _moe_combine(pos, y_sorted, rt, x1, mod, batch, ctx_len, ctx_row, with_ctx, tm)
    return tokens.reshape(batch, seq_len, d)
```

```python
import functools

import jax
import jax.numpy as jnp
from jax import lax
from jax.experimental import pallas as pl
from jax.experimental.pallas import tpu as pltpu

F32 = jnp.float32
BF16 = jnp.bfloat16
HIGHEST = lax.Precision.HIGHEST

HEAD_DIM = 64
GROUP_W = 256
N_HEADS = GROUP_W // HEAD_DIM
KV_W = 128
GRID_W = 64
ROPE_THETA = 10000.0
EXPERTS_PER_GROUP = 4
NORM_EPS = 1e-6
ROW_BLK = 256
SUB_BLK = 8
LEVEL_ROWS = 128
KEY_BLK = 256
LANES = 128
SUBLANES = 8
BF16_SUBLANES = 16
LOG2_E = 1.4426950408889634
MOD_ROWS = 24
ROUTER_ROWS = 32
ROUTE_ROWS = 8
PAIRS_PER_GROUP = EXPERTS_PER_GROUP * (EXPERTS_PER_GROUP - 1) // 2
TOKEN_SUB = 8
VMEM_LIMIT_BYTES = 56 * 1024 * 1024
PA_W, PB_W, PC_W, PD_W = 5 * GROUP_W, 3 * GROUP_W, 4 * GROUP_W, GROUP_W + 2 * KV_W


def _params(*semantics):
    return pltpu.CompilerParams(dimension_semantics=semantics, vmem_limit_bytes=VMEM_LIMIT_BYTES)


def _dot(a, b):
    return jnp.dot(a, b, preferred_element_type=F32)


def _dot_nt(a, b):
    return lax.dot_general(a, b, (((1,), (1,)), ((), ())), preferred_element_type=F32)


def _dot_tn(a, b):
    return lax.dot_general(a, b, (((0,), (0,)), ((), ())), preferred_element_type=F32)


def _recip(x):
    return pl.reciprocal(x, approx=True)


def _sigmoid(x):
    return _recip(1.0 + jnp.exp(-x))


def _silu(x):
    return x * _sigmoid(x)


def _log_sigmoid(x):
    return jnp.minimum(x, 0.0) - jnp.log(1.0 + jnp.exp(-jnp.abs(x)))


def _split_bf16(x, terms):
    parts = []
    rem = x
    for i in range(terms):
        p = rem.astype(BF16)
        parts.append(p)
        if i + 1 < terms:
            rem = rem - p.astype(F32)
    return parts


def _head_ones(width):
    r = lax.broadcasted_iota(jnp.int32, (width, width), 0) // HEAD_DIM
    c = lax.broadcasted_iota(jnp.int32, (width, width), 1) // HEAD_DIM
    return r == c


def _head_masks(width):
    lane_head = lax.broadcasted_iota(jnp.int32, (1, width), 1) // HEAD_DIM
    return [lane_head == h for h in range(width // HEAD_DIM)]


def _head_expand(x, hmasks):
    zero = jnp.zeros_like(x)
    return jnp.concatenate([jnp.where(hm, x, zero) for hm in hmasks], axis=0).astype(BF16)


def _head_mean_sq(x, ones_bf16):
    hi, lo = _split_bf16(x * x, 2)
    return (_dot(hi, ones_bf16) + _dot(lo, ones_bf16)) * (1.0 / HEAD_DIM)


def _rows8(v):
    return jnp.broadcast_to(v.reshape(1, -1), (SUBLANES, v.size))


def _tile_rows(m8, rows):
    n = m8.shape[-1]
    return jnp.broadcast_to(m8[None], (rows // SUBLANES, SUBLANES, n)).reshape(rows, n)


def _head_rms_norm(x, gain8, ones_bf16):
    return x * lax.rsqrt(_head_mean_sq(x, ones_bf16) + NORM_EPS) * _tile_rows(gain8, x.shape[0])


def _norm_modulate(x, gain8, mod8, idx):
    rows, d = x.shape
    ms = jnp.mean(x * x, axis=-1, keepdims=True)
    y = x * lax.rsqrt(ms + NORM_EPS) * _tile_rows(gain8, rows)
    shift = _tile_rows(mod8[:, idx * d:(idx + 1) * d], rows)
    scale = _tile_rows(mod8[:, (idx + 1) * d:(idx + 2) * d], rows)
    return y * (1.0 + scale) + shift


def _scan_rows(j, nchunk):
    rf = pl.multiple_of(j * ROW_BLK, ROW_BLK)
    rb = pl.multiple_of(jnp.where(j == 0, 0, nchunk - j) * ROW_BLK, ROW_BLK)
    return rf, rb


def _mod_kernel(cv_ref, w_ref, b_ref, o_ref):
    s = _silu(cv_ref[...])
    o_ref[0] = jnp.dot(s, w_ref[0], precision=HIGHEST, preferred_element_type=F32) + b_ref[0]


def _modulation(cv, ada_w, ada_b):
    depth, d, n = ada_w.shape
    tn = n // 4
    return pl.pallas_call(
        _mod_kernel,
        grid=(depth, n // tn),
        in_specs=[pl.BlockSpec((MOD_ROWS, d), lambda l, j: (0, 0)),
                  pl.BlockSpec((1, d, tn), lambda l, j: (l, 0, j)),
                  pl.BlockSpec((1, 1, tn), lambda l, j: (l, 0, j))],
        out_specs=pl.BlockSpec((1, MOD_ROWS, tn), lambda l, j: (l, 0, j)),
        out_shape=jax.ShapeDtypeStruct((depth, MOD_ROWS, n), F32),
        compiler_params=_params("parallel", "parallel"),
        name="adaln_mod",
    )(cv, ada_w, ada_b.reshape(depth, 1, n))


def _inproj_kernel(x_ref, mod_ref, gain_ref, w_ref, pa_ref, pb_ref, pc_ref, pd_ref):
    h = _norm_modulate(x_ref[...], gain_ref[...], mod_ref[0], 0)
    p = _dot(h.astype(BF16), w_ref[...])
    pa_ref[...] = p[:, 0:PA_W]
    pb_ref[...] = p[:, PA_W:PA_W + PB_W].astype(pb_ref.dtype)
    pc_ref[...] = p[:, PA_W + PB_W:PA_W + PB_W + PC_W].astype(pc_ref.dtype)
    pd_ref[...] = p[:, PA_W + PB_W + PC_W:].astype(pd_ref.dtype)


def _inproj(tokens, mod, gain, w_bf16, blocks_per_batch, ctx_row):
    n, d = tokens.shape
    nblk = n // ROW_BLK

    def mod_idx(i):
        return (jnp.where(i % blocks_per_batch == 0, ctx_row, i // blocks_per_batch), 0, 0)

    outs = ((PA_W, F32), (PB_W, BF16), (PC_W, BF16), (PD_W, BF16))
    return pl.pallas_call(
        _inproj_kernel,
        grid=(nblk,),
        in_specs=[pl.BlockSpec((ROW_BLK, d), lambda i: (i, 0)),
                  pl.BlockSpec((1, SUBLANES, mod.shape[-1]), mod_idx),
                  pl.BlockSpec((SUBLANES, d), lambda i: (0, 0)),
                  pl.BlockSpec(w_bf16.shape, lambda i: (0, 0))],
        out_specs=[pl.BlockSpec((ROW_BLK, w), lambda i: (i, 0)) for w, _ in outs],
        out_shape=[jax.ShapeDtypeStruct((n, w), dt) for w, dt in outs],
        compiler_params=_params("parallel"),
        name="in_proj",
    )(tokens, mod, _rows8(gain), w_bf16)


def _hgrn_prepare(blk, lb_row, z_col, anti, s_ref, consts):
    tril, triu, bmask, hmasks, level_masks = consts
    w = GROUP_W
    c_len = blk.shape[0]
    q = blk[:, 0:w]
    v = blk[:, w:2 * w]
    z = blk[:, z_col * w:(z_col + 1) * w]
    f = lb_row + (1.0 - lb_row) * _sigmoid(z)
    g = jnp.log(f) * LOG2_E
    k = 1.0 - f
    tri = triu if anti else tril
    c = sum(_dot(tri, part) for part in _split_bf16(g, 3))
    tot = c[0:1] if anti else c[c_len - 1:c_len]

    s_t = s_ref[...]
    o = _dot_nt((q * jnp.exp2(c)).astype(BF16), s_t.astype(BF16))
    k_end = (k * jnp.exp2(tot - c)).astype(BF16)
    s_ref[...] = s_t * jnp.exp2(tot) + jnp.where(bmask, _dot_tn(v.astype(BF16), k_end), 0.0)

    jobs = []
    m = c_len // 2
    while m >= SUB_BLK:
        nb = c_len // (2 * m)
        mid = m if anti else m - 1
        refs = [jnp.broadcast_to(c[b * 2 * m + mid:b * 2 * m + mid + 1], (2 * m, w)) for b in range(nb)]
        ref = refs[0] if nb == 1 else jnp.concatenate(refs, axis=0)
        e = jnp.exp2(-jnp.abs(c - ref))
        qt = q * e
        kt = k * e
        if 2 * m >= LEVEL_ROWS:
            for b in range(nb):
                first, second = b * 2 * m, b * 2 * m + m
                q0, k0 = (first, second) if anti else (second, first)
                jobs.append((m, q0, k0, m, _dot_nt(qt[q0:q0 + m].astype(BF16), _head_expand(kt[k0:k0 + m], hmasks))))
        else:
            for r0 in range(0, c_len, LEVEL_ROWS):
                a = _dot_nt(qt[r0:r0 + LEVEL_ROWS].astype(BF16), _head_expand(kt[r0:r0 + LEVEL_ROWS], hmasks))
                jobs.append((m, r0, r0, LEVEL_ROWS, jnp.where(level_masks[(m, anti)], a, 0.0)))
        m //= 2
    v_exp = {}
    pieces = {}
    for m, q0, k0, n, a in jobs:
        if (k0, n) not in v_exp:
            v_exp[(k0, n)] = _head_expand(v[k0:k0 + n], hmasks)
        pieces.setdefault(m, {})[q0] = _dot(a.astype(BF16), v_exp[(k0, n)])
    for m, level in pieces.items():
        n = next(iter(level.values())).shape[0]
        zero = jnp.zeros((n, w), F32)
        o = o + jnp.concatenate([level.get(r0, zero) for r0 in range(0, c_len, n)], axis=0)
    return o, (c, c - jnp.log(k) * LOG2_E, q, v)


def _hgrn_level_masks(c_len):
    masks = {}
    rg = LEVEL_ROWS
    m = LEVEL_ROWS // 4
    while m >= SUB_BLK:
        t = lax.broadcasted_iota(jnp.int32, (rg, N_HEADS * rg), 0)
        s = lax.broadcasted_iota(jnp.int32, (rg, N_HEADS * rg), 1) % rg
        same = (t // (2 * m)) == (s // (2 * m))
        t_late = (t // m) % 2 == 1
        s_late = (s // m) % 2 == 1
        masks[(m, False)] = same & t_late & ~s_late
        masks[(m, True)] = same & ~t_late & s_late
        m //= 2
    return masks


def _hgrn_diag_unit(cqkv, r, anti, ones_bf16):
    cs, cks, qs, vs = (a[r:r + SUB_BLK] for a in cqkv)
    row = lax.broadcasted_iota(jnp.int32, cs.shape, 0)
    prods = []
    for s in range(SUB_BLK):
        valid = (row <= s) if anti else (row >= s)
        prods.append(qs * jnp.exp2(jnp.where(valid, cs - cks[s:s + 1], -jnp.inf)))
    scores = _dot(jnp.concatenate(prods, axis=0).astype(BF16), ones_bf16)
    od = scores[0:SUB_BLK] * vs[0:1]
    for s in range(1, SUB_BLK):
        od = od + scores[s * SUB_BLK:(s + 1) * SUB_BLK] * vs[s:s + 1]
    return od


def _hgrn_kernel(p_ref, lbl_ref, gain_ref, o_ref, acc, s_f, s_b, *, layer, out_off):
    w = GROUP_W
    c_len = ROW_BLK
    nchunk = p_ref.shape[0] // c_len
    depth = lbl_ref.shape[0]

    logits = [lbl_ref[l] for l in range(depth)]
    mx = functools.reduce(jnp.maximum, logits)
    exps = [jnp.exp(l - mx) for l in logits]
    lb = sum(exps[1:layer + 1], jnp.zeros_like(mx)) * _recip(sum(exps))

    ri = lax.broadcasted_iota(jnp.int32, (c_len, c_len), 0)
    ci = lax.broadcasted_iota(jnp.int32, (c_len, c_len), 1)
    tril = (ci <= ri).astype(BF16)
    triu = (ci >= ri).astype(BF16)
    bmask = _head_ones(w)
    ones_bf16 = bmask.astype(BF16)
    consts = (tril, triu, bmask, _head_masks(w), _hgrn_level_masks(c_len))

    acc[...] = jnp.zeros_like(acc)
    s_f[...] = jnp.zeros_like(s_f)
    s_b[...] = jnp.zeros_like(s_b)

    def step(j, carry):
        rf, rb = _scan_rows(j, nchunk)
        of, cf = _hgrn_prepare(p_ref[pl.ds(rf, c_len), :], lb[0:1], 2, False, s_f, consts)
        ob, cb = _hgrn_prepare(p_ref[pl.ds(rb, c_len), :], lb[1:2], 3, True, s_b, consts)
        starts = range(0, c_len, SUB_BLK)
        od_f = jnp.concatenate([_hgrn_diag_unit(cf, r, False, ones_bf16) for r in starts], axis=0)
        od_b = jnp.concatenate([_hgrn_diag_unit(cb, r, True, ones_bf16) for r in starts], axis=0)
        acc[pl.ds(rf, c_len), :] += of + od_f
        acc[pl.ds(rb, c_len), :] += ob + od_b
        return carry

    lax.fori_loop(0, nchunk, step, 0)

    o = acc[out_off:, :]
    gate = p_ref[out_off:, 4 * w:5 * w]
    o_ref[...] = (_head_rms_norm(o, gain_ref[...], ones_bf16) * _silu(gate)).astype(o_ref.dtype)


def _hgrn(pa, lb_logits, gain, layer, batch, out_rows):
    n, width = pa.shape
    rows = n // batch
    w = GROUP_W
    scr = lambda r: pltpu.VMEM((r, w), F32)
    return pl.pallas_call(
        functools.partial(_hgrn_kernel, layer=layer, out_off=rows - out_rows),
        grid=(batch,),
        in_specs=[pl.BlockSpec((rows, width), lambda b: (b, 0)),
                  pl.BlockSpec(lb_logits.shape, lambda b: (0, 0, 0)),
                  pl.BlockSpec((SUBLANES, w), lambda b: (0, 0))],
        out_specs=pl.BlockSpec((out_rows, w), lambda b: (b, 0)),
        out_shape=jax.ShapeDtypeStruct((batch * out_rows, w), BF16),
        scratch_shapes=[scr(rows), scr(w), scr(w)],
        compiler_params=_params("parallel"),
        name="hgrn2_mixer",
    )(pa, lb_logits, _rows8(gain))


def _conv_kernel(p_ref, w_ref, o_ref, *, ctx_len, out_off):
    w = GROUP_W
    p = p_ref[...].astype(F32)
    u = p[:, w:2 * w] * p[:, 2 * w:3 * w]
    n = u.shape[0]
    row = lax.broadcasted_iota(jnp.int32, u.shape, 0)
    prev = jnp.where((row == 0) | (row == ctx_len), 0.0, pltpu.roll(u, 1, 0))
    nxt = jnp.where((row == ctx_len - 1) | (row == n - 1), 0.0, pltpu.roll(u, n - 1, 0))
    cw = w_ref[...]
    y = p[:, 0:w] * (cw[0:1] * prev + cw[1:2] * u + cw[2:3] * nxt)
    o_ref[...] = y[out_off:].astype(o_ref.dtype)


def _conv(pb, conv_w, batch, ctx_len, out_rows):
    n, width = pb.shape
    rows = n // batch
    return pl.pallas_call(
        functools.partial(_conv_kernel, ctx_len=ctx_len, out_off=rows - out_rows),
        grid=(batch,),
        in_specs=[pl.BlockSpec((rows, width), lambda b: (b, 0)),
                  pl.BlockSpec(conv_w.shape, lambda b: (0, 0))],
        out_specs=pl.BlockSpec((out_rows, GROUP_W), lambda b: (b, 0)),
        out_shape=jax.ShapeDtypeStruct((batch * out_rows, GROUP_W), BF16),
        compiler_params=_params("parallel"),
        name="conv_mixer",
    )(pb, conv_w)


def _ret_kernel(p_ref, dl_ref, gain_ref, o_ref, acc, s_f, s_b, d2_scr, *, out_off):
    w = GROUP_W
    c_len = ROW_BLK
    nchunk = p_ref.shape[0] // c_len
    hmasks = _head_masks(w)
    bmask = _head_ones(w)
    ones_bf16 = bmask.astype(BF16)

    lg = _log_sigmoid(dl_ref[...])
    lane_w = lambda r: sum(jnp.where(hm, lg[r:r + 1, h:h + 1], 0.0) for h, hm in enumerate(hmasks))
    lgf, lgb = lane_w(0), lane_w(1)
    t = lax.broadcasted_iota(jnp.int32, (c_len, w), 0).astype(F32)
    scale = HEAD_DIM ** -0.5
    qf_dec = jnp.exp(lgf * (t + 1.0)) * scale
    kf_dec = jnp.exp(lgf * (c_len - 1.0 - t))
    qb_dec = jnp.exp(lgb * (c_len - t)) * scale
    kb_dec = jnp.exp(lgb * t)
    tot_f = jnp.exp(lgf * float(c_len))
    tot_b = jnp.exp(lgb * float(c_len))

    @pl.when(pl.program_id(0) == 0)
    def _():
        col = lax.broadcasted_iota(jnp.int32, (c_len, N_HEADS * c_len), 1)
        row = lax.broadcasted_iota(jnp.int32, (c_len, N_HEADS * c_len), 0)
        col_head = lax.broadcasted_iota(jnp.int32, (1, N_HEADS * c_len), 1) // c_len
        col_lg = lambda r: sum(jnp.where(col_head == h, lg[r:r + 1, h:h + 1], 0.0) for h in range(N_HEADS))
        dist = (row - col % c_len).astype(F32)
        d2_scr[...] = (jnp.where(dist >= 0.0, jnp.exp(col_lg(0) * jnp.maximum(dist, 0.0)), 0.0)
                       + jnp.where(dist <= 0.0, jnp.exp(col_lg(1) * jnp.maximum(-dist, 0.0)), 0.0))

    acc[...] = jnp.zeros_like(acc)
    s_f[...] = jnp.zeros_like(s_f)
    s_b[...] = jnp.zeros_like(s_b)

    def step(j, carry):
        rf, rb = _scan_rows(j, nchunk)
        blk = p_ref[pl.ds(rf, c_len), :]
        q = blk[:, 0:w].astype(F32)
        k = blk[:, w:2 * w]
        v = blk[:, 2 * w:3 * w]
        blk = p_ref[pl.ds(rb, c_len), :]
        q2 = blk[:, 0:w].astype(F32)
        k2 = blk[:, w:2 * w].astype(F32)
        v2 = blk[:, 2 * w:3 * w]
        sf_t = s_f[...]
        sb_t = s_b[...]
        a = _dot_nt((q * scale).astype(BF16), _head_expand(k, hmasks))
        of = _dot_nt((q * qf_dec).astype(BF16), sf_t.astype(BF16))
        uf = _dot_tn(v, (k.astype(F32) * kf_dec).astype(BF16))
        ob = _dot_nt((q2 * qb_dec).astype(BF16), sb_t.astype(BF16))
        ub = _dot_tn(v2, (k2 * kb_dec).astype(BF16))
        s_f[...] = sf_t * tot_f + jnp.where(bmask, uf, 0.0)
        s_b[...] = sb_t * tot_b + jnp.where(bmask, ub, 0.0)
        acc[pl.ds(rf, c_len), :] += of + _dot((a * d2_scr[...]).astype(BF16), _head_expand(v, hmasks))
        acc[pl.ds(rb, c_len), :] += ob
        return carry

    lax.fori_loop(0, nchunk, step, 0)

    o = acc[out_off:, :]
    gate = p_ref[out_off:, 3 * w:4 * w].astype(F32)
    o_ref[...] = (_head_rms_norm(o, gain_ref[...], ones_bf16) * _silu(gate)).astype(o_ref.dtype)


def _retention(pc, decay_logit, gain, batch, out_rows):
    n, width = pc.shape
    rows = n // batch
    w = GROUP_W
    dl = jnp.zeros((8, LANES), F32).at[:decay_logit.shape[0], :decay_logit.shape[1]].set(decay_logit)
    return pl.pallas_call(
        functools.partial(_ret_kernel, out_off=rows - out_rows),
        grid=(batch,),
        in_specs=[pl.BlockSpec((rows, width), lambda b: (b, 0)),
                  pl.BlockSpec((8, LANES), lambda b: (0, 0)),
                  pl.BlockSpec((SUBLANES, w), lambda b: (0, 0))],
        out_specs=pl.BlockSpec((out_rows, w), lambda b: (b, 0)),
        out_shape=jax.ShapeDtypeStruct((batch * out_rows, w), BF16),
        scratch_shapes=[pltpu.VMEM((rows, w), F32), pltpu.VMEM((w, w), F32), pltpu.VMEM((w, w), F32),
                        pltpu.VMEM((ROW_BLK, N_HEADS * ROW_BLK), F32)],
        compiler_params=_params("arbitrary"),
        name="retention_mixer",
    )(pc, dl, _rows8(gain))


def _rope(x, cos, sin_signed):
    n = x.shape[-1]
    lane = lax.broadcasted_iota(jnp.int32, x.shape, 1)
    swapped = jnp.where(lane % 2 == 0, pltpu.roll(x, n - 1, 1), pltpu.roll(x, 1, 1))
    return x * cos + swapped * sin_signed


def _attn_kernel(q_ref, k_ref, v_ref, cos_ref, sin_ref, qn_ref, kn_ref, o_ref, k_buf, vt_buf, s_scr, *, qb_off):
    j = pl.program_id(1)
    nkb = k_ref.shape[0] // KEY_BLK

    @pl.when(j == 0)
    def _():
        ones_kv = _head_ones(KV_W).astype(BF16)
        kn = _head_rms_norm(k_ref[...].astype(F32), kn_ref[...], ones_kv)
        key_rows = k_ref.shape[0]
        k_buf[...] = _rope(kn, cos_ref[0:key_rows, 0:KV_W], sin_ref[0:key_rows, 0:KV_W]).astype(BF16)
        vt = v_ref[...].astype(F32).T.astype(BF16)
        for kb in range(nkb):
            vt_buf[kb] = vt[:, kb * KEY_BLK:(kb + 1) * KEY_BLK]

    qb = j + qb_off
    r0 = pl.multiple_of(qb * ROW_BLK, ROW_BLK)
    ones_q = _head_ones(GROUP_W).astype(BF16)
    qn = _head_rms_norm(q_ref[...].astype(F32), qn_ref[...], ones_q)
    qr = _rope(qn, cos_ref[pl.ds(r0, ROW_BLK), :], sin_ref[pl.ds(r0, ROW_BLK), :])
    qr = (qr * (HEAD_DIM ** -0.5 * LOG2_E)).astype(BF16)
    group = GROUP_W // KV_W
    ones_rows = jnp.ones((BF16_SUBLANES, KEY_BLK), BF16)
    n_kv = KV_W // HEAD_DIM
    ksl = lambda kv: slice(kv * HEAD_DIM, (kv + 1) * HEAD_DIM)
    q2 = [jnp.concatenate([qr[:, (kv * group + g) * HEAD_DIM:(kv * group + g + 1) * HEAD_DIM]
                           for g in range(group)], axis=0) for kv in range(n_kv)]

    def scores(kv, kb, m):
        r = pl.multiple_of(kb * KEY_BLK, KEY_BLK)
        s = _dot_nt(k_buf[pl.ds(r, KEY_BLK), ksl(kv)], q2[kv])
        s_scr[kv, kb] = s
        return jnp.maximum(m, jnp.max(s, axis=0, keepdims=True))

    def values(kv, kb, m, acc):
        p = jnp.exp2(s_scr[kv, kb] - m).astype(BF16)
        vt = jnp.concatenate([vt_buf[kb, ksl(kv), :], ones_rows], axis=0)
        return acc + _dot(vt, p)

    unroll = min(3, nkb)
    m0 = jnp.full((1, group * ROW_BLK), -jnp.inf, F32)
    zero = jnp.zeros((HEAD_DIM + BF16_SUBLANES, group * ROW_BLK), F32)
    m_cur = lax.fori_loop(0, nkb, lambda kb, m: scores(0, kb, m), m0, unroll=unroll)
    outs = []
    for kv in range(n_kv):
        if kv + 1 < n_kv:
            m_next, acc = lax.fori_loop(
                0, nkb, lambda kb, c: (scores(kv + 1, kb, c[0]), values(kv, kb, m_cur, c[1])), (m0, zero),
                unroll=unroll)
        else:
            m_next, acc = None, lax.fori_loop(0, nkb, lambda kb, a: values(kv, kb, m_cur, a), zero, unroll=unroll)
        outs.append(acc[0:HEAD_DIM] * _recip(acc[HEAD_DIM:HEAD_DIM + 1]))
        m_cur = m_next
    o_t = jnp.concatenate(outs, axis=0).T
    for kv in range(n_kv):
        for g in range(group):
            h = kv * group + g
            o_ref[:, h * HEAD_DIM:(h + 1) * HEAD_DIM] = (
                o_t[g * ROW_BLK:(g + 1) * ROW_BLK, kv * HEAD_DIM:(kv + 1) * HEAD_DIM].astype(o_ref.dtype))


def _attention(pd, cos, sin_signed, q_norm, k_norm, batch, q_first, n_q, key_blks):
    n, width = pd.shape
    rows = n // batch
    bpb = rows // ROW_BLK
    assert bpb % key_blks == 0
    key_rows = key_blks * ROW_BLK
    kcol = GROUP_W // KV_W
    tile = lambda g, reps: _rows8(jnp.tile(g, reps))
    kv_spec = lambda col: pl.BlockSpec((key_rows, KV_W), lambda b, j: (b * (bpb // key_blks), col))
    return pl.pallas_call(
        functools.partial(_attn_kernel, qb_off=q_first),
        grid=(batch, n_q),
        in_specs=[pl.BlockSpec((ROW_BLK, GROUP_W), lambda b, j: (b * bpb + j + q_first, 0)),
                  kv_spec(kcol), kv_spec(kcol + 1),
                  pl.BlockSpec((rows, GROUP_W), lambda b, j: (0, 0)),
                  pl.BlockSpec((rows, GROUP_W), lambda b, j: (0, 0)),
                  pl.BlockSpec((SUBLANES, GROUP_W), lambda b, j: (0, 0)),
                  pl.BlockSpec((SUBLANES, KV_W), lambda b, j: (0, 0))],
        out_specs=pl.BlockSpec((ROW_BLK, GROUP_W), lambda b, j: (b * n_q + j, 0)),
        out_shape=jax.ShapeDtypeStruct((batch * n_q * ROW_BLK, GROUP_W), BF16),
        scratch_shapes=[pltpu.VMEM((key_rows, KV_W), BF16), pltpu.VMEM((key_rows // KEY_BLK, KV_W, KEY_BLK), BF16),
                        pltpu.VMEM((KV_W // HEAD_DIM, key_rows // KEY_BLK, KEY_BLK, (GROUP_W // KV_W) * ROW_BLK),
                                   F32)],
        compiler_params=_params("parallel", "arbitrary"),
        name="attention_mixer",
    )(pd, pd, pd, cos, sin_signed, tile(q_norm, N_HEADS), tile(k_norm, KV_W // HEAD_DIM))


def _attention_mixer(pd, cos, sin_signed, q_norm, k_norm, batch, ctx_len, with_ctx):
    rows = pd.shape[0] // batch
    bpb = rows // ROW_BLK
    cb = ctx_len // ROW_BLK
    gx = _attention(pd, cos, sin_signed, q_norm, k_norm, batch, cb, bpb - cb, bpb)
    if not with_ctx:
        return gx
    gc = _attention(pd, cos, sin_signed, q_norm, k_norm, batch, 0, cb, cb)
    return jnp.concatenate([gc.reshape(batch, ctx_len, GROUP_W), gx.reshape(batch, rows - ctx_len, GROUP_W)],
                           axis=1).reshape(batch * rows, GROUP_W)


def _rope_tables(ctx_len, seq_len):
    t = jnp.arange(seq_len)
    row = (t // GRID_W).astype(F32)
    col = (t % GRID_W).astype(F32)
    n_freq = HEAD_DIM // 4
    inv_freq = ROPE_THETA ** (-jnp.arange(n_freq, dtype=F32) / n_freq)
    ang = jnp.concatenate([row[:, None] * inv_freq, col[:, None] * inv_freq], axis=-1)
    cos = jnp.repeat(jnp.cos(ang), 2, axis=-1)
    sin = jnp.stack([-jnp.sin(ang), jnp.sin(ang)], axis=-1).reshape(seq_len, HEAD_DIM)
    cos = jnp.concatenate([jnp.ones((ctx_len, HEAD_DIM), F32), cos], axis=0)
    sin = jnp.concatenate([jnp.zeros((ctx_len, HEAD_DIM), F32), sin], axis=0)
    return jnp.tile(cos, (1, N_HEADS)), jnp.tile(sin, (1, N_HEADS))


def _route(lt, n_experts):
    n_groups = n_experts // EXPERTS_PER_GROUP
    first = lambda x, hit, n: jnp.min(jnp.where(hit, x, float(n)), axis=0, keepdims=True)

    gl = lt[n_experts:n_experts + n_groups]
    gexp = jnp.exp(gl - jnp.max(gl, axis=0, keepdims=True))
    gprob = gexp * _recip(jnp.sum(gexp, axis=0, keepdims=True))
    group_p = jnp.max(gprob, axis=0, keepdims=True)
    grow = lax.broadcasted_iota(jnp.int32, gl.shape, 0).astype(F32)
    gidx = first(grow, gprob == group_p, n_groups)

    el = sum(jnp.where(gidx == float(g), lt[g * EXPERTS_PER_GROUP:(g + 1) * EXPERTS_PER_GROUP], 0.0)
             for g in range(n_groups))
    eexp = jnp.exp(el - jnp.max(el, axis=0, keepdims=True))
    eprob = eexp * _recip(jnp.sum(eexp, axis=0, keepdims=True))
    erow = lax.broadcasted_iota(jnp.int32, el.shape, 0).astype(F32)
    p1 = jnp.max(eprob, axis=0, keepdims=True)
    i1 = first(erow, eprob == p1, EXPERTS_PER_GROUP)
    rest = erow != i1
    p2 = jnp.max(jnp.where(rest, eprob, -1.0), axis=0, keepdims=True)
    i2 = first(erow, rest & (eprob == p2), EXPERTS_PER_GROUP)
    scale = group_p * _recip(p1 + p2)
    lo = jnp.minimum(i1, i2)
    hi = jnp.maximum(i1, i2)
    w_lo = jnp.where(i1 < i2, p1, p2) * scale
    w_hi = jnp.where(i1 < i2, p2, p1) * scale
    pair = lo * (2.0 * EXPERTS_PER_GROUP - 1.0 - lo) * 0.5 + (hi - lo - 1.0)
    bucket = gidx * float(PAIRS_PER_GROUP) + pair
    return jnp.concatenate([bucket, w_lo, w_hi, jnp.zeros((LANES - 3, lt.shape[1]), F32)], axis=0)


def _outproj_kernel(*refs, n_experts, chunks, tiles_per_batch, with_ctx):
    x_refs = refs[:chunks]
    (a_ref, b_ref, c_ref, d_ref, modc_ref, modb_ref, gain_ref, w_ref, wr_ref, br_ref,
     x1_ref, h2_ref, rt_ref) = refs[chunks:]
    d = x1_ref.shape[-1]
    rows = [slice(c * ROW_BLK, (c + 1) * ROW_BLK) for c in range(chunks)]
    mods = [modb_ref[0]] * chunks
    if with_ctx:
        mods[0] = jnp.where(pl.program_id(0) % tiles_per_batch == 0, modc_ref[0], mods[0])
    projs = [_dot(jnp.concatenate([a_ref[rs, :], b_ref[rs, :], c_ref[rs, :], d_ref[rs, :]], axis=1), w_ref[...])
             for rs in rows]
    h2s = []
    for c, rs in enumerate(rows):
        x1 = x_refs[c][...] + _tile_rows(mods[c][:, 2 * d:3 * d], ROW_BLK) * projs[c]
        x1_ref[rs, :] = x1
        h2s.append(_norm_modulate(x1, gain_ref[...], mods[c], 3))
        _store_token_tiles(h2_ref, h2s[c], c * ROW_BLK)
    lts = [lax.dot_general(wr_ref[...], h2, (((1,), (1,)), ((), ())), precision=HIGHEST,
                           preferred_element_type=F32) + br_ref[...] for h2 in h2s]
    for rs, lt in zip(rows, lts):
        rt_ref[rs, :] = _route(lt, n_experts).T


def _outproj(tokens, mixers, mod, gain, w_bf16, w_router_t, b_router_t, n_experts, batch, ctx_row):
    d = tokens.shape[1]
    n_out = mixers[0].shape[0]
    out_bpb = n_out // ROW_BLK // batch
    tok_bpb = tokens.shape[0] // ROW_BLK // batch
    off = tok_bpb - out_bpb
    chunks = max(c for c in (4, 3, 2, 1) if out_bpb % c == 0)
    tpb = out_bpb // chunks
    tr = chunks * ROW_BLK

    def x_spec(c):
        return pl.BlockSpec((ROW_BLK, d), lambda i: ((i // tpb) * tok_bpb + (i % tpb) * chunks + c + off, 0))

    row_spec = lambda w: pl.BlockSpec((tr, w), lambda i: (i, 0))
    full = lambda a: pl.BlockSpec(a.shape, lambda i: (0,) * a.ndim)
    mod_spec = lambda idx: pl.BlockSpec((1, SUBLANES, mod.shape[-1]), idx)
    return pl.pallas_call(
        functools.partial(_outproj_kernel, n_experts=n_experts, chunks=chunks, tiles_per_batch=tpb,
                          with_ctx=off == 0),
        grid=(n_out // tr,),
        in_specs=[x_spec(c) for c in range(chunks)] + [
            row_spec(GROUP_W), row_spec(GROUP_W), row_spec(GROUP_W), row_spec(GROUP_W),
            mod_spec(lambda i: (ctx_row, 0, 0)), mod_spec(lambda i: (i // tpb, 0, 0)),
            pl.BlockSpec((SUBLANES, d), lambda i: (0, 0)),
            full(w_bf16), full(w_router_t), full(b_router_t)],
        out_specs=[row_spec(d),
                   pl.BlockSpec((tr * TOKEN_SUB, LANES), lambda i: (i, 0)),
                   row_spec(LANES)],
        out_shape=[jax.ShapeDtypeStruct((n_out, d), F32),
                   jax.ShapeDtypeStruct((n_out * TOKEN_SUB, LANES), F32),
                   jax.ShapeDtypeStruct((n_out, LANES), F32)],
        compiler_params=_params("parallel"),
        name="out_proj_router",
    )(*([tokens] * chunks), *mixers, mod, mod, _rows8(gain), w_bf16, w_router_t, b_router_t)


def _store_token_tiles(ref, x, first_token=0):
    rows = x.shape[0]
    for j in range(TOKEN_SUB):
        ref[pl.ds(first_token * TOKEN_SUB + j, rows, stride=TOKEN_SUB), :] = x[:, j * LANES:(j + 1) * LANES]


def _load_token_tiles(ref, first_token=0, rows=None):
    rows = ref.shape[0] // TOKEN_SUB if rows is None else rows
    return jnp.concatenate([ref[pl.ds(first_token * TOKEN_SUB + j, rows, stride=TOKEN_SUB), :]
                            for j in range(TOKEN_SUB)], axis=-1)


def _token_rows(t):
    return pl.ds(pl.multiple_of(t * TOKEN_SUB, TOKEN_SUB), TOKEN_SUB)


def _token_dmas(n, make_copy):
    def start(r, carry):
        make_copy(r).start()
        return carry

    def wait(r, carry):
        make_copy(r).wait()
        return carry

    lax.fori_loop(0, n, start, 0, unroll=8)
    lax.fori_loop(0, n, wait, 0, unroll=8)


def _pack_bf16_pair(a, b):
    au = lax.bitcast_convert_type(a.astype(BF16).astype(F32), jnp.uint32)
    bu = lax.bitcast_convert_type(b.astype(BF16).astype(F32), jnp.uint32)
    return au | (bu >> 16)


def _unpack_bf16_pair(u):
    a = lax.bitcast_convert_type(u & jnp.uint32(0xFFFF0000), F32)
    b = lax.bitcast_convert_type(u << 16, F32)
    return a, b


def _moe_plan(route, n_buckets):
    n = route.shape[0]
    nt = n // ROW_BLK
    n_tiles = nt + n_buckets
    bucket = route[:, 0].astype(jnp.int32)
    onehot = (bucket[:, None] == jnp.arange(n_buckets, dtype=jnp.int32)[None, :]).astype(F32)
    onehot = onehot.reshape(nt, ROW_BLK, n_buckets)
    tile_counts = jnp.sum(onehot, axis=1)
    before = jnp.cumsum(tile_counts, axis=0) - tile_counts
    earlier = (jnp.arange(ROW_BLK)[:, None] > jnp.arange(ROW_BLK)[None, :]).astype(F32)
    rank = jnp.einsum('ij,tjb->tib', earlier, onehot) + before[:, None, :]
    counts = jnp.sum(tile_counts, axis=0)
    tiles = jnp.ceil(counts / ROW_BLK)
    tend = jnp.cumsum(tiles)
    tstart = tend - tiles
    pos = jnp.sum(onehot * (rank + tstart * ROW_BLK), axis=-1).astype(jnp.int32).reshape(n)
    tile_id = jnp.arange(n_tiles, dtype=F32)
    tile_valid = (tile_id < tend[-1]).astype(jnp.int32)
    tile_bucket = jnp.sum((tile_id[:, None] >= tend[None, :]).astype(jnp.int32), axis=1)
    last_bucket = jnp.max(jnp.where(tiles > 0, jnp.arange(n_buckets, dtype=jnp.int32), 0))
    tile_bucket = jnp.minimum(tile_bucket, last_bucket)
    group = tile_bucket // PAIRS_PER_GROUP
    pair = tile_bucket % PAIRS_PER_GROUP
    lo = (pair >= 3).astype(jnp.int32) + (pair >= 5).astype(jnp.int32)
    hi = jnp.where(pair < 3, pair + 1, jnp.where(pair < 5, pair - 1, 3))
    return group * EXPERTS_PER_GROUP + lo, group * EXPERTS_PER_GROUP + hi, tile_valid, pos


def _moe_scatter_kernel(pos_ref, h_ref, hs_in, hs_out, sem):
    del hs_in
    _token_dmas(h_ref.shape[0] // TOKEN_SUB,
                lambda r: pltpu.make_async_copy(h_ref.at[_token_rows(r)], hs_out.at[_token_rows(pos_ref[0, 0, r])],
                                                sem))


def _moe_scatter(pos, h2, n_sorted, tm):
    n = h2.shape[0] // TOKEN_SUB
    return pl.pallas_call(
        _moe_scatter_kernel,
        grid=(n // tm,),
        in_specs=[pl.BlockSpec((1, 1, tm), lambda i: (i, 0, 0), memory_space=pltpu.SMEM),
                  pl.BlockSpec((tm * TOKEN_SUB, LANES), lambda i: (i, 0)),
                  pl.BlockSpec(memory_space=pl.ANY)],
        out_specs=pl.BlockSpec(memory_space=pl.ANY),
        out_shape=jax.ShapeDtypeStruct((n_sorted * TOKEN_SUB, LANES), F32),
        scratch_shapes=[pltpu.SemaphoreType.DMA(())],
        input_output_aliases={2: 0},
        compiler_params=_params("arbitrary"),
        name="moe_scatter",
    )(pos.reshape(n // tm, 1, tm), h2, jnp.zeros((n_sorted * TOKEN_SUB, LANES), F32))


def _moe_expert_kernel(lo_ref, hi_ref, valid_ref, h_ref, wg_lo, wu_lo, wd_lo, wg_hi, wu_hi, wd_hi, y_ref):
    i = pl.program_id(0)

    @pl.when(valid_ref[i] == 0)
    def _():
        y_ref[...] = jnp.zeros_like(y_ref)

    @pl.when(valid_ref[i] == 1)
    def _():
        h = _load_token_tiles(h_ref).astype(BF16)

        g_lo, u_lo = _dot(h, wg_lo[0]), _dot(h, wu_lo[0])
        g_hi, u_hi = _dot(h, wg_hi[0]), _dot(h, wu_hi[0])
        y_lo = _dot((_silu(g_lo) * u_lo).astype(BF16), wd_lo[0])
        y_hi = _dot((_silu(g_hi) * u_hi).astype(BF16), wd_hi[0])
        _store_token_tiles(y_ref, _pack_bf16_pair(y_lo, y_hi))


def _moe_experts(tile_lo, tile_hi, tile_valid, h_sorted, wg, wu, wd):
    n_tiles = tile_lo.shape[0]
    by_lo = lambda a: pl.BlockSpec((1,) + a.shape[1:], lambda i, lo, hi, v: (lo[i], 0, 0))
    by_hi = lambda a: pl.BlockSpec((1,) + a.shape[1:], lambda i, lo, hi, v: (hi[i], 0, 0))
    tile_spec = pl.BlockSpec((ROW_BLK * TOKEN_SUB, LANES), lambda i, lo, hi, v: (i, 0))
    grid_spec = pltpu.PrefetchScalarGridSpec(
        num_scalar_prefetch=3,
        grid=(n_tiles,),
        in_specs=[tile_spec, by_lo(wg), by_lo(wu), by_lo(wd), by_hi(wg), by_hi(wu), by_hi(wd)],
        out_specs=tile_spec,
    )
    return pl.pallas_call(
        _moe_expert_kernel,
        grid_spec=grid_spec,
        out_shape=jax.ShapeDtypeStruct((n_tiles * ROW_BLK * TOKEN_SUB, LANES), jnp.uint32),
        compiler_params=_params("parallel"),
        name="moe_experts",
    )(tile_lo, tile_hi, tile_valid, h_sorted, wg, wu, wd, wg, wu, wd)


def _moe_combine_kernel(pos_ref, pos_next_ref, y_hbm, rt_ref, x1_ref, modc_ref, modb_ref, o_ref, buf, sem,
                        *, ctx_len, tiles_per_batch):
    d = x1_ref.shape[-1]
    tm = x1_ref.shape[0]
    i = pl.program_id(0)
    half = i % 2

    def copies(idx_ref, h, op):
        def body(r, carry):
            op(pltpu.make_async_copy(y_hbm.at[_token_rows(idx_ref[0, 0, r])], buf.at[_token_rows(h * tm + r)],
                                     sem.at[h]))
            return carry
        lax.fori_loop(0, tm, body, 0, unroll=8)

    @pl.when(i == 0)
    def _():
        copies(pos_ref, 0, lambda cp: cp.start())

    @pl.when(i + 1 < pl.num_programs(0))
    def _():
        copies(pos_next_ref, 1 - half, lambda cp: cp.start())

    copies(pos_ref, half, lambda cp: cp.wait())
    y_lo, y_hi = _unpack_bf16_pair(_load_token_tiles(buf, half * tm, tm))
    rt = rt_ref[...]
    y = rt[:, 1:2] * y_lo + rt[:, 2:3] * y_hi
    gate = _tile_rows(modb_ref[0][:, 5 * d:6 * d], y.shape[0])
    if ctx_len:
        first = pl.program_id(0) % tiles_per_batch == 0
        row = lax.broadcasted_iota(jnp.int32, y.shape, 0)
        gate = jnp.where(first & (row < ctx_len), _tile_rows(modc_ref[0][:, 5 * d:6 * d], y.shape[0]), gate)
    o_ref[...] = x1_ref[...] + gate * y


def _moe_combine(pos, y_sorted, route, x1, mod, batch, ctx_len, ctx_row, with_ctx, tm):
    n, d = x1.shape
    tiles_per_batch = n // batch // tm
    n_tiles = n // tm
    row_spec = lambda w: pl.BlockSpec((tm, w), lambda i: (i, 0))
    pos = pos.reshape(n_tiles, 1, tm)
    return pl.pallas_call(
        functools.partial(_moe_combine_kernel, ctx_len=ctx_len if with_ctx else 0, tiles_per_batch=tiles_per_batch),
        grid=(n_tiles,),
        in_specs=[pl.BlockSpec((1, 1, tm), lambda i: (i, 0, 0), memory_space=pltpu.SMEM),
                  pl.BlockSpec((1, 1, tm), lambda i: (jnp.minimum(i + 1, n_tiles - 1), 0, 0),
                               memory_space=pltpu.SMEM),
                  pl.BlockSpec(memory_space=pl.ANY),
                  row_spec(LANES), row_spec(d),
                  pl.BlockSpec((1, SUBLANES, mod.shape[-1]), lambda i: (ctx_row, 0, 0)),
                  pl.BlockSpec((1, SUBLANES, mod.shape[-1]), lambda i: (i // tiles_per_batch, 0, 0))],
        out_specs=row_spec(d),
        out_shape=jax.ShapeDtypeStruct((n, d), F32),
        scratch_shapes=[pltpu.VMEM((2 * tm * TOKEN_SUB, LANES), jnp.uint32), pltpu.SemaphoreType.DMA((2,))],
        compiler_params=_params("arbitrary"),
        name="moe_combine",
    )(pos, pos, y_sorted, route, x1, mod, mod)


def kernel(x, c, ctx, c_ctx, ada_w, ada_b, norm_mix, norm_ffn, w_in, hgrn_lb_logits, hgrn_norm, conv_w,
           ret_decay_logit, ret_norm, q_norm, k_norm, w_out, router_group_w, router_group_b, router_expert_w,
           router_expert_b, expert_w_gate, expert_w_up, expert_w_down):
    batch, seq_len, d = x.shape
    ctx_len = ctx.shape[1]
    depth = ada_w.shape[0]
    n_experts = expert_w_gate.shape[1]
    n_groups = router_group_w.shape[-1]
    assert ctx_len == ROW_BLK and seq_len % ROW_BLK == 0 and batch + 1 <= MOD_ROWS
    assert n_experts + n_groups <= ROUTER_ROWS and n_experts == n_groups * EXPERTS_PER_GROUP
    assert d == TOKEN_SUB * LANES
    rows = ctx_len + seq_len
    bpb = rows // ROW_BLK
    ctx_row = batch

    cv = jnp.concatenate([c, c_ctx[None], jnp.zeros((MOD_ROWS - batch - 1, d), F32)], axis=0)
    mod_all = _modulation(cv, ada_w, ada_b)
    cos, sin_signed = _rope_tables(ctx_len, seq_len)
    tokens = jnp.concatenate([ctx, x], axis=1).reshape(batch * rows, d)

    for layer in range(depth):
        with_ctx = layer < depth - 1
        out_rows = rows if with_ctx else seq_len
        mod = jnp.broadcast_to(mod_all[layer][:, None, :], (MOD_ROWS, SUBLANES, 6 * d))
        pa, pb, pc, pd = _inproj(tokens, mod, norm_mix[layer], w_in[layer].astype(BF16), bpb, ctx_row)
        mixers = (
            _hgrn(pa, hgrn_lb_logits, hgrn_norm[layer], layer, batch, out_rows),
            _conv(pb, conv_w[layer], batch, ctx_len, out_rows),
            _retention(pc, ret_decay_logit[layer], ret_norm[layer], batch, out_rows),
            _attention_mixer(pd, cos, sin_signed, q_norm[layer], k_norm[layer], batch, ctx_len, with_ctx),
        )
        pad = ROUTER_ROWS - n_experts - n_groups
        w_router_t = jnp.concatenate([router_expert_w[layer].T, router_group_w[layer].T, jnp.zeros((pad, d), F32)])
        b_router = jnp.concatenate([router_expert_b[layer], router_group_b[layer], jnp.zeros((pad,), F32)])
        b_router_t = jnp.broadcast_to(b_router[:, None], (ROUTER_ROWS, ROW_BLK))
        x1, h2, rt = _outproj(tokens, mixers, mod, norm_ffn[layer], w_out[layer].astype(BF16), w_router_t,
                              b_router_t, n_experts, batch, ctx_row)
        tile_lo, tile_hi, tile_valid, pos = _moe_plan(rt, n_groups * PAIRS_PER_GROUP)
        tm = max(t for t in range(ROW_BLK, 4 * ROW_BLK + 1, ROW_BLK) if out_rows % t == 0)
        h_sorted = _moe_scatter(pos, h2, tile_lo.shape[0] * ROW_BLK, tm)
        y_sorted = _moe_experts(tile_lo, tile_hi, tile_valid, h_sorted, expert_w_gate[layer].astype(BF16),
                                expert_w_up[layer].astype(BF16), expert_w_down[layer].astype(BF16))
        tokens = _moe_combine(pos, y_sorted, rt, x1, mod, batch, ctx_len, ctx_row, with_ctx, tm)
    return tokens.reshape(batch, seq_len, d)
```

```python
import functools

import jax
import jax.numpy as jnp
from jax import lax
from jax.experimental import pallas as pl
from jax.experimental.pallas import tpu as pltpu

F32 = jnp.float32
BF16 = jnp.bfloat16
HIGHEST = lax.Precision.HIGHEST

HEAD_DIM = 64
GROUP_W = 256
N_HEADS = GROUP_W // HEAD_DIM
KV_W = 128
GRID_W = 64
ROPE_THETA = 10000.0
EXPERTS_PER_GROUP = 4
NORM_EPS = 1e-6
ROW_BLK = 256
SUB_BLK = 8
LEVEL_ROWS = 128
KEY_BLK = 256
LANES = 128
SUBLANES = 8
BF16_SUBLANES = 16
LOG2_E = 1.4426950408889634
MOD_ROWS = 24
ROUTER_ROWS = 32
ROUTE_ROWS = 8
PAIRS_PER_GROUP = EXPERTS_PER_GROUP * (EXPERTS_PER_GROUP - 1) // 2
TOKEN_SUB = 8
VMEM_LIMIT_BYTES = 56 * 1024 * 1024
PA_W, PB_W, PC_W, PD_W = 5 * GROUP_W, 3 * GROUP_W, 4 * GROUP_W, GROUP_W + 2 * KV_W


def _params(*semantics):
    return pltpu.CompilerParams(dimension_semantics=semantics, vmem_limit_bytes=VMEM_LIMIT_BYTES)


def _dot(a, b):
    return jnp.dot(a, b, preferred_element_type=F32)


def _dot_nt(a, b):
    return lax.dot_general(a, b, (((1,), (1,)), ((), ())), preferred_element_type=F32)


def _dot_tn(a, b):
    return lax.dot_general(a, b, (((0,), (0,)), ((), ())), preferred_element_type=F32)


def _recip(x):
    return pl.reciprocal(x, approx=True)


def _sigmoid(x):
    return _recip(1.0 + jnp.exp(-x))


def _silu(x):
    return x * _sigmoid(x)


def _log_sigmoid(x):
    return jnp.minimum(x, 0.0) - jnp.log(1.0 + jnp.exp(-jnp.abs(x)))


def _split_bf16(x, terms):
    parts = []
    rem = x
    for i in range(terms):
        p = rem.astype(BF16)
        parts.append(p)
        if i + 1 < terms:
            rem = rem - p.astype(F32)
    return parts


def _head_ones(width):
    r = lax.broadcasted_iota(jnp.int32, (width, width), 0) // HEAD_DIM
    c = lax.broadcasted_iota(jnp.int32, (width, width), 1) // HEAD_DIM
    return r == c


def _head_masks(width):
    lane_head = lax.broadcasted_iota(jnp.int32, (1, width), 1) // HEAD_DIM
    return [lane_head == h for h in range(width // HEAD_DIM)]


def _head_expand(x, hmasks):
    zero = jnp.zeros_like(x)
    return jnp.concatenate([jnp.where(hm, x, zero) for hm in hmasks], axis=0).astype(BF16)


def _head_mean_sq(x, ones_bf16):
    hi, lo = _split_bf16(x * x, 2)
    return (_dot(hi, ones_bf16) + _dot(lo, ones_bf16)) * (1.0 / HEAD_DIM)


def _rows8(v):
    return jnp.broadcast_to(v.reshape(1, -1), (SUBLANES, v.size))


def _tile_rows(m8, rows):
    n = m8.shape[-1]
    return jnp.broadcast_to(m8[None], (rows // SUBLANES, SUBLANES, n)).reshape(rows, n)


def _head_rms_norm(x, gain8, ones_bf16):
    return x * lax.rsqrt(_head_mean_sq(x, ones_bf16) + NORM_EPS) * _tile_rows(gain8, x.shape[0])


def _norm_modulate(x, gain8, mod8, idx):
    rows, d = x.shape
    ms = jnp.mean(x * x, axis=-1, keepdims=True)
    y = x * lax.rsqrt(ms + NORM_EPS) * _tile_rows(gain8, rows)
    shift = _tile_rows(mod8[:, idx * d:(idx + 1) * d], rows)
    scale = _tile_rows(mod8[:, (idx + 1) * d:(idx + 2) * d], rows)
    return y * (1.0 + scale) + shift


def _scan_rows(j, nchunk):
    rf = pl.multiple_of(j * ROW_BLK, ROW_BLK)
    rb = pl.multiple_of(jnp.where(j == 0, 0, nchunk - j) * ROW_BLK, ROW_BLK)
    return rf, rb


def _mod_kernel(cv_ref, w_ref, b_ref, o_ref):
    s = _silu(cv_ref[...])
    o_ref[0] = jnp.dot(s, w_ref[0], precision=HIGHEST, preferred_element_type=F32) + b_ref[0]


def _modulation(cv, ada_w, ada_b):
    depth, d, n = ada_w.shape
    tn = n // 4
    return pl.pallas_call(
        _mod_kernel,
        grid=(depth, n // tn),
        in_specs=[pl.BlockSpec((MOD_ROWS, d), lambda l, j: (0, 0)),
                  pl.BlockSpec((1, d, tn), lambda l, j: (l, 0, j)),
                  pl.BlockSpec((1, 1, tn), lambda l, j: (l, 0, j))],
        out_specs=pl.BlockSpec((1, MOD_ROWS, tn), lambda l, j: (l, 0, j)),
        out_shape=jax.ShapeDtypeStruct((depth, MOD_ROWS, n), F32),
        compiler_params=_params("parallel", "parallel"),
        name="adaln_mod",
    )(cv, ada_w, ada_b.reshape(depth, 1, n))


def _inproj_kernel(*refs, blocks_per_batch):
    srcs, (mod_ref, gain_ref, w_ref, pa_ref, pb_ref, pc_ref, pd_ref) = refs[:-7], refs[-7:]
    x = srcs[-1][...]
    if len(srcs) == 2:
        x = jnp.where(pl.program_id(0) % blocks_per_batch == 0, srcs[0][...], x)
    h = _norm_modulate(x, gain_ref[...], mod_ref[0], 0)
    p = _dot(h.astype(BF16), w_ref[...])
    pa_ref[...] = p[:, 0:PA_W]
    pb_ref[...] = p[:, PA_W:PA_W + PB_W].astype(pb_ref.dtype)
    pc_ref[...] = p[:, PA_W + PB_W:PA_W + PB_W + PC_W].astype(pc_ref.dtype)
    pd_ref[...] = p[:, PA_W + PB_W + PC_W:].astype(pd_ref.dtype)


def _resid_specs(srcs, d, bpb, block_of):
    if len(srcs) == 1:
        return [pl.BlockSpec((ROW_BLK, d), lambda i: (block_of(i), 0))]
    return [pl.BlockSpec((ROW_BLK, d), lambda i: (block_of(i) // bpb, 0)),
            pl.BlockSpec((ROW_BLK, d), lambda i: ((block_of(i) // bpb) * (bpb - 1)
                                                  + jnp.maximum(block_of(i) % bpb - 1, 0), 0))]


def _inproj(srcs, mod, gain, w_bf16, blocks_per_batch, ctx_row):
    n = sum(s.shape[0] for s in srcs)
    d = srcs[0].shape[1]
    nblk = n // ROW_BLK

    def mod_idx(i):
        return (jnp.where(i % blocks_per_batch == 0, ctx_row, i // blocks_per_batch), 0, 0)

    outs = ((PA_W, F32), (PB_W, BF16), (PC_W, BF16), (PD_W, BF16))
    return pl.pallas_call(
        functools.partial(_inproj_kernel, blocks_per_batch=blocks_per_batch),
        grid=(nblk,),
        in_specs=_resid_specs(srcs, d, blocks_per_batch, lambda i: i) + [
                  pl.BlockSpec((1, SUBLANES, mod.shape[-1]), mod_idx),
                  pl.BlockSpec((SUBLANES, d), lambda i: (0, 0)),
                  pl.BlockSpec(w_bf16.shape, lambda i: (0, 0))],
        out_specs=[pl.BlockSpec((ROW_BLK, w), lambda i: (i, 0)) for w, _ in outs],
        out_shape=[jax.ShapeDtypeStruct((n, w), dt) for w, dt in outs],
        compiler_params=_params("parallel"),
        name="in_proj",
    )(*srcs, mod, _rows8(gain), w_bf16)


def _hgrn_prepare(blk, lb_row, z_col, anti, s_ref, consts):
    tril, triu, bmask, hmasks, level_masks = consts
    w = GROUP_W
    c_len = blk.shape[0]
    q = blk[:, 0:w]
    v = blk[:, w:2 * w]
    z = blk[:, z_col * w:(z_col + 1) * w]
    f = lb_row + (1.0 - lb_row) * _sigmoid(z)
    g = jnp.log(f) * LOG2_E
    k = 1.0 - f
    tri = triu if anti else tril
    c = sum(_dot(tri, part) for part in _split_bf16(g, 3))
    tot = c[0:1] if anti else c[c_len - 1:c_len]

    s_t = s_ref[...]
    o = _dot_nt((q * jnp.exp2(c)).astype(BF16), s_t.astype(BF16))
    k_end = (k * jnp.exp2(tot - c)).astype(BF16)
    s_ref[...] = s_t * jnp.exp2(tot) + jnp.where(bmask, _dot_tn(v.astype(BF16), k_end), 0.0)

    jobs = []
    m = c_len // 2
    while m >= SUB_BLK:
        nb = c_len // (2 * m)
        mid = m if anti else m - 1
        refs = [jnp.broadcast_to(c[b * 2 * m + mid:b * 2 * m + mid + 1], (2 * m, w)) for b in range(nb)]
        ref = refs[0] if nb == 1 else jnp.concatenate(refs, axis=0)
        e = jnp.exp2(-jnp.abs(c - ref))
        qt = q * e
        kt = k * e
        if 2 * m >= LEVEL_ROWS:
            for b in range(nb):
                first, second = b * 2 * m, b * 2 * m + m
                q0, k0 = (first, second) if anti else (second, first)
                jobs.append((m, q0, k0, m, _dot_nt(qt[q0:q0 + m].astype(BF16), _head_expand(kt[k0:k0 + m], hmasks))))
        else:
            for r0 in range(0, c_len, LEVEL_ROWS):
                a = _dot_nt(qt[r0:r0 + LEVEL_ROWS].astype(BF16), _head_expand(kt[r0:r0 + LEVEL_ROWS], hmasks))
                jobs.append((m, r0, r0, LEVEL_ROWS, jnp.where(level_masks[(m, anti)], a, 0.0)))
        m //= 2
    v_exp = {}
    pieces = {}
    for m, q0, k0, n, a in jobs:
        if (k0, n) not in v_exp:
            v_exp[(k0, n)] = _head_expand(v[k0:k0 + n], hmasks)
        pieces.setdefault(m, {})[q0] = _dot(a.astype(BF16), v_exp[(k0, n)])
    for m, level in pieces.items():
        n = next(iter(level.values())).shape[0]
        zero = jnp.zeros((n, w), F32)
        o = o + jnp.concatenate([level.get(r0, zero) for r0 in range(0, c_len, n)], axis=0)
    return o, (c, c - jnp.log(k) * LOG2_E, q, v)


def _hgrn_level_masks(c_len):
    masks = {}
    rg = LEVEL_ROWS
    m = LEVEL_ROWS // 4
    while m >= SUB_BLK:
        t = lax.broadcasted_iota(jnp.int32, (rg, N_HEADS * rg), 0)
        s = lax.broadcasted_iota(jnp.int32, (rg, N_HEADS * rg), 1) % rg
        same = (t // (2 * m)) == (s // (2 * m))
        t_late = (t // m) % 2 == 1
        s_late = (s // m) % 2 == 1
        masks[(m, False)] = same & t_late & ~s_late
        masks[(m, True)] = same & ~t_late & s_late
        m //= 2
    return masks


def _hgrn_diag_unit(cqkv, r, anti, ones_bf16):
    cs, cks, qs, vs = (a[r:r + SUB_BLK] for a in cqkv)
    row = lax.broadcasted_iota(jnp.int32, cs.shape, 0)
    prods = []
    for s in range(SUB_BLK):
        valid = (row <= s) if anti else (row >= s)
        prods.append(qs * jnp.exp2(jnp.where(valid, cs - cks[s:s + 1], -jnp.inf)))
    scores = _dot(jnp.concatenate(prods, axis=0).astype(BF16), ones_bf16)
    od = scores[0:SUB_BLK] * vs[0:1]
    for s in range(1, SUB_BLK):
        od = od + scores[s * SUB_BLK:(s + 1) * SUB_BLK] * vs[s:s + 1]
    return od


def _hgrn_kernel(p_ref, lbl_ref, gain_ref, o_ref, acc, s_f, s_b, *, layer, out_off):
    w = GROUP_W
    c_len = ROW_BLK
    nchunk = p_ref.shape[0] // c_len
    depth = lbl_ref.shape[0]

    logits = [lbl_ref[l] for l in range(depth)]
    mx = functools.reduce(jnp.maximum, logits)
    exps = [jnp.exp(l - mx) for l in logits]
    lb = sum(exps[1:layer + 1], jnp.zeros_like(mx)) * _recip(sum(exps))

    ri = lax.broadcasted_iota(jnp.int32, (c_len, c_len), 0)
    ci = lax.broadcasted_iota(jnp.int32, (c_len, c_len), 1)
    tril = (ci <= ri).astype(BF16)
    triu = (ci >= ri).astype(BF16)
    bmask = _head_ones(w)
    ones_bf16 = bmask.astype(BF16)
    consts = (tril, triu, bmask, _head_masks(w), _hgrn_level_masks(c_len))

    acc[...] = jnp.zeros_like(acc)
    s_f[...] = jnp.zeros_like(s_f)
    s_b[...] = jnp.zeros_like(s_b)

    def step(j, carry):
        rf, rb = _scan_rows(j, nchunk)
        of, cf = _hgrn_prepare(p_ref[pl.ds(rf, c_len), :], lb[0:1], 2, False, s_f, consts)
        ob, cb = _hgrn_prepare(p_ref[pl.ds(rb, c_len), :], lb[1:2], 3, True, s_b, consts)
        starts = range(0, c_len, SUB_BLK)
        od_f = jnp.concatenate([_hgrn_diag_unit(cf, r, False, ones_bf16) for r in starts], axis=0)
        od_b = jnp.concatenate([_hgrn_diag_unit(cb, r, True, ones_bf16) for r in starts], axis=0)
        acc[pl.ds(rf, c_len), :] += of + od_f
        acc[pl.ds(rb, c_len), :] += ob + od_b
        return carry

    lax.fori_loop(0, nchunk, step, 0)

    o = acc[out_off:, :]
    gate = p_ref[out_off:, 4 * w:5 * w]
    o_ref[...] = (_head_rms_norm(o, gain_ref[...], ones_bf16) * _silu(gate)).astype(o_ref.dtype)


def _hgrn(pa, lb_logits, gain, layer, batch, out_rows):
    n, width = pa.shape
    rows = n // batch
    w = GROUP_W
    scr = lambda r: pltpu.VMEM((r, w), F32)
    return pl.pallas_call(
        functools.partial(_hgrn_kernel, layer=layer, out_off=rows - out_rows),
        grid=(batch,),
        in_specs=[pl.BlockSpec((rows, width), lambda b: (b, 0)),
                  pl.BlockSpec(lb_logits.shape, lambda b: (0, 0, 0)),
                  pl.BlockSpec((SUBLANES, w), lambda b: (0, 0))],
        out_specs=pl.BlockSpec((out_rows, w), lambda b: (b, 0)),
        out_shape=jax.ShapeDtypeStruct((batch * out_rows, w), BF16),
        scratch_shapes=[scr(rows), scr(w), scr(w)],
        compiler_params=_params("parallel"),
        name="hgrn2_mixer",
    )(pa, lb_logits, _rows8(gain))


def _conv_kernel(p_ref, w_ref, o_ref, *, ctx_len, out_off):
    w = GROUP_W
    p = p_ref[...].astype(F32)
    u = p[:, w:2 * w] * p[:, 2 * w:3 * w]
    n = u.shape[0]
    row = lax.broadcasted_iota(jnp.int32, u.shape, 0)
    prev = jnp.where((row == 0) | (row == ctx_len), 0.0, pltpu.roll(u, 1, 0))
    nxt = jnp.where((row == ctx_len - 1) | (row == n - 1), 0.0, pltpu.roll(u, n - 1, 0))
    cw = w_ref[...]
    y = p[:, 0:w] * (cw[0:1] * prev + cw[1:2] * u + cw[2:3] * nxt)
    o_ref[...] = y[out_off:].astype(o_ref.dtype)


def _conv(pb, conv_w, batch, ctx_len, out_rows):
    n, width = pb.shape
    rows = n // batch
    return pl.pallas_call(
        functools.partial(_conv_kernel, ctx_len=ctx_len, out_off=rows - out_rows),
        grid=(batch,),
        in_specs=[pl.BlockSpec((rows, width), lambda b: (b, 0)),
                  pl.BlockSpec(conv_w.shape, lambda b: (0, 0))],
        out_specs=pl.BlockSpec((out_rows, GROUP_W), lambda b: (b, 0)),
        out_shape=jax.ShapeDtypeStruct((batch * out_rows, GROUP_W), BF16),
        compiler_params=_params("parallel"),
        name="conv_mixer",
    )(pb, conv_w)


def _ret_kernel(p_ref, dl_ref, gain_ref, o_ref, acc, s_f, s_b, d2_scr, *, out_off):
    w = GROUP_W
    c_len = ROW_BLK
    nchunk = p_ref.shape[0] // c_len
    hmasks = _head_masks(w)
    bmask = _head_ones(w)
    ones_bf16 = bmask.astype(BF16)

    lg = _log_sigmoid(dl_ref[...])
    lane_w = lambda r: sum(jnp.where(hm, lg[r:r + 1, h:h + 1], 0.0) for h, hm in enumerate(hmasks))
    lgf, lgb = lane_w(0), lane_w(1)
    t = lax.broadcasted_iota(jnp.int32, (c_len, w), 0).astype(F32)
    scale = HEAD_DIM ** -0.5
    qf_dec = jnp.exp(lgf * (t + 1.0)) * scale
    kf_dec = jnp.exp(lgf * (c_len - 1.0 - t))
    qb_dec = jnp.exp(lgb * (c_len - t)) * scale
    kb_dec = jnp.exp(lgb * t)
    tot_f = jnp.exp(lgf * float(c_len))
    tot_b = jnp.exp(lgb * float(c_len))

    @pl.when(pl.program_id(0) == 0)
    def _():
        col = lax.broadcasted_iota(jnp.int32, (c_len, N_HEADS * c_len), 1)
        row = lax.broadcasted_iota(jnp.int32, (c_len, N_HEADS * c_len), 0)
        col_head = lax.broadcasted_iota(jnp.int32, (1, N_HEADS * c_len), 1) // c_len
        col_lg = lambda r: sum(jnp.where(col_head == h, lg[r:r + 1, h:h + 1], 0.0) for h in range(N_HEADS))
        dist = (row - col % c_len).astype(F32)
        d2_scr[...] = (jnp.where(dist >= 0.0, jnp.exp(col_lg(0) * jnp.maximum(dist, 0.0)), 0.0)
                       + jnp.where(dist <= 0.0, jnp.exp(col_lg(1) * jnp.maximum(-dist, 0.0)), 0.0))

    acc[...] = jnp.zeros_like(acc)
    s_f[...] = jnp.zeros_like(s_f)
    s_b[...] = jnp.zeros_like(s_b)

    def step(j, carry):
        rf, rb = _scan_rows(j, nchunk)
        blk = p_ref[pl.ds(rf, c_len), :]
        q = blk[:, 0:w].astype(F32)
        k = blk[:, w:2 * w]
        v = blk[:, 2 * w:3 * w]
        blk = p_ref[pl.ds(rb, c_len), :]
        q2 = blk[:, 0:w].astype(F32)
        k2 = blk[:, w:2 * w].astype(F32)
        v2 = blk[:, 2 * w:3 * w]
        sf_t = s_f[...]
        sb_t = s_b[...]
        a = _dot_nt((q * scale).astype(BF16), _head_expand(k, hmasks))
        of = _dot_nt((q * qf_dec).astype(BF16), sf_t.astype(BF16))
        uf = _dot_tn(v, (k.astype(F32) * kf_dec).astype(BF16))
        ob = _dot_nt((q2 * qb_dec).astype(BF16), sb_t.astype(BF16))
        ub = _dot_tn(v2, (k2 * kb_dec).astype(BF16))
        s_f[...] = sf_t * tot_f + jnp.where(bmask, uf, 0.0)
        s_b[...] = sb_t * tot_b + jnp.where(bmask, ub, 0.0)
        acc[pl.ds(rf, c_len), :] += of + _dot((a * d2_scr[...]).astype(BF16), _head_expand(v, hmasks))
        acc[pl.ds(rb, c_len), :] += ob
        return carry

    lax.fori_loop(0, nchunk, step, 0)

    o = acc[out_off:, :]
    gate = p_ref[out_off:, 3 * w:4 * w].astype(F32)
    o_ref[...] = (_head_rms_norm(o, gain_ref[...], ones_bf16) * _silu(gate)).astype(o_ref.dtype)


def _retention(pc, decay_logit, gain, batch, out_rows):
    n, width = pc.shape
    rows = n // batch
    w = GROUP_W
    dl = jnp.zeros((8, LANES), F32).at[:decay_logit.shape[0], :decay_logit.shape[1]].set(decay_logit)
    return pl.pallas_call(
        functools.partial(_ret_kernel, out_off=rows - out_rows),
        grid=(batch,),
        in_specs=[pl.BlockSpec((rows, width), lambda b: (b, 0)),
                  pl.BlockSpec((8, LANES), lambda b: (0, 0)),
                  pl.BlockSpec((SUBLANES, w), lambda b: (0, 0))],
        out_specs=pl.BlockSpec((out_rows, w), lambda b: (b, 0)),
        out_shape=jax.ShapeDtypeStruct((batch * out_rows, w), BF16),
        scratch_shapes=[pltpu.VMEM((rows, w), F32), pltpu.VMEM((w, w), F32), pltpu.VMEM((w, w), F32),
                        pltpu.VMEM((ROW_BLK, N_HEADS * ROW_BLK), F32)],
        compiler_params=_params("arbitrary"),
        name="retention_mixer",
    )(pc, dl, _rows8(gain))


def _rope(x, cos, sin_signed):
    n = x.shape[-1]
    lane = lax.broadcasted_iota(jnp.int32, x.shape, 1)
    swapped = jnp.where(lane % 2 == 0, pltpu.roll(x, n - 1, 1), pltpu.roll(x, 1, 1))
    return x * cos + swapped * sin_signed


def _attn_kernel(q_ref, k_ref, v_ref, cos_ref, sin_ref, qn_ref, kn_ref, o_ref, k_buf, vt_buf, s_scr, *, qb_off):
    j = pl.program_id(1)
    nkb = k_ref.shape[0] // KEY_BLK

    @pl.when(j == 0)
    def _():
        ones_kv = _head_ones(KV_W).astype(BF16)
        kn = _head_rms_norm(k_ref[...].astype(F32), kn_ref[...], ones_kv)
        key_rows = k_ref.shape[0]
        k_buf[...] = _rope(kn, cos_ref[0:key_rows, 0:KV_W], sin_ref[0:key_rows, 0:KV_W]).astype(BF16)
        vt_buf[...] = v_ref[...].astype(F32).T.astype(BF16)

    qb = j + qb_off
    r0 = pl.multiple_of(qb * ROW_BLK, ROW_BLK)
    ones_q = _head_ones(GROUP_W).astype(BF16)
    qn = _head_rms_norm(q_ref[...].astype(F32), qn_ref[...], ones_q)
    qr = _rope(qn, cos_ref[pl.ds(r0, ROW_BLK), :], sin_ref[pl.ds(r0, ROW_BLK), :])
    qr = (qr * (HEAD_DIM ** -0.5 * LOG2_E)).astype(BF16)
    group = GROUP_W // KV_W
    ones_rows = jnp.ones((BF16_SUBLANES, KEY_BLK), BF16)
    n_kv = KV_W // HEAD_DIM
    ksl = lambda kv: slice(kv * HEAD_DIM, (kv + 1) * HEAD_DIM)
    q2 = [jnp.concatenate([qr[:, (kv * group + g) * HEAD_DIM:(kv * group + g + 1) * HEAD_DIM]
                           for g in range(group)], axis=0) for kv in range(n_kv)]

    def scores(kv, kb, m):
        s = _dot_nt(k_buf[kb * KEY_BLK:(kb + 1) * KEY_BLK, ksl(kv)], q2[kv])
        s_scr[kv, kb] = s
        bm = jnp.max(s, axis=0, keepdims=True)
        return bm if m is None else jnp.maximum(m, bm)

    def values(kv, kb, m, acc):
        p = jnp.exp2(s_scr[kv, kb] - m).astype(BF16)
        vt = jnp.concatenate([vt_buf[ksl(kv), kb * KEY_BLK:(kb + 1) * KEY_BLK], ones_rows], axis=0)
        return acc + _dot(vt, p)

    zero = jnp.zeros((HEAD_DIM + BF16_SUBLANES, group * ROW_BLK), F32)
    m_cur = None
    for kb in range(nkb):
        m_cur = scores(0, kb, m_cur)
    outs = []
    for kv in range(n_kv):
        m_next, acc = None, zero
        for kb in range(nkb):
            if kv + 1 < n_kv:
                m_next = scores(kv + 1, kb, m_next)
            acc = values(kv, kb, m_cur, acc)
        outs.append(acc[0:HEAD_DIM] * _recip(acc[HEAD_DIM:HEAD_DIM + 1]))
        m_cur = m_next
    o_t = jnp.concatenate(outs, axis=0).T
    for kv in range(n_kv):
        for g in range(group):
            h = kv * group + g
            o_ref[:, h * HEAD_DIM:(h + 1) * HEAD_DIM] = (
                o_t[g * ROW_BLK:(g + 1) * ROW_BLK, kv * HEAD_DIM:(kv + 1) * HEAD_DIM].astype(o_ref.dtype))


def _attention(pd, cos, sin_signed, q_norm, k_norm, batch, q_first, n_q, key_blks):
    n, width = pd.shape
    rows = n // batch
    bpb = rows // ROW_BLK
    assert bpb % key_blks == 0
    key_rows = key_blks * ROW_BLK
    kcol = GROUP_W // KV_W
    tile = lambda g, reps: _rows8(jnp.tile(g, reps))
    kv_spec = lambda col: pl.BlockSpec((key_rows, KV_W), lambda b, j: (b * (bpb // key_blks), col))
    return pl.pallas_call(
        functools.partial(_attn_kernel, qb_off=q_first),
        grid=(batch, n_q),
        in_specs=[pl.BlockSpec((ROW_BLK, GROUP_W), lambda b, j: (b * bpb + j + q_first, 0)),
                  kv_spec(kcol), kv_spec(kcol + 1),
                  pl.BlockSpec((rows, GROUP_W), lambda b, j: (0, 0)),
                  pl.BlockSpec((rows, GROUP_W), lambda b, j: (0, 0)),
                  pl.BlockSpec((SUBLANES, GROUP_W), lambda b, j: (0, 0)),
                  pl.BlockSpec((SUBLANES, KV_W), lambda b, j: (0, 0))],
        out_specs=pl.BlockSpec((ROW_BLK, GROUP_W), lambda b, j: (b * n_q + j, 0)),
        out_shape=jax.ShapeDtypeStruct((batch * n_q * ROW_BLK, GROUP_W), BF16),
        scratch_shapes=[pltpu.VMEM((key_rows, KV_W), BF16), pltpu.VMEM((KV_W, key_rows), BF16),
                        pltpu.VMEM((KV_W // HEAD_DIM, key_rows // KEY_BLK, KEY_BLK, (GROUP_W // KV_W) * ROW_BLK),
                                   F32)],
        compiler_params=_params("parallel", "arbitrary"),
        name="attention_mixer",
    )(pd, pd, pd, cos, sin_signed, tile(q_norm, N_HEADS), tile(k_norm, KV_W // HEAD_DIM))


def _attention_mixer(pd, cos, sin_signed, q_norm, k_norm, batch, ctx_len, with_ctx):
    rows = pd.shape[0] // batch
    bpb = rows // ROW_BLK
    cb = ctx_len // ROW_BLK
    gx = _attention(pd, cos, sin_signed, q_norm, k_norm, batch, cb, bpb - cb, bpb)
    if not with_ctx:
        return gx
    gc = _attention(pd, cos, sin_signed, q_norm, k_norm, batch, 0, cb, cb)
    return jnp.concatenate([gc.reshape(batch, ctx_len, GROUP_W), gx.reshape(batch, rows - ctx_len, GROUP_W)],
                           axis=1).reshape(batch * rows, GROUP_W)


def _rope_tables(ctx_len, seq_len):
    t = jnp.arange(seq_len)
    row = (t // GRID_W).astype(F32)
    col = (t % GRID_W).astype(F32)
    n_freq = HEAD_DIM // 4
    inv_freq = ROPE_THETA ** (-jnp.arange(n_freq, dtype=F32) / n_freq)
    ang = jnp.concatenate([row[:, None] * inv_freq, col[:, None] * inv_freq], axis=-1)
    cos = jnp.repeat(jnp.cos(ang), 2, axis=-1)
    sin = jnp.stack([-jnp.sin(ang), jnp.sin(ang)], axis=-1).reshape(seq_len, HEAD_DIM)
    cos = jnp.concatenate([jnp.ones((ctx_len, HEAD_DIM), F32), cos], axis=0)
    sin = jnp.concatenate([jnp.zeros((ctx_len, HEAD_DIM), F32), sin], axis=0)
    return jnp.tile(cos, (1, N_HEADS)), jnp.tile(sin, (1, N_HEADS))


def _route(lt, n_experts):
    n_groups = n_experts // EXPERTS_PER_GROUP
    first = lambda x, hit, n: jnp.min(jnp.where(hit, x, float(n)), axis=0, keepdims=True)

    gl = lt[n_experts:n_experts + n_groups]
    gexp = jnp.exp(gl - jnp.max(gl, axis=0, keepdims=True))
    gprob = gexp * _recip(jnp.sum(gexp, axis=0, keepdims=True))
    group_p = jnp.max(gprob, axis=0, keepdims=True)
    grow = lax.broadcasted_iota(jnp.int32, gl.shape, 0).astype(F32)
    gidx = first(grow, gprob == group_p, n_groups)

    el = sum(jnp.where(gidx == float(g), lt[g * EXPERTS_PER_GROUP:(g + 1) * EXPERTS_PER_GROUP], 0.0)
             for g in range(n_groups))
    eexp = jnp.exp(el - jnp.max(el, axis=0, keepdims=True))
    eprob = eexp * _recip(jnp.sum(eexp, axis=0, keepdims=True))
    erow = lax.broadcasted_iota(jnp.int32, el.shape, 0).astype(F32)
    p1 = jnp.max(eprob, axis=0, keepdims=True)
    i1 = first(erow, eprob == p1, EXPERTS_PER_GROUP)
    rest = erow != i1
    p2 = jnp.max(jnp.where(rest, eprob, -1.0), axis=0, keepdims=True)
    i2 = first(erow, rest & (eprob == p2), EXPERTS_PER_GROUP)
    scale = group_p * _recip(p1 + p2)
    lo = jnp.minimum(i1, i2)
    hi = jnp.maximum(i1, i2)
    w_lo = jnp.where(i1 < i2, p1, p2) * scale
    w_hi = jnp.where(i1 < i2, p2, p1) * scale
    pair = lo * (2.0 * EXPERTS_PER_GROUP - 1.0 - lo) * 0.5 + (hi - lo - 1.0)
    bucket = gidx * float(PAIRS_PER_GROUP) + pair
    return jnp.concatenate([bucket, w_lo, w_hi, jnp.zeros((LANES - 3, lt.shape[1]), F32)], axis=0)


def _outproj_kernel(*refs, n_experts, chunks, tiles_per_batch, with_ctx):
    n_fixed = 13
    srcs = refs[:-n_fixed]
    (a_ref, b_ref, c_ref, d_ref, modc_ref, modb_ref, gain_ref, w_ref, wr_ref, br_ref,
     x1_ref, h2_ref, rt_ref) = refs[-n_fixed:]
    d = x1_ref.shape[-1]
    rows = [slice(c * ROW_BLK, (c + 1) * ROW_BLK) for c in range(chunks)]
    mods = [modb_ref[0]] * chunks
    first_tile = pl.program_id(0) % tiles_per_batch == 0
    if with_ctx:
        mods[0] = jnp.where(first_tile, modc_ref[0], mods[0])
    xs = [r[...] for r in srcs[len(srcs) - chunks:]]
    if len(srcs) > chunks:
        xs[0] = jnp.where(first_tile, srcs[0][...], xs[0])
    projs = [_dot(jnp.concatenate([a_ref[rs, :], b_ref[rs, :], c_ref[rs, :], d_ref[rs, :]], axis=1), w_ref[...])
             for rs in rows]
    h2s = []
    for c, rs in enumerate(rows):
        x1 = xs[c] + _tile_rows(mods[c][:, 2 * d:3 * d], ROW_BLK) * projs[c]
        x1_ref[rs, :] = x1
        h2s.append(_norm_modulate(x1, gain_ref[...], mods[c], 3))
        _store_token_tiles(h2_ref, h2s[c], c * ROW_BLK)
    lts = [lax.dot_general(wr_ref[...], h2, (((1,), (1,)), ((), ())), precision=HIGHEST,
                           preferred_element_type=F32) + br_ref[...] for h2 in h2s]
    for rs, lt in zip(rows, lts):
        rt_ref[rs, :] = _route(lt, n_experts).T


def _outproj(srcs, mixers, mod, gain, w_bf16, w_router_t, b_router_t, n_experts, batch, ctx_row):
    d = srcs[0].shape[1]
    n_out = mixers[0].shape[0]
    out_bpb = n_out // ROW_BLK // batch
    tok_bpb = sum(s.shape[0] for s in srcs) // ROW_BLK // batch
    off = tok_bpb - out_bpb
    assert len(srcs) == 1 or off == 0
    chunks = max(c for c in (4, 3, 2, 1) if out_bpb % c == 0)
    tpb = out_bpb // chunks
    tr = chunks * ROW_BLK

    def x_specs(c):
        specs = _resid_specs(srcs, d, tok_bpb, lambda i: (i // tpb) * tok_bpb + (i % tpb) * chunks + c + off)
        return specs if c == 0 else specs[-1:]

    row_spec = lambda w: pl.BlockSpec((tr, w), lambda i: (i, 0))
    full = lambda a: pl.BlockSpec(a.shape, lambda i: (0,) * a.ndim)
    mod_spec = lambda idx: pl.BlockSpec((1, SUBLANES, mod.shape[-1]), idx)
    return pl.pallas_call(
        functools.partial(_outproj_kernel, n_experts=n_experts, chunks=chunks, tiles_per_batch=tpb,
                          with_ctx=off == 0),
        grid=(n_out // tr,),
        in_specs=[s for c in range(chunks) for s in x_specs(c)] + [
            row_spec(GROUP_W), row_spec(GROUP_W), row_spec(GROUP_W), row_spec(GROUP_W),
            mod_spec(lambda i: (ctx_row, 0, 0)), mod_spec(lambda i: (i // tpb, 0, 0)),
            pl.BlockSpec((SUBLANES, d), lambda i: (0, 0)),
            full(w_bf16), full(w_router_t), full(b_router_t)],
        out_specs=[row_spec(d),
                   pl.BlockSpec((tr * TOKEN_SUB, LANES), lambda i: (i, 0)),
                   row_spec(LANES)],
        out_shape=[jax.ShapeDtypeStruct((n_out, d), F32),
                   jax.ShapeDtypeStruct((n_out * TOKEN_SUB, LANES), F32),
                   jax.ShapeDtypeStruct((n_out, LANES), F32)],
        compiler_params=_params("parallel"),
        name="out_proj_router",
    )(*srcs, *([srcs[-1]] * (chunks - 1)), *mixers, mod, mod, _rows8(gain), w_bf16, w_router_t, b_router_t)


def _store_token_tiles(ref, x, first_token=0):
    rows = x.shape[0]
    for j in range(TOKEN_SUB):
        ref[pl.ds(first_token * TOKEN_SUB + j, rows, stride=TOKEN_SUB), :] = x[:, j * LANES:(j + 1) * LANES]


def _load_token_tiles(ref, first_token=0, rows=None):
    rows = ref.shape[0] // TOKEN_SUB if rows is None else rows
    return jnp.concatenate([ref[pl.ds(first_token * TOKEN_SUB + j, rows, stride=TOKEN_SUB), :]
                            for j in range(TOKEN_SUB)], axis=-1)


def _token_rows(t):
    return pl.ds(pl.multiple_of(t * TOKEN_SUB, TOKEN_SUB), TOKEN_SUB)


def _token_dmas(n, make_copy):
    def start(r, carry):
        make_copy(r).start()
        return carry

    def wait(r, carry):
        make_copy(r).wait()
        return carry

    lax.fori_loop(0, n, start, 0, unroll=8)
    lax.fori_loop(0, n, wait, 0, unroll=8)


def _pack_bf16_pair(a, b):
    au = lax.bitcast_convert_type(a.astype(BF16).astype(F32), jnp.uint32)
    bu = lax.bitcast_convert_type(b.astype(BF16).astype(F32), jnp.uint32)
    return au | (bu >> 16)


def _unpack_bf16_pair(u):
    a = lax.bitcast_convert_type(u & jnp.uint32(0xFFFF0000), F32)
    b = lax.bitcast_convert_type(u << 16, F32)
    return a, b


def _moe_plan(route, n_buckets):
    n = route.shape[0]
    nt = n // ROW_BLK
    n_tiles = nt + n_buckets
    bucket = route[:, 0].astype(jnp.int32)
    onehot = (bucket[:, None] == jnp.arange(n_buckets, dtype=jnp.int32)[None, :]).astype(F32)
    onehot = onehot.reshape(nt, ROW_BLK, n_buckets)
    tile_counts = jnp.sum(onehot, axis=1)
    before = jnp.cumsum(tile_counts, axis=0) - tile_counts
    earlier = (jnp.arange(ROW_BLK)[:, None] > jnp.arange(ROW_BLK)[None, :]).astype(F32)
    rank = jnp.einsum('ij,tjb->tib', earlier, onehot) + before[:, None, :]
    counts = jnp.sum(tile_counts, axis=0)
    tiles = jnp.ceil(counts / ROW_BLK)
    tend = jnp.cumsum(tiles)
    tstart = tend - tiles
    pos = jnp.sum(onehot * (rank + tstart * ROW_BLK), axis=-1).astype(jnp.int32).reshape(n)
    tile_id = jnp.arange(n_tiles, dtype=F32)
    tile_valid = (tile_id < tend[-1]).astype(jnp.int32)
    tile_bucket = jnp.sum((tile_id[:, None] >= tend[None, :]).astype(jnp.int32), axis=1)
    last_bucket = jnp.max(jnp.where(tiles > 0, jnp.arange(n_buckets, dtype=jnp.int32), 0))
    tile_bucket = jnp.minimum(tile_bucket, last_bucket)
    group = tile_bucket // PAIRS_PER_GROUP
    pair = tile_bucket % PAIRS_PER_GROUP
    lo = (pair >= 3).astype(jnp.int32) + (pair >= 5).astype(jnp.int32)
    hi = jnp.where(pair < 3, pair + 1, jnp.where(pair < 5, pair - 1, 3))
    last_tile = jnp.maximum(tend - 1.0, 0.0).astype(jnp.int32)
    has_tile = (tiles > 0).astype(jnp.int32)
    return group * EXPERTS_PER_GROUP + lo, group * EXPERTS_PER_GROUP + hi, tile_valid, pos, last_tile, has_tile


def _moe_scatter_kernel(last_ref, has_ref, pos_ref, h_ref, hs_out, zbuf, sem):
    tile_rows = zbuf.shape[0]

    @pl.when(pl.program_id(0) == 0)
    def _():
        zbuf[...] = jnp.zeros_like(zbuf)
        fill = lambda b: pltpu.make_async_copy(
            zbuf, hs_out.at[pl.ds(pl.multiple_of(last_ref[b] * tile_rows, tile_rows), tile_rows)], sem.at[1])
        for b in range(last_ref.shape[0]):
            @pl.when(has_ref[b] == 1)
            def _():
                fill(b).start()
        for b in range(last_ref.shape[0]):
            @pl.when(has_ref[b] == 1)
            def _():
                fill(b).wait()

    _token_dmas(h_ref.shape[0] // TOKEN_SUB,
                lambda r: pltpu.make_async_copy(h_ref.at[_token_rows(r)], hs_out.at[_token_rows(pos_ref[0, 0, r])],
                                                sem.at[0]))


def _moe_scatter(last_tile, has_tile, pos, h2, n_sorted, tm):
    n = h2.shape[0] // TOKEN_SUB
    grid_spec = pltpu.PrefetchScalarGridSpec(
        num_scalar_prefetch=2,
        grid=(n // tm,),
        in_specs=[pl.BlockSpec((1, 1, tm), lambda i, last, has: (i, 0, 0), memory_space=pltpu.SMEM),
                  pl.BlockSpec((tm * TOKEN_SUB, LANES), lambda i, last, has: (i, 0))],
        out_specs=pl.BlockSpec(memory_space=pl.ANY),
        scratch_shapes=[pltpu.VMEM((ROW_BLK * TOKEN_SUB, LANES), F32), pltpu.SemaphoreType.DMA((2,))],
    )
    return pl.pallas_call(
        _moe_scatter_kernel,
        grid_spec=grid_spec,
        out_shape=jax.ShapeDtypeStruct((n_sorted * TOKEN_SUB, LANES), F32),
        compiler_params=_params("arbitrary"),
        name="moe_scatter",
    )(last_tile, has_tile, pos.reshape(n // tm, 1, tm), h2)


def _moe_expert_kernel(lo_ref, hi_ref, valid_ref, h_ref, wg_lo, wu_lo, wd_lo, wg_hi, wu_hi, wd_hi, y_ref):
    i = pl.program_id(0)

    @pl.when(valid_ref[i] == 0)
    def _():
        y_ref[...] = jnp.zeros_like(y_ref)

    @pl.when(valid_ref[i] == 1)
    def _():
        h = _load_token_tiles(h_ref).astype(BF16)

        g_lo, u_lo = _dot(h, wg_lo[0]), _dot(h, wu_lo[0])
        g_hi, u_hi = _dot(h, wg_hi[0]), _dot(h, wu_hi[0])
        y_lo = _dot((_silu(g_lo) * u_lo).astype(BF16), wd_lo[0])
        y_hi = _dot((_silu(g_hi) * u_hi).astype(BF16), wd_hi[0])
        _store_token_tiles(y_ref, _pack_bf16_pair(y_lo, y_hi))


def _moe_experts(tile_lo, tile_hi, tile_valid, h_sorted, wg, wu, wd):
    n_tiles = tile_lo.shape[0]
    by_lo = lambda a: pl.BlockSpec((1,) + a.shape[1:], lambda i, lo, hi, v: (lo[i], 0, 0))
    by_hi = lambda a: pl.BlockSpec((1,) + a.shape[1:], lambda i, lo, hi, v: (hi[i], 0, 0))
    tile_spec = pl.BlockSpec((ROW_BLK * TOKEN_SUB, LANES), lambda i, lo, hi, v: (i, 0))
    grid_spec = pltpu.PrefetchScalarGridSpec(
        num_scalar_prefetch=3,
        grid=(n_tiles,),
        in_specs=[tile_spec, by_lo(wg), by_lo(wu), by_lo(wd), by_hi(wg), by_hi(wu), by_hi(wd)],
        out_specs=tile_spec,
    )
    return pl.pallas_call(
        _moe_expert_kernel,
        grid_spec=grid_spec,
        out_shape=jax.ShapeDtypeStruct((n_tiles * ROW_BLK * TOKEN_SUB, LANES), jnp.uint32),
        compiler_params=_params("parallel"),
        name="moe_experts",
    )(tile_lo, tile_hi, tile_valid, h_sorted, wg, wu, wd, wg, wu, wd)


def _moe_combine_kernel(pos_ref, pos_next_ref, y_hbm, rt_ref, x1_ref, modc_ref, modb_ref, o_ref, buf, sem,
                        *, ctx_len, tiles_per_batch):
    d = x1_ref.shape[-1]
    tm = x1_ref.shape[0]
    i = pl.program_id(0)
    half = i % 2

    def copies(idx_ref, h, op):
        def body(r, carry):
            op(pltpu.make_async_copy(y_hbm.at[_token_rows(idx_ref[0, 0, r])], buf.at[_token_rows(h * tm + r)],
                                     sem.at[h]))
            return carry
        lax.fori_loop(0, tm, body, 0, unroll=8)

    @pl.when(i == 0)
    def _():
        copies(pos_ref, 0, lambda cp: cp.start())

    @pl.when(i + 1 < pl.num_programs(0))
    def _():
        copies(pos_next_ref, 1 - half, lambda cp: cp.start())

    copies(pos_ref, half, lambda cp: cp.wait())
    y_lo, y_hi = _unpack_bf16_pair(_load_token_tiles(buf, half * tm, tm))
    rt = rt_ref[...]
    y = rt[:, 1:2] * y_lo + rt[:, 2:3] * y_hi
    gate = _tile_rows(modb_ref[0][:, 5 * d:6 * d], y.shape[0])
    if ctx_len:
        first = pl.program_id(0) % tiles_per_batch == 0
        row = lax.broadcasted_iota(jnp.int32, y.shape, 0)
        gate = jnp.where(first & (row < ctx_len), _tile_rows(modc_ref[0][:, 5 * d:6 * d], y.shape[0]), gate)
    o_ref[...] = x1_ref[...] + gate * y


def _moe_combine(pos, y_sorted, route, x1, mod, batch, ctx_len, ctx_row, with_ctx, tm):
    n, d = x1.shape
    tiles_per_batch = n // batch // tm
    n_tiles = n // tm
    row_spec = lambda w: pl.BlockSpec((tm, w), lambda i: (i, 0))
    pos = pos.reshape(n_tiles, 1, tm)
    return pl.pallas_call(
        functools.partial(_moe_combine_kernel, ctx_len=ctx_len if with_ctx else 0, tiles_per_batch=tiles_per_batch),
        grid=(n_tiles,),
        in_specs=[pl.BlockSpec((1, 1, tm), lambda i: (i, 0, 0), memory_space=pltpu.SMEM),
                  pl.BlockSpec((1, 1, tm), lambda i: (jnp.minimum(i + 1, n_tiles - 1), 0, 0),
                               memory_space=pltpu.SMEM),
                  pl.BlockSpec(memory_space=pl.ANY),
                  row_spec(LANES), row_spec(d),
                  pl.BlockSpec((1, SUBLANES, mod.shape[-1]), lambda i: (ctx_row, 0, 0)),
                  pl.BlockSpec((1, SUBLANES, mod.shape[-1]), lambda i: (i // tiles_per_batch, 0, 0))],
        out_specs=row_spec(d),
        out_shape=jax.ShapeDtypeStruct((n, d), F32),
        scratch_shapes=[pltpu.VMEM((2 * tm * TOKEN_SUB, LANES), jnp.uint32), pltpu.SemaphoreType.DMA((2,))],
        compiler_params=_params("arbitrary"),
        name="moe_combine",
    )(pos, pos, y_sorted, route, x1, mod, mod)


def kernel(x, c, ctx, c_ctx, ada_w, ada_b, norm_mix, norm_ffn, w_in, hgrn_lb_logits, hgrn_norm, conv_w,
           ret_decay_logit, ret_norm, q_norm, k_norm, w_out, router_group_w, router_group_b, router_expert_w,
           router_expert_b, expert_w_gate, expert_w_up, expert_w_down):
    batch, seq_len, d = x.shape
    ctx_len = ctx.shape[1]
    depth = ada_w.shape[0]
    n_experts = expert_w_gate.shape[1]
    n_groups = router_group_w.shape[-1]
    assert ctx_len == ROW_BLK and seq_len % ROW_BLK == 0 and batch + 1 <= MOD_ROWS
    assert n_experts + n_groups <= ROUTER_ROWS and n_experts == n_groups * EXPERTS_PER_GROUP
    assert d == TOKEN_SUB * LANES
    rows = ctx_len + seq_len
    bpb = rows // ROW_BLK
    ctx_row = batch

    cv = jnp.concatenate([c, c_ctx[None], jnp.zeros((MOD_ROWS - batch - 1, d), F32)], axis=0)
    mod_all = _modulation(cv, ada_w, ada_b)
    cos, sin_signed = _rope_tables(ctx_len, seq_len)
    if depth > 1:
        tokens = (ctx.reshape(batch * ctx_len, d), x.reshape(batch * seq_len, d))
    else:
        tokens = (jnp.concatenate([ctx, x], axis=1).reshape(batch * rows, d),)

    for layer in range(depth):
        with_ctx = layer < depth - 1
        out_rows = rows if with_ctx else seq_len
        mod = jnp.broadcast_to(mod_all[layer][:, None, :], (MOD_ROWS, SUBLANES, 6 * d))
        pa, pb, pc, pd = _inproj(tokens, mod, norm_mix[layer], w_in[layer].astype(BF16), bpb, ctx_row)
        mixers = (
            _hgrn(pa, hgrn_lb_logits, hgrn_norm[layer], layer, batch, out_rows),
            _conv(pb, conv_w[layer], batch, ctx_len, out_rows),
            _retention(pc, ret_decay_logit[layer], ret_norm[layer], batch, out_rows),
            _attention_mixer(pd, cos, sin_signed, q_norm[layer], k_norm[layer], batch, ctx_len, with_ctx),
        )
        pad = ROUTER_ROWS - n_experts - n_groups
        w_router_t = jnp.concatenate([router_expert_w[layer].T, router_group_w[layer].T, jnp.zeros((pad, d), F32)])
        b_router = jnp.concatenate([router_expert_b[layer], router_group_b[layer], jnp.zeros((pad,), F32)])
        b_router_t = jnp.broadcast_to(b_router[:, None], (ROUTER_ROWS, ROW_BLK))
        x1, h2, rt = _outproj(tokens, mixers, mod, norm_ffn[layer], w_out[layer].astype(BF16), w_router_t,
                              b_router_t, n_experts, batch, ctx_row)
        tile_lo, tile_hi, tile_valid, pos, last_tile, has_tile = _moe_plan(rt, n_groups * PAIRS_PER_GROUP)
        tm = max(t for t in range(ROW_BLK, 4 * ROW_BLK + 1, ROW_BLK) if out_rows % t == 0)
        h_sorted = _moe_scatter(last_tile, has_tile, pos, h2, tile_lo.shape[0] * ROW_BLK, tm)
        y_sorted = _moe_experts(tile_lo, tile_hi, tile_valid, h_sorted, expert_w_gate[layer].astype(BF16),
                                expert_w_up[layer].astype(BF16), expert_w_down[layer].astype(BF16))
        tokens = (_moe_combine(pos, y_sorted, rt, x1, mod, batch, ctx_len, ctx_row, with_ctx, tm),)
    return tokens[0].reshape(batch, seq_len, d)
```

```python
import functools

import jax
import jax.numpy as jnp
from jax import lax
from jax.experimental import pallas as pl
from jax.experimental.pallas import tpu as pltpu

F32 = jnp.float32
BF16 = jnp.bfloat16
HIGHEST = lax.Precision.HIGHEST

HEAD_DIM = 64
GROUP_W = 256
N_HEADS = GROUP_W // HEAD_DIM
KV_W = 128
GRID_W = 64
ROPE_THETA = 10000.0
EXPERTS_PER_GROUP = 4
NORM_EPS = 1e-6
ROW_BLK = 256
SUB_BLK = 8
LEVEL_ROWS = 128
KEY_BLK = 256
LANES = 128
SUBLANES = 8
BF16_SUBLANES = 16
LOG2_E = 1.4426950408889634
MOD_ROWS = 24
ROUTER_ROWS = 32
ROUTE_ROWS = 8
PAIRS_PER_GROUP = EXPERTS_PER_GROUP * (EXPERTS_PER_GROUP - 1) // 2
TOKEN_SUB = 8
VMEM_LIMIT_BYTES = 56 * 1024 * 1024
PA_W, PB_W, PC_W, PD_W = 5 * GROUP_W, 3 * GROUP_W, 4 * GROUP_W, GROUP_W + 2 * KV_W


def _params(*semantics):
    return pltpu.CompilerParams(dimension_semantics=semantics, vmem_limit_bytes=VMEM_LIMIT_BYTES)


def _dot(a, b):
    return jnp.dot(a, b, preferred_element_type=F32)


def _dot_nt(a, b):
    return lax.dot_general(a, b, (((1,), (1,)), ((), ())), preferred_element_type=F32)


def _dot_tn(a, b):
    return lax.dot_general(a, b, (((0,), (0,)), ((), ())), preferred_element_type=F32)


def _recip(x):
    return pl.reciprocal(x, approx=True)


def _sigmoid(x):
    return _recip(1.0 + jnp.exp(-x))


def _silu(x):
    return x * _sigmoid(x)


def _log_sigmoid(x):
    return jnp.minimum(x, 0.0) - jnp.log(1.0 + jnp.exp(-jnp.abs(x)))


def _split_bf16(x, terms):
    parts = []
    rem = x
    for i in range(terms):
        p = rem.astype(BF16)
        parts.append(p)
        if i + 1 < terms:
            rem = rem - p.astype(F32)
    return parts


def _head_ones(width):
    r = lax.broadcasted_iota(jnp.int32, (width, width), 0) // HEAD_DIM
    c = lax.broadcasted_iota(jnp.int32, (width, width), 1) // HEAD_DIM
    return r == c


def _head_masks(width):
    lane_head = lax.broadcasted_iota(jnp.int32, (1, width), 1) // HEAD_DIM
    return [lane_head == h for h in range(width // HEAD_DIM)]


def _head_expand(x, hmasks):
    zero = jnp.zeros_like(x)
    return jnp.concatenate([jnp.where(hm, x, zero) for hm in hmasks], axis=0).astype(BF16)


def _head_mean_sq(x, ones_bf16):
    hi, lo = _split_bf16(x * x, 2)
    return (_dot(hi, ones_bf16) + _dot(lo, ones_bf16)) * (1.0 / HEAD_DIM)


def _rows8(v):
    return jnp.broadcast_to(v.reshape(1, -1), (SUBLANES, v.size))


def _tile_rows(m8, rows):
    n = m8.shape[-1]
    return jnp.broadcast_to(m8[None], (rows // SUBLANES, SUBLANES, n)).reshape(rows, n)


def _head_rms_norm(x, gain8, ones_bf16):
    return x * lax.rsqrt(_head_mean_sq(x, ones_bf16) + NORM_EPS) * _tile_rows(gain8, x.shape[0])


def _norm_modulate(x, gain8, mod8, idx):
    rows, d = x.shape
    ms = jnp.mean(x * x, axis=-1, keepdims=True)
    y = x * lax.rsqrt(ms + NORM_EPS) * _tile_rows(gain8, rows)
    shift = _tile_rows(mod8[:, idx * d:(idx + 1) * d], rows)
    scale = _tile_rows(mod8[:, (idx + 1) * d:(idx + 2) * d], rows)
    return y * (1.0 + scale) + shift


def _scan_rows(j, nchunk):
    rf = pl.multiple_of(j * ROW_BLK, ROW_BLK)
    rb = pl.multiple_of(jnp.where(j == 0, 0, nchunk - j) * ROW_BLK, ROW_BLK)
    return rf, rb


def _mod_kernel(cv_ref, w_ref, b_ref, o_ref):
    s = _silu(cv_ref[...])
    o_ref[0] = jnp.dot(s, w_ref[0], precision=HIGHEST, preferred_element_type=F32) + b_ref[0]


def _modulation(cv, ada_w, ada_b):
    depth, d, n = ada_w.shape
    tn = n // 4
    return pl.pallas_call(
        _mod_kernel,
        grid=(depth, n // tn),
        in_specs=[pl.BlockSpec((MOD_ROWS, d), lambda l, j: (0, 0)),
                  pl.BlockSpec((1, d, tn), lambda l, j: (l, 0, j)),
                  pl.BlockSpec((1, 1, tn), lambda l, j: (l, 0, j))],
        out_specs=pl.BlockSpec((1, MOD_ROWS, tn), lambda l, j: (l, 0, j)),
        out_shape=jax.ShapeDtypeStruct((depth, MOD_ROWS, n), F32),
        compiler_params=_params("parallel", "parallel"),
        name="adaln_mod",
    )(cv, ada_w, ada_b.reshape(depth, 1, n))


def _inproj_kernel(*refs, blocks_per_batch):
    srcs, (mod_ref, gain_ref, w_ref, pa_ref, pb_ref, pc_ref, pd_ref) = refs[:-7], refs[-7:]
    x = srcs[-1][...]
    if len(srcs) == 2:
        x = jnp.where(pl.program_id(0) % blocks_per_batch == 0, srcs[0][...], x)
    h = _norm_modulate(x, gain_ref[...], mod_ref[0], 0)
    p = _dot(h.astype(BF16), w_ref[...])
    pa_ref[...] = p[:, 0:PA_W]
    pb_ref[...] = p[:, PA_W:PA_W + PB_W].astype(pb_ref.dtype)
    pc_ref[...] = p[:, PA_W + PB_W:PA_W + PB_W + PC_W].astype(pc_ref.dtype)
    pd_ref[...] = p[:, PA_W + PB_W + PC_W:].astype(pd_ref.dtype)


def _resid_specs(srcs, d, bpb, block_of):
    if len(srcs) == 1:
        return [pl.BlockSpec((ROW_BLK, d), lambda i: (block_of(i), 0))]
    return [pl.BlockSpec((ROW_BLK, d), lambda i: (block_of(i) // bpb, 0)),
            pl.BlockSpec((ROW_BLK, d), lambda i: ((block_of(i) // bpb) * (bpb - 1)
                                                  + jnp.maximum(block_of(i) % bpb - 1, 0), 0))]


def _inproj(srcs, mod, gain, w_bf16, blocks_per_batch, ctx_row):
    n = sum(s.shape[0] for s in srcs)
    d = srcs[0].shape[1]
    nblk = n // ROW_BLK

    def mod_idx(i):
        return (jnp.where(i % blocks_per_batch == 0, ctx_row, i // blocks_per_batch), 0, 0)

    outs = ((PA_W, F32), (PB_W, BF16), (PC_W, BF16), (PD_W, BF16))
    return pl.pallas_call(
        functools.partial(_inproj_kernel, blocks_per_batch=blocks_per_batch),
        grid=(nblk,),
        in_specs=_resid_specs(srcs, d, blocks_per_batch, lambda i: i) + [
                  pl.BlockSpec((1, SUBLANES, mod.shape[-1]), mod_idx),
                  pl.BlockSpec((SUBLANES, d), lambda i: (0, 0)),
                  pl.BlockSpec(w_bf16.shape, lambda i: (0, 0))],
        out_specs=[pl.BlockSpec((ROW_BLK, w), lambda i: (i, 0)) for w, _ in outs],
        out_shape=[jax.ShapeDtypeStruct((n, w), dt) for w, dt in outs],
        compiler_params=_params("parallel"),
        name="in_proj",
    )(*srcs, mod, _rows8(gain), w_bf16)


def _hgrn_prepare(blk, lb_row, z_col, anti, s_ref, consts):
    tril, triu, bmask, hmasks, level_masks = consts
    w = GROUP_W
    c_len = blk.shape[0]
    q = blk[:, 0:w]
    v = blk[:, w:2 * w]
    z = blk[:, z_col * w:(z_col + 1) * w]
    f = lb_row + (1.0 - lb_row) * _sigmoid(z)
    g = jnp.log(f) * LOG2_E
    k = 1.0 - f
    tri = triu if anti else tril
    c = sum(_dot(tri, part) for part in _split_bf16(g, 3))
    tot = c[0:1] if anti else c[c_len - 1:c_len]

    s_t = s_ref[...]
    o = _dot_nt((q * jnp.exp2(c)).astype(BF16), s_t.astype(BF16))
    k_end = (k * jnp.exp2(tot - c)).astype(BF16)
    s_ref[...] = s_t * jnp.exp2(tot) + jnp.where(bmask, _dot_tn(v.astype(BF16), k_end), 0.0)

    jobs = []
    m = c_len // 2
    while m >= SUB_BLK:
        nb = c_len // (2 * m)
        mid = m if anti else m - 1
        refs = [jnp.broadcast_to(c[b * 2 * m + mid:b * 2 * m + mid + 1], (2 * m, w)) for b in range(nb)]
        ref = refs[0] if nb == 1 else jnp.concatenate(refs, axis=0)
        e = jnp.exp2(-jnp.abs(c - ref))
        qt = q * e
        kt = k * e
        if 2 * m >= LEVEL_ROWS:
            for b in range(nb):
                first, second = b * 2 * m, b * 2 * m + m
                q0, k0 = (first, second) if anti else (second, first)
                jobs.append((m, q0, k0, m, _dot_nt(qt[q0:q0 + m].astype(BF16), _head_expand(kt[k0:k0 + m], hmasks))))
        else:
            for r0 in range(0, c_len, LEVEL_ROWS):
                a = _dot_nt(qt[r0:r0 + LEVEL_ROWS].astype(BF16), _head_expand(kt[r0:r0 + LEVEL_ROWS], hmasks))
                jobs.append((m, r0, r0, LEVEL_ROWS, jnp.where(level_masks[(m, anti)], a, 0.0)))
        m //= 2
    v_exp = {}
    pieces = {}
    for m, q0, k0, n, a in jobs:
        if (k0, n) not in v_exp:
            v_exp[(k0, n)] = _head_expand(v[k0:k0 + n], hmasks)
        pieces.setdefault(m, {})[q0] = _dot(a.astype(BF16), v_exp[(k0, n)])
    for m, level in pieces.items():
        n = next(iter(level.values())).shape[0]
        zero = jnp.zeros((n, w), F32)
        o = o + jnp.concatenate([level.get(r0, zero) for r0 in range(0, c_len, n)], axis=0)
    return o, (c, c - jnp.log(k) * LOG2_E, q, v)


def _hgrn_level_masks(c_len):
    masks = {}
    rg = LEVEL_ROWS
    m = LEVEL_ROWS // 4
    while m >= SUB_BLK:
        t = lax.broadcasted_iota(jnp.int32, (rg, N_HEADS * rg), 0)
        s = lax.broadcasted_iota(jnp.int32, (rg, N_HEADS * rg), 1) % rg
        same = (t // (2 * m)) == (s // (2 * m))
        t_late = (t // m) % 2 == 1
        s_late = (s // m) % 2 == 1
        masks[(m, False)] = same & t_late & ~s_late
        masks[(m, True)] = same & ~t_late & s_late
        m //= 2
    return masks


def _hgrn_diag_unit(cqkv, r, anti, ones_bf16):
    cs, cks, qs, vs = (a[r:r + SUB_BLK] for a in cqkv)
    row = lax.broadcasted_iota(jnp.int32, cs.shape, 0)
    prods = []
    for s in range(SUB_BLK):
        valid = (row <= s) if anti else (row >= s)
        prods.append(qs * jnp.exp2(jnp.where(valid, cs - cks[s:s + 1], -jnp.inf)))
    scores = _dot(jnp.concatenate(prods, axis=0).astype(BF16), ones_bf16)
    od = scores[0:SUB_BLK] * vs[0:1]
    for s in range(1, SUB_BLK):
        od = od + scores[s * SUB_BLK:(s + 1) * SUB_BLK] * vs[s:s + 1]
    return od


def _hgrn_kernel(p_ref, lbl_ref, gain_ref, o_ref, acc, s_f, s_b, *, layer, out_off):
    w = GROUP_W
    c_len = ROW_BLK
    nchunk = p_ref.shape[0] // c_len
    depth = lbl_ref.shape[0]

    logits = [lbl_ref[l] for l in range(depth)]
    mx = functools.reduce(jnp.maximum, logits)
    exps = [jnp.exp(l - mx) for l in logits]
    lb = sum(exps[1:layer + 1], jnp.zeros_like(mx)) * _recip(sum(exps))

    ri = lax.broadcasted_iota(jnp.int32, (c_len, c_len), 0)
    ci = lax.broadcasted_iota(jnp.int32, (c_len, c_len), 1)
    tril = (ci <= ri).astype(BF16)
    triu = (ci >= ri).astype(BF16)
    bmask = _head_ones(w)
    ones_bf16 = bmask.astype(BF16)
    consts = (tril, triu, bmask, _head_masks(w), _hgrn_level_masks(c_len))

    acc[...] = jnp.zeros_like(acc)
    s_f[...] = jnp.zeros_like(s_f)
    s_b[...] = jnp.zeros_like(s_b)

    def step(j, carry):
        rf, rb = _scan_rows(j, nchunk)
        of, cf = _hgrn_prepare(p_ref[pl.ds(rf, c_len), :], lb[0:1], 2, False, s_f, consts)
        ob, cb = _hgrn_prepare(p_ref[pl.ds(rb, c_len), :], lb[1:2], 3, True, s_b, consts)
        starts = range(0, c_len, SUB_BLK)
        od_f = jnp.concatenate([_hgrn_diag_unit(cf, r, False, ones_bf16) for r in starts], axis=0)
        od_b = jnp.concatenate([_hgrn_diag_unit(cb, r, True, ones_bf16) for r in starts], axis=0)
        acc[pl.ds(rf, c_len), :] += of + od_f
        acc[pl.ds(rb, c_len), :] += ob + od_b
        return carry

    lax.fori_loop(0, nchunk, step, 0)

    o = acc[out_off:, :]
    gate = p_ref[out_off:, 4 * w:5 * w]
    o_ref[...] = (_head_rms_norm(o, gain_ref[...], ones_bf16) * _silu(gate)).astype(o_ref.dtype)


def _hgrn(pa, lb_logits, gain, layer, batch, out_rows):
    n, width = pa.shape
    rows = n // batch
    w = GROUP_W
    scr = lambda r: pltpu.VMEM((r, w), F32)
    return pl.pallas_call(
        functools.partial(_hgrn_kernel, layer=layer, out_off=rows - out_rows),
        grid=(batch,),
        in_specs=[pl.BlockSpec((rows, width), lambda b: (b, 0)),
                  pl.BlockSpec(lb_logits.shape, lambda b: (0, 0, 0)),
                  pl.BlockSpec((SUBLANES, w), lambda b: (0, 0))],
        out_specs=pl.BlockSpec((out_rows, w), lambda b: (b, 0)),
        out_shape=jax.ShapeDtypeStruct((batch * out_rows, w), BF16),
        scratch_shapes=[scr(rows), scr(w), scr(w)],
        compiler_params=_params("parallel"),
        name="hgrn2_mixer",
    )(pa, lb_logits, _rows8(gain))


def _conv_kernel(p_ref, w_ref, o_ref, *, ctx_len, out_off):
    w = GROUP_W
    p = p_ref[...].astype(F32)
    u = p[:, w:2 * w] * p[:, 2 * w:3 * w]
    n = u.shape[0]
    row = lax.broadcasted_iota(jnp.int32, u.shape, 0)
    prev = jnp.where((row == 0) | (row == ctx_len), 0.0, pltpu.roll(u, 1, 0))
    nxt = jnp.where((row == ctx_len - 1) | (row == n - 1), 0.0, pltpu.roll(u, n - 1, 0))
    cw = w_ref[...]
    y = p[:, 0:w] * (cw[0:1] * prev + cw[1:2] * u + cw[2:3] * nxt)
    o_ref[...] = y[out_off:].astype(o_ref.dtype)


def _conv(pb, conv_w, batch, ctx_len, out_rows):
    n, width = pb.shape
    rows = n // batch
    return pl.pallas_call(
        functools.partial(_conv_kernel, ctx_len=ctx_len, out_off=rows - out_rows),
        grid=(batch,),
        in_specs=[pl.BlockSpec((rows, width), lambda b: (b, 0)),
                  pl.BlockSpec(conv_w.shape, lambda b: (0, 0))],
        out_specs=pl.BlockSpec((out_rows, GROUP_W), lambda b: (b, 0)),
        out_shape=jax.ShapeDtypeStruct((batch * out_rows, GROUP_W), BF16),
        compiler_params=_params("parallel"),
        name="conv_mixer",
    )(pb, conv_w)


def _ret_kernel(p_ref, dl_ref, gain_ref, o_ref, acc, s_f, s_b, d2_scr, *, out_off):
    w = GROUP_W
    c_len = ROW_BLK
    nchunk = p_ref.shape[0] // c_len
    hmasks = _head_masks(w)
    bmask = _head_ones(w)
    ones_bf16 = bmask.astype(BF16)

    lg = _log_sigmoid(dl_ref[...])
    lane_w = lambda r: sum(jnp.where(hm, lg[r:r + 1, h:h + 1], 0.0) for h, hm in enumerate(hmasks))
    lgf, lgb = lane_w(0), lane_w(1)
    t = lax.broadcasted_iota(jnp.int32, (c_len, w), 0).astype(F32)
    scale = HEAD_DIM ** -0.5
    qf_dec = jnp.exp(lgf * (t + 1.0)) * scale
    kf_dec = jnp.exp(lgf * (c_len - 1.0 - t))
    qb_dec = jnp.exp(lgb * (c_len - t)) * scale
    kb_dec = jnp.exp(lgb * t)
    tot_f = jnp.exp(lgf * float(c_len))
    tot_b = jnp.exp(lgb * float(c_len))

    @pl.when(pl.program_id(0) == 0)
    def _():
        col = lax.broadcasted_iota(jnp.int32, (c_len, N_HEADS * c_len), 1)
        row = lax.broadcasted_iota(jnp.int32, (c_len, N_HEADS * c_len), 0)
        col_head = lax.broadcasted_iota(jnp.int32, (1, N_HEADS * c_len), 1) // c_len
        col_lg = lambda r: sum(jnp.where(col_head == h, lg[r:r + 1, h:h + 1], 0.0) for h in range(N_HEADS))
        dist = (row - col % c_len).astype(F32)
        d2_scr[...] = (jnp.where(dist >= 0.0, jnp.exp(col_lg(0) * jnp.maximum(dist, 0.0)), 0.0)
                       + jnp.where(dist <= 0.0, jnp.exp(col_lg(1) * jnp.maximum(-dist, 0.0)), 0.0))

    acc[...] = jnp.zeros_like(acc)
    s_f[...] = jnp.zeros_like(s_f)
    s_b[...] = jnp.zeros_like(s_b)

    def step(j, carry):
        rf, rb = _scan_rows(j, nchunk)
        blk = p_ref[pl.ds(rf, c_len), :]
        q = blk[:, 0:w].astype(F32)
        k = blk[:, w:2 * w]
        v = blk[:, 2 * w:3 * w]
        blk = p_ref[pl.ds(rb, c_len), :]
        q2 = blk[:, 0:w].astype(F32)
        k2 = blk[:, w:2 * w].astype(F32)
        v2 = blk[:, 2 * w:3 * w]
        sf_t = s_f[...]
        sb_t = s_b[...]
        a = _dot_nt((q * scale).astype(BF16), _head_expand(k, hmasks))
        of = _dot_nt((q * qf_dec).astype(BF16), sf_t.astype(BF16))
        uf = _dot_tn(v, (k.astype(F32) * kf_dec).astype(BF16))
        ob = _dot_nt((q2 * qb_dec).astype(BF16), sb_t.astype(BF16))
        ub = _dot_tn(v2, (k2 * kb_dec).astype(BF16))
        s_f[...] = sf_t * tot_f + jnp.where(bmask, uf, 0.0)
        s_b[...] = sb_t * tot_b + jnp.where(bmask, ub, 0.0)
        acc[pl.ds(rf, c_len), :] += of + _dot((a * d2_scr[...]).astype(BF16), _head_expand(v, hmasks))
        acc[pl.ds(rb, c_len), :] += ob
        return carry

    lax.fori_loop(0, nchunk, step, 0)

    o = acc[out_off:, :]
    gate = p_ref[out_off:, 3 * w:4 * w].astype(F32)
    o_ref[...] = (_head_rms_norm(o, gain_ref[...], ones_bf16) * _silu(gate)).astype(o_ref.dtype)


def _retention(pc, decay_logit, gain, batch, out_rows):
    n, width = pc.shape
    rows = n // batch
    w = GROUP_W
    dl = jnp.zeros((8, LANES), F32).at[:decay_logit.shape[0], :decay_logit.shape[1]].set(decay_logit)
    return pl.pallas_call(
        functools.partial(_ret_kernel, out_off=rows - out_rows),
        grid=(batch,),
        in_specs=[pl.BlockSpec((rows, width), lambda b: (b, 0)),
                  pl.BlockSpec((8, LANES), lambda b: (0, 0)),
                  pl.BlockSpec((SUBLANES, w), lambda b: (0, 0))],
        out_specs=pl.BlockSpec((out_rows, w), lambda b: (b, 0)),
        out_shape=jax.ShapeDtypeStruct((batch * out_rows, w), BF16),
        scratch_shapes=[pltpu.VMEM((rows, w), F32), pltpu.VMEM((w, w), F32), pltpu.VMEM((w, w), F32),
                        pltpu.VMEM((ROW_BLK, N_HEADS * ROW_BLK), F32)],
        compiler_params=_params("arbitrary"),
        name="retention_mixer",
    )(pc, dl, _rows8(gain))


def _rope(x, cos, sin_signed):
    n = x.shape[-1]
    lane = lax.broadcasted_iota(jnp.int32, x.shape, 1)
    swapped = jnp.where(lane % 2 == 0, pltpu.roll(x, n - 1, 1), pltpu.roll(x, 1, 1))
    return x * cos + swapped * sin_signed


def _attn_kernel(q_ref, k_ref, v_ref, cos_ref, sin_ref, qn_ref, kn_ref, o_ref, k_buf, vt_buf, s_scr, *, qb_off):
    j = pl.program_id(1)
    nkb = k_ref.shape[0] // KEY_BLK

    @pl.when(j == 0)
    def _():
        ones_kv = _head_ones(KV_W).astype(BF16)
        kn = _head_rms_norm(k_ref[...].astype(F32), kn_ref[...], ones_kv)
        key_rows = k_ref.shape[0]
        k_buf[...] = _rope(kn, cos_ref[0:key_rows, 0:KV_W], sin_ref[0:key_rows, 0:KV_W]).astype(BF16)
        vt_buf[...] = v_ref[...].astype(F32).T.astype(BF16)

    qb = j + qb_off
    r0 = pl.multiple_of(qb * ROW_BLK, ROW_BLK)
    ones_q = _head_ones(GROUP_W).astype(BF16)
    qn = _head_rms_norm(q_ref[...].astype(F32), qn_ref[...], ones_q)
    qr = _rope(qn, cos_ref[pl.ds(r0, ROW_BLK), :], sin_ref[pl.ds(r0, ROW_BLK), :])
    qr = (qr * (HEAD_DIM ** -0.5 * LOG2_E)).astype(BF16)
    group = GROUP_W // KV_W
    ones_rows = jnp.ones((BF16_SUBLANES, KEY_BLK), BF16)
    n_kv = KV_W // HEAD_DIM
    ksl = lambda kv: slice(kv * HEAD_DIM, (kv + 1) * HEAD_DIM)
    q2 = [jnp.concatenate([qr[:, (kv * group + g) * HEAD_DIM:(kv * group + g + 1) * HEAD_DIM]
                           for g in range(group)], axis=0) for kv in range(n_kv)]

    def scores(kv, kb, m):
        s = _dot_nt(k_buf[kb * KEY_BLK:(kb + 1) * KEY_BLK, ksl(kv)], q2[kv])
        s_scr[kv, kb] = s
        bm = jnp.max(s, axis=0, keepdims=True)
        return bm if m is None else jnp.maximum(m, bm)

    def values(kv, kb, m, acc):
        p = jnp.exp2(s_scr[kv, kb] - m).astype(BF16)
        vt = jnp.concatenate([vt_buf[ksl(kv), kb * KEY_BLK:(kb + 1) * KEY_BLK], ones_rows], axis=0)
        return acc + _dot(vt, p)

    zero = jnp.zeros((HEAD_DIM + BF16_SUBLANES, group * ROW_BLK), F32)
    m_cur = None
    for kb in range(nkb):
        m_cur = scores(0, kb, m_cur)
    outs = []
    for kv in range(n_kv):
        m_next, acc = None, zero
        for kb in range(nkb):
            if kv + 1 < n_kv:
                m_next = scores(kv + 1, kb, m_next)
            acc = values(kv, kb, m_cur, acc)
        outs.append(acc[0:HEAD_DIM] * _recip(acc[HEAD_DIM:HEAD_DIM + 1]))
        m_cur = m_next
    o_t = jnp.concatenate(outs, axis=0).T
    for kv in range(n_kv):
        for g in range(group):
            h = kv * group + g
            o_ref[:, h * HEAD_DIM:(h + 1) * HEAD_DIM] = (
                o_t[g * ROW_BLK:(g + 1) * ROW_BLK, kv * HEAD_DIM:(kv + 1) * HEAD_DIM].astype(o_ref.dtype))


def _attention(pd, cos, sin_signed, q_norm, k_norm, batch, q_first, n_q, key_blks):
    n, width = pd.shape
    rows = n // batch
    bpb = rows // ROW_BLK
    assert bpb % key_blks == 0
    key_rows = key_blks * ROW_BLK
    kcol = GROUP_W // KV_W
    tile = lambda g, reps: _rows8(jnp.tile(g, reps))
    kv_spec = lambda col: pl.BlockSpec((key_rows, KV_W), lambda b, j: (b * (bpb // key_blks), col))
    return pl.pallas_call(
        functools.partial(_attn_kernel, qb_off=q_first),
        grid=(batch, n_q),
        in_specs=[pl.BlockSpec((ROW_BLK, GROUP_W), lambda b, j: (b * bpb + j + q_first, 0)),
                  kv_spec(kcol), kv_spec(kcol + 1),
                  pl.BlockSpec((rows, GROUP_W), lambda b, j: (0, 0)),
                  pl.BlockSpec((rows, GROUP_W), lambda b, j: (0, 0)),
                  pl.BlockSpec((SUBLANES, GROUP_W), lambda b, j: (0, 0)),
                  pl.BlockSpec((SUBLANES, KV_W), lambda b, j: (0, 0))],
        out_specs=pl.BlockSpec((ROW_BLK, GROUP_W), lambda b, j: (b * n_q + j, 0)),
        out_shape=jax.ShapeDtypeStruct((batch * n_q * ROW_BLK, GROUP_W), BF16),
        scratch_shapes=[pltpu.VMEM((key_rows, KV_W), BF16), pltpu.VMEM((KV_W, key_rows), BF16),
                        pltpu.VMEM((KV_W // HEAD_DIM, key_rows // KEY_BLK, KEY_BLK, (GROUP_W // KV_W) * ROW_BLK),
                                   F32)],
        compiler_params=_params("parallel", "arbitrary"),
        name="attention_mixer",
    )(pd, pd, pd, cos, sin_signed, tile(q_norm, N_HEADS), tile(k_norm, KV_W // HEAD_DIM))


def _attention_mixer(pd, cos, sin_signed, q_norm, k_norm, batch, ctx_len, with_ctx):
    rows = pd.shape[0] // batch
    bpb = rows // ROW_BLK
    cb = ctx_len // ROW_BLK
    gx = _attention(pd, cos, sin_signed, q_norm, k_norm, batch, cb, bpb - cb, bpb)
    if not with_ctx:
        return gx
    gc = _attention(pd, cos, sin_signed, q_norm, k_norm, batch, 0, cb, cb)
    return jnp.concatenate([gc.reshape(batch, ctx_len, GROUP_W), gx.reshape(batch, rows - ctx_len, GROUP_W)],
                           axis=1).reshape(batch * rows, GROUP_W)


def _rope_tables(ctx_len, seq_len):
    t = jnp.arange(seq_len)
    row = (t // GRID_W).astype(F32)
    col = (t % GRID_W).astype(F32)
    n_freq = HEAD_DIM // 4
    inv_freq = ROPE_THETA ** (-jnp.arange(n_freq, dtype=F32) / n_freq)
    ang = jnp.concatenate([row[:, None] * inv_freq, col[:, None] * inv_freq], axis=-1)
    cos = jnp.repeat(jnp.cos(ang), 2, axis=-1)
    sin = jnp.stack([-jnp.sin(ang), jnp.sin(ang)], axis=-1).reshape(seq_len, HEAD_DIM)
    cos = jnp.concatenate([jnp.ones((ctx_len, HEAD_DIM), F32), cos], axis=0)
    sin = jnp.concatenate([jnp.zeros((ctx_len, HEAD_DIM), F32), sin], axis=0)
    return jnp.tile(cos, (1, N_HEADS)), jnp.tile(sin, (1, N_HEADS))


def _route(lt, n_experts):
    n_groups = n_experts // EXPERTS_PER_GROUP
    first = lambda x, hit, n: jnp.min(jnp.where(hit, x, float(n)), axis=0, keepdims=True)

    gl = lt[n_experts:n_experts + n_groups]
    gexp = jnp.exp(gl - jnp.max(gl, axis=0, keepdims=True))
    gprob = gexp * _recip(jnp.sum(gexp, axis=0, keepdims=True))
    group_p = jnp.max(gprob, axis=0, keepdims=True)
    grow = lax.broadcasted_iota(jnp.int32, gl.shape, 0).astype(F32)
    gidx = first(grow, gprob == group_p, n_groups)

    el = sum(jnp.where(gidx == float(g), lt[g * EXPERTS_PER_GROUP:(g + 1) * EXPERTS_PER_GROUP], 0.0)
             for g in range(n_groups))
    eexp = jnp.exp(el - jnp.max(el, axis=0, keepdims=True))
    eprob = eexp * _recip(jnp.sum(eexp, axis=0, keepdims=True))
    erow = lax.broadcasted_iota(jnp.int32, el.shape, 0).astype(F32)
    p1 = jnp.max(eprob, axis=0, keepdims=True)
    i1 = first(erow, eprob == p1, EXPERTS_PER_GROUP)
    rest = erow != i1
    p2 = jnp.max(jnp.where(rest, eprob, -1.0), axis=0, keepdims=True)
    i2 = first(erow, rest & (eprob == p2), EXPERTS_PER_GROUP)
    scale = group_p * _recip(p1 + p2)
    lo = jnp.minimum(i1, i2)
    hi = jnp.maximum(i1, i2)
    w_lo = jnp.where(i1 < i2, p1, p2) * scale
    w_hi = jnp.where(i1 < i2, p2, p1) * scale
    pair = lo * (2.0 * EXPERTS_PER_GROUP - 1.0 - lo) * 0.5 + (hi - lo - 1.0)
    bucket = gidx * float(PAIRS_PER_GROUP) + pair
    return jnp.concatenate([bucket, w_lo, w_hi, jnp.zeros((LANES - 3, lt.shape[1]), F32)], axis=0)


def _outproj_kernel(*refs, n_experts, chunks, tiles_per_batch, with_ctx):
    n_fixed = 13
    srcs = refs[:-n_fixed]
    (a_ref, b_ref, c_ref, d_ref, modc_ref, modb_ref, gain_ref, w_ref, wr_ref, br_ref,
     x1_ref, h2_ref, rt_ref) = refs[-n_fixed:]
    d = x1_ref.shape[-1]
    rows = [slice(c * ROW_BLK, (c + 1) * ROW_BLK) for c in range(chunks)]
    mods = [modb_ref[0]] * chunks
    first_tile = pl.program_id(0) % tiles_per_batch == 0
    if with_ctx:
        mods[0] = jnp.where(first_tile, modc_ref[0], mods[0])
    xs = [r[...] for r in srcs[len(srcs) - chunks:]]
    if len(srcs) > chunks:
        xs[0] = jnp.where(first_tile, srcs[0][...], xs[0])
    projs = [_dot(jnp.concatenate([a_ref[rs, :], b_ref[rs, :], c_ref[rs, :], d_ref[rs, :]], axis=1), w_ref[...])
             for rs in rows]
    h2s = []
    for c, rs in enumerate(rows):
        x1 = xs[c] + _tile_rows(mods[c][:, 2 * d:3 * d], ROW_BLK) * projs[c]
        x1_ref[rs, :] = x1
        h2s.append(_norm_modulate(x1, gain_ref[...], mods[c], 3))
        _store_token_tiles(h2_ref, h2s[c], c * ROW_BLK)
    lts = [lax.dot_general(wr_ref[...], h2, (((1,), (1,)), ((), ())), precision=HIGHEST,
                           preferred_element_type=F32) + br_ref[...] for h2 in h2s]
    for rs, lt in zip(rows, lts):
        rt_ref[rs, :] = _route(lt, n_experts).T


def _outproj(srcs, mixers, mod, gain, w_bf16, w_router_t, b_router_t, n_experts, batch, ctx_row):
    d = srcs[0].shape[1]
    n_out = mixers[0].shape[0]
    out_bpb = n_out // ROW_BLK // batch
    tok_bpb = sum(s.shape[0] for s in srcs) // ROW_BLK // batch
    off = tok_bpb - out_bpb
    assert len(srcs) == 1 or off == 0
    chunks = max(c for c in (4, 3, 2, 1) if out_bpb % c == 0)
    tpb = out_bpb // chunks
    tr = chunks * ROW_BLK

    def x_specs(c):
        specs = _resid_specs(srcs, d, tok_bpb, lambda i: (i // tpb) * tok_bpb + (i % tpb) * chunks + c + off)
        return specs if c == 0 else specs[-1:]

    row_spec = lambda w: pl.BlockSpec((tr, w), lambda i: (i, 0))
    full = lambda a: pl.BlockSpec(a.shape, lambda i: (0,) * a.ndim)
    mod_spec = lambda idx: pl.BlockSpec((1, SUBLANES, mod.shape[-1]), idx)
    return pl.pallas_call(
        functools.partial(_outproj_kernel, n_experts=n_experts, chunks=chunks, tiles_per_batch=tpb,
                          with_ctx=off == 0),
        grid=(n_out // tr,),
        in_specs=[s for c in range(chunks) for s in x_specs(c)] + [
            row_spec(GROUP_W), row_spec(GROUP_W), row_spec(GROUP_W), row_spec(GROUP_W),
            mod_spec(lambda i: (ctx_row, 0, 0)), mod_spec(lambda i: (i // tpb, 0, 0)),
            pl.BlockSpec((SUBLANES, d), lambda i: (0, 0)),
            full(w_bf16), full(w_router_t), full(b_router_t)],
        out_specs=[row_spec(d),
                   pl.BlockSpec((tr * TOKEN_SUB, LANES), lambda i: (i, 0)),
                   row_spec(LANES)],
        out_shape=[jax.ShapeDtypeStruct((n_out, d), F32),
                   jax.ShapeDtypeStruct((n_out * TOKEN_SUB, LANES), F32),
                   jax.ShapeDtypeStruct((n_out, LANES), F32)],
        compiler_params=_params("parallel"),
        name="out_proj_router",
    )(*srcs, *([srcs[-1]] * (chunks - 1)), *mixers, mod, mod, _rows8(gain), w_bf16, w_router_t, b_router_t)


def _store_token_tiles(ref, x, first_token=0):
    rows = x.shape[0]
    for j in range(TOKEN_SUB):
        ref[pl.ds(first_token * TOKEN_SUB + j, rows, stride=TOKEN_SUB), :] = x[:, j * LANES:(j + 1) * LANES]


def _load_token_tiles(ref, first_token=0, rows=None):
    rows = ref.shape[0] // TOKEN_SUB if rows is None else rows
    return jnp.concatenate([ref[pl.ds(first_token * TOKEN_SUB + j, rows, stride=TOKEN_SUB), :]
                            for j in range(TOKEN_SUB)], axis=-1)


def _token_rows(t):
    return pl.ds(pl.multiple_of(t * TOKEN_SUB, TOKEN_SUB), TOKEN_SUB)


def _token_dmas(n, make_copy):
    def start(r, carry):
        make_copy(r).start()
        return carry

    def wait(r, carry):
        make_copy(r).wait()
        return carry

    lax.fori_loop(0, n, start, 0, unroll=8)
    lax.fori_loop(0, n, wait, 0, unroll=8)


def _pack_bf16_pair(a, b):
    au = lax.bitcast_convert_type(a.astype(BF16).astype(F32), jnp.uint32)
    bu = lax.bitcast_convert_type(b.astype(BF16).astype(F32), jnp.uint32)
    return au | (bu >> 16)


def _unpack_bf16_pair(u):
    a = lax.bitcast_convert_type(u & jnp.uint32(0xFFFF0000), F32)
    b = lax.bitcast_convert_type(u << 16, F32)
    return a, b


def _moe_plan(route, n_buckets):
    n = route.shape[0]
    nt = n // ROW_BLK
    n_tiles = nt + n_buckets
    bucket = route[:, 0].astype(jnp.int32)
    onehot = (bucket[:, None] == jnp.arange(n_buckets, dtype=jnp.int32)[None, :]).astype(F32)
    onehot = onehot.reshape(nt, ROW_BLK, n_buckets)
    tile_counts = jnp.sum(onehot, axis=1)
    before = jnp.cumsum(tile_counts, axis=0) - tile_counts
    earlier = (jnp.arange(ROW_BLK)[:, None] > jnp.arange(ROW_BLK)[None, :]).astype(F32)
    rank = jnp.einsum('ij,tjb->tib', earlier, onehot) + before[:, None, :]
    counts = jnp.sum(tile_counts, axis=0)
    tiles = jnp.ceil(counts / ROW_BLK)
    tend = jnp.cumsum(tiles)
    tstart = tend - tiles
    pos = jnp.sum(onehot * (rank + tstart * ROW_BLK), axis=-1).astype(jnp.int32).reshape(n)
    tile_id = jnp.arange(n_tiles, dtype=F32)
    tile_valid = (tile_id < tend[-1]).astype(jnp.int32)
    tile_bucket = jnp.sum((tile_id[:, None] >= tend[None, :]).astype(jnp.int32), axis=1)
    last_bucket = jnp.max(jnp.where(tiles > 0, jnp.arange(n_buckets, dtype=jnp.int32), 0))
    tile_bucket = jnp.minimum(tile_bucket, last_bucket)
    group = tile_bucket // PAIRS_PER_GROUP
    pair = tile_bucket % PAIRS_PER_GROUP
    lo = (pair >= 3).astype(jnp.int32) + (pair >= 5).astype(jnp.int32)
    hi = jnp.where(pair < 3, pair + 1, jnp.where(pair < 5, pair - 1, 3))
    last_tile = jnp.maximum(tend - 1.0, 0.0).astype(jnp.int32)
    has_tile = (tiles > 0).astype(jnp.int32)
    used = tend[-1:].astype(jnp.int32)
    return (group * EXPERTS_PER_GROUP + lo, group * EXPERTS_PER_GROUP + hi, tile_valid, pos,
            (last_tile, has_tile, used))


def _moe_scatter_kernel(last_ref, has_ref, used_ref, pos_ref, h_ref, hs_out, zbuf, sem):
    tile_rows = zbuf.shape[0]
    n_tiles = hs_out.shape[0] // tile_rows

    @pl.when(pl.program_id(0) == 0)
    def _():
        zbuf[...] = jnp.zeros_like(zbuf)
        fill = lambda t: pltpu.make_async_copy(
            zbuf, hs_out.at[pl.ds(pl.multiple_of(t * tile_rows, tile_rows), tile_rows)], sem.at[1])

        def tail(op):
            def body(t, carry):
                op(fill(t))
                return carry
            lax.fori_loop(used_ref[0], n_tiles, body, 0)

        for b in range(last_ref.shape[0]):
            @pl.when(has_ref[b] == 1)
            def _():
                fill(last_ref[b]).start()
        tail(lambda cp: cp.start())
        for b in range(last_ref.shape[0]):
            @pl.when(has_ref[b] == 1)
            def _():
                fill(last_ref[b]).wait()
        tail(lambda cp: cp.wait())

    _token_dmas(h_ref.shape[0] // TOKEN_SUB,
                lambda r: pltpu.make_async_copy(h_ref.at[_token_rows(r)], hs_out.at[_token_rows(pos_ref[0, 0, r])],
                                                sem.at[0]))


def _moe_scatter(last_tile, has_tile, used, pos, h2, n_sorted, tm):
    n = h2.shape[0] // TOKEN_SUB
    grid_spec = pltpu.PrefetchScalarGridSpec(
        num_scalar_prefetch=3,
        grid=(n // tm,),
        in_specs=[pl.BlockSpec((1, 1, tm), lambda i, *_: (i, 0, 0), memory_space=pltpu.SMEM),
                  pl.BlockSpec((tm * TOKEN_SUB, LANES), lambda i, *_: (i, 0))],
        out_specs=pl.BlockSpec(memory_space=pl.ANY),
        scratch_shapes=[pltpu.VMEM((ROW_BLK * TOKEN_SUB, LANES), F32), pltpu.SemaphoreType.DMA((2,))],
    )
    return pl.pallas_call(
        _moe_scatter_kernel,
        grid_spec=grid_spec,
        out_shape=jax.ShapeDtypeStruct((n_sorted * TOKEN_SUB, LANES), F32),
        compiler_params=_params("arbitrary"),
        name="moe_scatter",
    )(last_tile, has_tile, used, pos.reshape(n // tm, 1, tm), h2)


def _moe_expert_kernel(lo_ref, hi_ref, valid_ref, h_ref, wg_lo, wu_lo, wd_lo, wg_hi, wu_hi, wd_hi, y_ref):
    i = pl.program_id(0)

    @pl.when(valid_ref[i] == 0)
    def _():
        y_ref[...] = jnp.zeros_like(y_ref)

    @pl.when(valid_ref[i] == 1)
    def _():
        h = _load_token_tiles(h_ref).astype(BF16)

        g_lo, u_lo = _dot(h, wg_lo[0]), _dot(h, wu_lo[0])
        g_hi, u_hi = _dot(h, wg_hi[0]), _dot(h, wu_hi[0])
        y_lo = _dot((_silu(g_lo) * u_lo).astype(BF16), wd_lo[0])
        y_hi = _dot((_silu(g_hi) * u_hi).astype(BF16), wd_hi[0])
        _store_token_tiles(y_ref, _pack_bf16_pair(y_lo, y_hi))


def _moe_experts(tile_lo, tile_hi, tile_valid, h_sorted, wg, wu, wd):
    n_tiles = tile_lo.shape[0]
    by_lo = lambda a: pl.BlockSpec((1,) + a.shape[1:], lambda i, lo, hi, v: (lo[i], 0, 0))
    by_hi = lambda a: pl.BlockSpec((1,) + a.shape[1:], lambda i, lo, hi, v: (hi[i], 0, 0))
    tile_spec = pl.BlockSpec((ROW_BLK * TOKEN_SUB, LANES), lambda i, lo, hi, v: (i, 0))
    grid_spec = pltpu.PrefetchScalarGridSpec(
        num_scalar_prefetch=3,
        grid=(n_tiles,),
        in_specs=[tile_spec, by_lo(wg), by_lo(wu), by_lo(wd), by_hi(wg), by_hi(wu), by_hi(wd)],
        out_specs=tile_spec,
    )
    return pl.pallas_call(
        _moe_expert_kernel,
        grid_spec=grid_spec,
        out_shape=jax.ShapeDtypeStruct((n_tiles * ROW_BLK * TOKEN_SUB, LANES), jnp.uint32),
        compiler_params=_params("parallel"),
        name="moe_experts",
    )(tile_lo, tile_hi, tile_valid, h_sorted, wg, wu, wd, wg, wu, wd)


def _moe_combine_kernel(pos_ref, pos_next_ref, y_hbm, rt_ref, x1_ref, modc_ref, modb_ref, o_ref, buf, sem,
                        *, ctx_len, tiles_per_batch):
    d = x1_ref.shape[-1]
    tm = x1_ref.shape[0]
    i = pl.program_id(0)
    half = i % 2

    def copies(idx_ref, h, op):
        def body(r, carry):
            op(pltpu.make_async_copy(y_hbm.at[_token_rows(idx_ref[0, 0, r])], buf.at[_token_rows(h * tm + r)],
                                     sem.at[h]))
            return carry
        lax.fori_loop(0, tm, body, 0, unroll=8)

    @pl.when(i == 0)
    def _():
        copies(pos_ref, 0, lambda cp: cp.start())

    @pl.when(i + 1 < pl.num_programs(0))
    def _():
        copies(pos_next_ref, 1 - half, lambda cp: cp.start())

    copies(pos_ref, half, lambda cp: cp.wait())
    y_lo, y_hi = _unpack_bf16_pair(_load_token_tiles(buf, half * tm, tm))
    rt = rt_ref[...]
    y = rt[:, 1:2] * y_lo + rt[:, 2:3] * y_hi
    gate = _tile_rows(modb_ref[0][:, 5 * d:6 * d], y.shape[0])
    if ctx_len:
        first = pl.program_id(0) % tiles_per_batch == 0
        row = lax.broadcasted_iota(jnp.int32, y.shape, 0)
        gate = jnp.where(first & (row < ctx_len), _tile_rows(modc_ref[0][:, 5 * d:6 * d], y.shape[0]), gate)
    o_ref[...] = x1_ref[...] + gate * y


def _moe_combine(pos, y_sorted, route, x1, mod, batch, ctx_len, ctx_row, with_ctx, tm):
    n, d = x1.shape
    tiles_per_batch = n // batch // tm
    n_tiles = n // tm
    row_spec = lambda w: pl.BlockSpec((tm, w), lambda i: (i, 0))
    pos = pos.reshape(n_tiles, 1, tm)
    return pl.pallas_call(
        functools.partial(_moe_combine_kernel, ctx_len=ctx_len if with_ctx else 0, tiles_per_batch=tiles_per_batch),
        grid=(n_tiles,),
        in_specs=[pl.BlockSpec((1, 1, tm), lambda i: (i, 0, 0), memory_space=pltpu.SMEM),
                  pl.BlockSpec((1, 1, tm), lambda i: (jnp.minimum(i + 1, n_tiles - 1), 0, 0),
                               memory_space=pltpu.SMEM),
                  pl.BlockSpec(memory_space=pl.ANY),
                  row_spec(LANES), row_spec(d),
                  pl.BlockSpec((1, SUBLANES, mod.shape[-1]), lambda i: (ctx_row, 0, 0)),
                  pl.BlockSpec((1, SUBLANES, mod.shape[-1]), lambda i: (i // tiles_per_batch, 0, 0))],
        out_specs=row_spec(d),
        out_shape=jax.ShapeDtypeStruct((n, d), F32),
        scratch_shapes=[pltpu.VMEM((2 * tm * TOKEN_SUB, LANES), jnp.uint32), pltpu.SemaphoreType.DMA((2,))],
        compiler_params=_params("arbitrary"),
        name="moe_combine",
    )(pos, pos, y_sorted, route, x1, mod, mod)


def kernel(x, c, ctx, c_ctx, ada_w, ada_b, norm_mix, norm_ffn, w_in, hgrn_lb_logits, hgrn_norm, conv_w,
           ret_decay_logit, ret_norm, q_norm, k_norm, w_out, router_group_w, router_group_b, router_expert_w,
           router_expert_b, expert_w_gate, expert_w_up, expert_w_down):
    batch, seq_len, d = x.shape
    ctx_len = ctx.shape[1]
    depth = ada_w.shape[0]
    n_experts = expert_w_gate.shape[1]
    n_groups = router_group_w.shape[-1]
    assert ctx_len == ROW_BLK and seq_len % ROW_BLK == 0 and batch + 1 <= MOD_ROWS
    assert n_experts + n_groups <= ROUTER_ROWS and n_experts == n_groups * EXPERTS_PER_GROUP
    assert d == TOKEN_SUB * LANES
    rows = ctx_len + seq_len
    bpb = rows // ROW_BLK
    ctx_row = batch

    cv = jnp.concatenate([c, c_ctx[None], jnp.zeros((MOD_ROWS - batch - 1, d), F32)], axis=0)
    mod_all = _modulation(cv, ada_w, ada_b)
    cos, sin_signed = _rope_tables(ctx_len, seq_len)
    if depth > 1:
        tokens = (ctx.reshape(batch * ctx_len, d), x.reshape(batch * seq_len, d))
    else:
        tokens = (jnp.concatenate([ctx, x], axis=1).reshape(batch * rows, d),)

    for layer in range(depth):
        with_ctx = layer < depth - 1
        out_rows = rows if with_ctx else seq_len
        mod = jnp.broadcast_to(mod_all[layer][:, None, :], (MOD_ROWS, SUBLANES, 6 * d))
        pa, pb, pc, pd = _inproj(tokens, mod, norm_mix[layer], w_in[layer].astype(BF16), bpb, ctx_row)
        mixers = (
            _hgrn(pa, hgrn_lb_logits, hgrn_norm[layer], layer, batch, out_rows),
            _conv(pb, conv_w[layer], batch, ctx_len, out_rows),
            _retention(pc, ret_decay_logit[layer], ret_norm[layer], batch, out_rows),
            _attention_mixer(pd, cos, sin_signed, q_norm[layer], k_norm[layer], batch, ctx_len, with_ctx),
        )
        pad = ROUTER_ROWS - n_experts - n_groups
        w_router_t = jnp.concatenate([router_expert_w[layer].T, router_group_w[layer].T, jnp.zeros((pad, d), F32)])
        b_router = jnp.concatenate([router_expert_b[layer], router_group_b[layer], jnp.zeros((pad,), F32)])
        b_router_t = jnp.broadcast_to(b_router[:, None], (ROUTER_ROWS, ROW_BLK))
        x1, h2, rt = _outproj(tokens, mixers, mod, norm_ffn[layer], w_out[layer].astype(BF16), w_router_t,
                              b_router_t, n_experts, batch, ctx_row)
        tile_lo, tile_hi, tile_valid, pos, fill_plan = _moe_plan(rt, n_groups * PAIRS_PER_GROUP)
        tm = max(t for t in range(ROW_BLK, 4 * ROW_BLK + 1, ROW_BLK) if out_rows % t == 0)
        h_sorted = _moe_scatter(*fill_plan, pos, h2, tile_lo.shape[0] * ROW_BLK, tm)
        y_sorted = _moe_experts(tile_lo, tile_hi, tile_valid, h_sorted, expert_w_gate[layer].astype(BF16),
                                expert_w_up[layer].astype(BF16), expert_w_down[layer].astype(BF16))
        tokens = (_moe_combine(pos, y_sorted, rt, x1, mod, batch, ctx_len, ctx_row, with_ctx, tm),)
    return tokens[0].reshape(batch, seq_len, d)
```

```python
import functools

import jax
import jax.numpy as jnp
from jax import lax
from jax.experimental import pallas as pl
from jax.experimental.pallas import tpu as pltpu

F32 = jnp.float32
BF16 = jnp.bfloat16
HIGHEST = lax.Precision.HIGHEST

HEAD_DIM = 64
GROUP_W = 256
N_HEADS = GROUP_W // HEAD_DIM
KV_W = 128
GRID_W = 64
ROPE_THETA = 10000.0
EXPERTS_PER_GROUP = 4
NORM_EPS = 1e-6
ROW_BLK = 256
SUB_BLK = 8
LEVEL_ROWS = 128
KEY_BLK = 256
LANES = 128
SUBLANES = 8
BF16_SUBLANES = 16
LOG2_E = 1.4426950408889634
MOD_ROWS = 24
ROUTER_ROWS = 32
ROUTE_ROWS = 8
PAIRS_PER_GROUP = EXPERTS_PER_GROUP * (EXPERTS_PER_GROUP - 1) // 2
TOKEN_SUB = 8
DMA_PRIORITIES = 2
VMEM_LIMIT_BYTES = 56 * 1024 * 1024
PA_W, PB_W, PC_W, PD_W = 5 * GROUP_W, 3 * GROUP_W, 4 * GROUP_W, GROUP_W + 2 * KV_W


def _params(*semantics):
    return pltpu.CompilerParams(dimension_semantics=semantics, vmem_limit_bytes=VMEM_LIMIT_BYTES)


def _dot(a, b):
    return jnp.dot(a, b, preferred_element_type=F32)


def _dot_nt(a, b):
    return lax.dot_general(a, b, (((1,), (1,)), ((), ())), preferred_element_type=F32)


def _dot_tn(a, b):
    return lax.dot_general(a, b, (((0,), (0,)), ((), ())), preferred_element_type=F32)


def _recip(x):
    return pl.reciprocal(x, approx=True)


def _sigmoid(x):
    return _recip(1.0 + jnp.exp(-x))


def _silu(x):
    return x * _sigmoid(x)


def _log_sigmoid(x):
    return jnp.minimum(x, 0.0) - jnp.log(1.0 + jnp.exp(-jnp.abs(x)))


def _split_bf16(x, terms):
    parts = []
    rem = x
    for i in range(terms):
        p = rem.astype(BF16)
        parts.append(p)
        if i + 1 < terms:
            rem = rem - p.astype(F32)
    return parts


def _head_ones(width):
    r = lax.broadcasted_iota(jnp.int32, (width, width), 0) // HEAD_DIM
    c = lax.broadcasted_iota(jnp.int32, (width, width), 1) // HEAD_DIM
    return r == c


def _head_masks(width):
    lane_head = lax.broadcasted_iota(jnp.int32, (1, width), 1) // HEAD_DIM
    return [lane_head == h for h in range(width // HEAD_DIM)]


def _head_expand(x, hmasks):
    zero = jnp.zeros_like(x)
    return jnp.concatenate([jnp.where(hm, x, zero) for hm in hmasks], axis=0).astype(BF16)


def _head_mean_sq(x, ones_bf16):
    hi, lo = _split_bf16(x * x, 2)
    return (_dot(hi, ones_bf16) + _dot(lo, ones_bf16)) * (1.0 / HEAD_DIM)


def _rows8(v):
    return jnp.broadcast_to(v.reshape(1, -1), (SUBLANES, v.size))


def _tile_rows(m8, rows):
    n = m8.shape[-1]
    return jnp.broadcast_to(m8[None], (rows // SUBLANES, SUBLANES, n)).reshape(rows, n)


def _head_rms_norm(x, gain8, ones_bf16):
    return x * lax.rsqrt(_head_mean_sq(x, ones_bf16) + NORM_EPS) * _tile_rows(gain8, x.shape[0])


def _norm_modulate(x, gain8, mod8, idx):
    rows, d = x.shape
    ms = jnp.mean(x * x, axis=-1, keepdims=True)
    y = x * lax.rsqrt(ms + NORM_EPS) * _tile_rows(gain8, rows)
    shift = _tile_rows(mod8[:, idx * d:(idx + 1) * d], rows)
    scale = _tile_rows(mod8[:, (idx + 1) * d:(idx + 2) * d], rows)
    return y * (1.0 + scale) + shift


def _scan_rows(j, nchunk):
    rf = pl.multiple_of(j * ROW_BLK, ROW_BLK)
    rb = pl.multiple_of(jnp.where(j == 0, 0, nchunk - j) * ROW_BLK, ROW_BLK)
    return rf, rb


def _mod_kernel(cv_ref, w_ref, b_ref, o_ref):
    s = _silu(cv_ref[...])
    o_ref[0] = jnp.dot(s, w_ref[0], precision=HIGHEST, preferred_element_type=F32) + b_ref[0]


def _modulation(cv, ada_w, ada_b):
    depth, d, n = ada_w.shape
    tn = n // 4
    return pl.pallas_call(
        _mod_kernel,
        grid=(depth, n // tn),
        in_specs=[pl.BlockSpec((MOD_ROWS, d), lambda l, j: (0, 0)),
                  pl.BlockSpec((1, d, tn), lambda l, j: (l, 0, j)),
                  pl.BlockSpec((1, 1, tn), lambda l, j: (l, 0, j))],
        out_specs=pl.BlockSpec((1, MOD_ROWS, tn), lambda l, j: (l, 0, j)),
        out_shape=jax.ShapeDtypeStruct((depth, MOD_ROWS, n), F32),
        compiler_params=_params("parallel", "parallel"),
        name="adaln_mod",
    )(cv, ada_w, ada_b.reshape(depth, 1, n))


def _inproj_kernel(*refs, blocks_per_batch):
    srcs, (mod_ref, gain_ref, w_ref, pa_ref, pb_ref, pc_ref, pd_ref) = refs[:-7], refs[-7:]
    x = srcs[-1][...]
    if len(srcs) == 2:
        x = jnp.where(pl.program_id(0) % blocks_per_batch == 0, srcs[0][...], x)
    h = _norm_modulate(x, gain_ref[...], mod_ref[0], 0)
    p = _dot(h.astype(BF16), w_ref[...])
    pa_ref[...] = p[:, 0:PA_W]
    pb_ref[...] = p[:, PA_W:PA_W + PB_W].astype(pb_ref.dtype)
    pc_ref[...] = p[:, PA_W + PB_W:PA_W + PB_W + PC_W].astype(pc_ref.dtype)
    pd_ref[...] = p[:, PA_W + PB_W + PC_W:].astype(pd_ref.dtype)


def _resid_specs(srcs, d, bpb, block_of):
    if len(srcs) == 1:
        return [pl.BlockSpec((ROW_BLK, d), lambda i: (block_of(i), 0))]
    return [pl.BlockSpec((ROW_BLK, d), lambda i: (block_of(i) // bpb, 0)),
            pl.BlockSpec((ROW_BLK, d), lambda i: ((block_of(i) // bpb) * (bpb - 1)
                                                  + jnp.maximum(block_of(i) % bpb - 1, 0), 0))]


def _inproj(srcs, mod, gain, w_bf16, blocks_per_batch, ctx_row):
    n = sum(s.shape[0] for s in srcs)
    d = srcs[0].shape[1]
    nblk = n // ROW_BLK

    def mod_idx(i):
        return (jnp.where(i % blocks_per_batch == 0, ctx_row, i // blocks_per_batch), 0, 0)

    outs = ((PA_W, F32), (PB_W, BF16), (PC_W, BF16), (PD_W, BF16))
    return pl.pallas_call(
        functools.partial(_inproj_kernel, blocks_per_batch=blocks_per_batch),
        grid=(nblk,),
        in_specs=_resid_specs(srcs, d, blocks_per_batch, lambda i: i) + [
                  pl.BlockSpec((1, SUBLANES, mod.shape[-1]), mod_idx),
                  pl.BlockSpec((SUBLANES, d), lambda i: (0, 0)),
                  pl.BlockSpec(w_bf16.shape, lambda i: (0, 0))],
        out_specs=[pl.BlockSpec((ROW_BLK, w), lambda i: (i, 0)) for w, _ in outs],
        out_shape=[jax.ShapeDtypeStruct((n, w), dt) for w, dt in outs],
        compiler_params=_params("parallel"),
        name="in_proj",
    )(*srcs, mod, _rows8(gain), w_bf16)


def _hgrn_prepare(blk, lb_row, z_col, anti, s_ref, consts):
    tril, triu, bmask, hmasks, level_masks = consts
    w = GROUP_W
    c_len = blk.shape[0]
    q = blk[:, 0:w]
    v = blk[:, w:2 * w]
    z = blk[:, z_col * w:(z_col + 1) * w]
    f = lb_row + (1.0 - lb_row) * _sigmoid(z)
    g = jnp.log(f) * LOG2_E
    k = 1.0 - f
    tri = triu if anti else tril
    c = sum(_dot(tri, part) for part in _split_bf16(g, 3))
    tot = c[0:1] if anti else c[c_len - 1:c_len]

    s_t = s_ref[...]
    o = _dot_nt((q * jnp.exp2(c)).astype(BF16), s_t.astype(BF16))
    k_end = (k * jnp.exp2(tot - c)).astype(BF16)
    s_ref[...] = s_t * jnp.exp2(tot) + jnp.where(bmask, _dot_tn(v.astype(BF16), k_end), 0.0)

    jobs = []
    m = c_len // 2
    while m >= SUB_BLK:
        nb = c_len // (2 * m)
        mid = m if anti else m - 1
        refs = [jnp.broadcast_to(c[b * 2 * m + mid:b * 2 * m + mid + 1], (2 * m, w)) for b in range(nb)]
        ref = refs[0] if nb == 1 else jnp.concatenate(refs, axis=0)
        e = jnp.exp2(-jnp.abs(c - ref))
        qt = q * e
        kt = k * e
        if 2 * m >= LEVEL_ROWS:
            for b in range(nb):
                first, second = b * 2 * m, b * 2 * m + m
                q0, k0 = (first, second) if anti else (second, first)
                jobs.append((m, q0, k0, m, _dot_nt(qt[q0:q0 + m].astype(BF16), _head_expand(kt[k0:k0 + m], hmasks))))
        else:
            for r0 in range(0, c_len, LEVEL_ROWS):
                a = _dot_nt(qt[r0:r0 + LEVEL_ROWS].astype(BF16), _head_expand(kt[r0:r0 + LEVEL_ROWS], hmasks))
                jobs.append((m, r0, r0, LEVEL_ROWS, jnp.where(level_masks[(m, anti)], a, 0.0)))
        m //= 2
    v_exp = {}
    pieces = {}
    for m, q0, k0, n, a in jobs:
        if (k0, n) not in v_exp:
            v_exp[(k0, n)] = _head_expand(v[k0:k0 + n], hmasks)
        pieces.setdefault(m, {})[q0] = _dot(a.astype(BF16), v_exp[(k0, n)])
    for m, level in pieces.items():
        n = next(iter(level.values())).shape[0]
        zero = jnp.zeros((n, w), F32)
        o = o + jnp.concatenate([level.get(r0, zero) for r0 in range(0, c_len, n)], axis=0)
    return o, (c, c - jnp.log(k) * LOG2_E, q, v)


def _hgrn_level_masks(c_len):
    masks = {}
    rg = LEVEL_ROWS
    m = LEVEL_ROWS // 4
    while m >= SUB_BLK:
        t = lax.broadcasted_iota(jnp.int32, (rg, N_HEADS * rg), 0)
        s = lax.broadcasted_iota(jnp.int32, (rg, N_HEADS * rg), 1) % rg
        same = (t // (2 * m)) == (s // (2 * m))
        t_late = (t // m) % 2 == 1
        s_late = (s // m) % 2 == 1
        masks[(m, False)] = same & t_late & ~s_late
        masks[(m, True)] = same & ~t_late & s_late
        m //= 2
    return masks


def _hgrn_diag_unit(cqkv, r, anti, ones_bf16):
    cs, cks, qs, vs = (a[r:r + SUB_BLK] for a in cqkv)
    row = lax.broadcasted_iota(jnp.int32, cs.shape, 0)
    prods = []
    for s in range(SUB_BLK):
        valid = (row <= s) if anti else (row >= s)
        prods.append(qs * jnp.exp2(jnp.where(valid, cs - cks[s:s + 1], -jnp.inf)))
    scores = _dot(jnp.concatenate(prods, axis=0).astype(BF16), ones_bf16)
    od = scores[0:SUB_BLK] * vs[0:1]
    for s in range(1, SUB_BLK):
        od = od + scores[s * SUB_BLK:(s + 1) * SUB_BLK] * vs[s:s + 1]
    return od


def _hgrn_kernel(p_ref, lbl_ref, gain_ref, o_ref, acc, s_f, s_b, *, layer, out_off):
    w = GROUP_W
    c_len = ROW_BLK
    nchunk = p_ref.shape[0] // c_len
    depth = lbl_ref.shape[0]

    logits = [lbl_ref[l] for l in range(depth)]
    mx = functools.reduce(jnp.maximum, logits)
    exps = [jnp.exp(l - mx) for l in logits]
    lb = sum(exps[1:layer + 1], jnp.zeros_like(mx)) * _recip(sum(exps))

    ri = lax.broadcasted_iota(jnp.int32, (c_len, c_len), 0)
    ci = lax.broadcasted_iota(jnp.int32, (c_len, c_len), 1)
    tril = (ci <= ri).astype(BF16)
    triu = (ci >= ri).astype(BF16)
    bmask = _head_ones(w)
    ones_bf16 = bmask.astype(BF16)
    consts = (tril, triu, bmask, _head_masks(w), _hgrn_level_masks(c_len))

    acc[...] = jnp.zeros_like(acc)
    s_f[...] = jnp.zeros_like(s_f)
    s_b[...] = jnp.zeros_like(s_b)

    def step(j, carry):
        rf, rb = _scan_rows(j, nchunk)
        of, cf = _hgrn_prepare(p_ref[pl.ds(rf, c_len), :], lb[0:1], 2, False, s_f, consts)
        ob, cb = _hgrn_prepare(p_ref[pl.ds(rb, c_len), :], lb[1:2], 3, True, s_b, consts)
        starts = range(0, c_len, SUB_BLK)
        od_f = jnp.concatenate([_hgrn_diag_unit(cf, r, False, ones_bf16) for r in starts], axis=0)
        od_b = jnp.concatenate([_hgrn_diag_unit(cb, r, True, ones_bf16) for r in starts], axis=0)
        acc[pl.ds(rf, c_len), :] += of + od_f
        acc[pl.ds(rb, c_len), :] += ob + od_b
        return carry

    lax.fori_loop(0, nchunk, step, 0)

    o = acc[out_off:, :]
    gate = p_ref[out_off:, 4 * w:5 * w]
    o_ref[...] = (_head_rms_norm(o, gain_ref[...], ones_bf16) * _silu(gate)).astype(o_ref.dtype)


def _hgrn(pa, lb_logits, gain, layer, batch, out_rows):
    n, width = pa.shape
    rows = n // batch
    w = GROUP_W
    scr = lambda r: pltpu.VMEM((r, w), F32)
    return pl.pallas_call(
        functools.partial(_hgrn_kernel, layer=layer, out_off=rows - out_rows),
        grid=(batch,),
        in_specs=[pl.BlockSpec((rows, width), lambda b: (b, 0)),
                  pl.BlockSpec(lb_logits.shape, lambda b: (0, 0, 0)),
                  pl.BlockSpec((SUBLANES, w), lambda b: (0, 0))],
        out_specs=pl.BlockSpec((out_rows, w), lambda b: (b, 0)),
        out_shape=jax.ShapeDtypeStruct((batch * out_rows, w), BF16),
        scratch_shapes=[scr(rows), scr(w), scr(w)],
        compiler_params=_params("parallel"),
        name="hgrn2_mixer",
    )(pa, lb_logits, _rows8(gain))


def _conv_kernel(p_ref, w_ref, o_ref, *, ctx_len, out_off):
    w = GROUP_W
    p = p_ref[...].astype(F32)
    u = p[:, w:2 * w] * p[:, 2 * w:3 * w]
    n = u.shape[0]
    row = lax.broadcasted_iota(jnp.int32, u.shape, 0)
    prev = jnp.where((row == 0) | (row == ctx_len), 0.0, pltpu.roll(u, 1, 0))
    nxt = jnp.where((row == ctx_len - 1) | (row == n - 1), 0.0, pltpu.roll(u, n - 1, 0))
    cw = w_ref[...]
    y = p[:, 0:w] * (cw[0:1] * prev + cw[1:2] * u + cw[2:3] * nxt)
    o_ref[...] = y[out_off:].astype(o_ref.dtype)


def _conv(pb, conv_w, batch, ctx_len, out_rows):
    n, width = pb.shape
    rows = n // batch
    return pl.pallas_call(
        functools.partial(_conv_kernel, ctx_len=ctx_len, out_off=rows - out_rows),
        grid=(batch,),
        in_specs=[pl.BlockSpec((rows, width), lambda b: (b, 0)),
                  pl.BlockSpec(conv_w.shape, lambda b: (0, 0))],
        out_specs=pl.BlockSpec((out_rows, GROUP_W), lambda b: (b, 0)),
        out_shape=jax.ShapeDtypeStruct((batch * out_rows, GROUP_W), BF16),
        compiler_params=_params("parallel"),
        name="conv_mixer",
    )(pb, conv_w)


def _ret_kernel(p_ref, dl_ref, gain_ref, o_ref, acc, s_f, s_b, d2_scr, *, out_off):
    w = GROUP_W
    c_len = ROW_BLK
    nchunk = p_ref.shape[0] // c_len
    hmasks = _head_masks(w)
    bmask = _head_ones(w)
    ones_bf16 = bmask.astype(BF16)

    lg = _log_sigmoid(dl_ref[...])
    lane_w = lambda r: sum(jnp.where(hm, lg[r:r + 1, h:h + 1], 0.0) for h, hm in enumerate(hmasks))
    lgf, lgb = lane_w(0), lane_w(1)
    t = lax.broadcasted_iota(jnp.int32, (c_len, w), 0).astype(F32)
    scale = HEAD_DIM ** -0.5
    qf_dec = jnp.exp(lgf * (t + 1.0)) * scale
    kf_dec = jnp.exp(lgf * (c_len - 1.0 - t))
    qb_dec = jnp.exp(lgb * (c_len - t)) * scale
    kb_dec = jnp.exp(lgb * t)
    tot_f = jnp.exp(lgf * float(c_len))
    tot_b = jnp.exp(lgb * float(c_len))

    @pl.when(pl.program_id(0) == 0)
    def _():
        col = lax.broadcasted_iota(jnp.int32, (c_len, N_HEADS * c_len), 1)
        row = lax.broadcasted_iota(jnp.int32, (c_len, N_HEADS * c_len), 0)
        col_head = lax.broadcasted_iota(jnp.int32, (1, N_HEADS * c_len), 1) // c_len
        col_lg = lambda r: sum(jnp.where(col_head == h, lg[r:r + 1, h:h + 1], 0.0) for h in range(N_HEADS))
        dist = (row - col % c_len).astype(F32)
        d2_scr[...] = (jnp.where(dist >= 0.0, jnp.exp(col_lg(0) * jnp.maximum(dist, 0.0)), 0.0)
                       + jnp.where(dist <= 0.0, jnp.exp(col_lg(1) * jnp.maximum(-dist, 0.0)), 0.0))

    acc[...] = jnp.zeros_like(acc)
    s_f[...] = jnp.zeros_like(s_f)
    s_b[...] = jnp.zeros_like(s_b)

    def step(j, carry):
        rf, rb = _scan_rows(j, nchunk)
        blk = p_ref[pl.ds(rf, c_len), :]
        q = blk[:, 0:w].astype(F32)
        k = blk[:, w:2 * w]
        v = blk[:, 2 * w:3 * w]
        blk = p_ref[pl.ds(rb, c_len), :]
        q2 = blk[:, 0:w].astype(F32)
        k2 = blk[:, w:2 * w].astype(F32)
        v2 = blk[:, 2 * w:3 * w]
        sf_t = s_f[...]
        sb_t = s_b[...]
        a = _dot_nt((q * scale).astype(BF16), _head_expand(k, hmasks))
        of = _dot_nt((q * qf_dec).astype(BF16), sf_t.astype(BF16))
        uf = _dot_tn(v, (k.astype(F32) * kf_dec).astype(BF16))
        ob = _dot_nt((q2 * qb_dec).astype(BF16), sb_t.astype(BF16))
        ub = _dot_tn(v2, (k2 * kb_dec).astype(BF16))
        s_f[...] = sf_t * tot_f + jnp.where(bmask, uf, 0.0)
        s_b[...] = sb_t * tot_b + jnp.where(bmask, ub, 0.0)
        acc[pl.ds(rf, c_len), :] += of + _dot((a * d2_scr[...]).astype(BF16), _head_expand(v, hmasks))
        acc[pl.ds(rb, c_len), :] += ob
        return carry

    lax.fori_loop(0, nchunk, step, 0)

    o = acc[out_off:, :]
    gate = p_ref[out_off:, 3 * w:4 * w].astype(F32)
    o_ref[...] = (_head_rms_norm(o, gain_ref[...], ones_bf16) * _silu(gate)).astype(o_ref.dtype)


def _retention(pc, decay_logit, gain, batch, out_rows):
    n, width = pc.shape
    rows = n // batch
    w = GROUP_W
    dl = jnp.zeros((8, LANES), F32).at[:decay_logit.shape[0], :decay_logit.shape[1]].set(decay_logit)
    return pl.pallas_call(
        functools.partial(_ret_kernel, out_off=rows - out_rows),
        grid=(batch,),
        in_specs=[pl.BlockSpec((rows, width), lambda b: (b, 0)),
                  pl.BlockSpec((8, LANES), lambda b: (0, 0)),
                  pl.BlockSpec((SUBLANES, w), lambda b: (0, 0))],
        out_specs=pl.BlockSpec((out_rows, w), lambda b: (b, 0)),
        out_shape=jax.ShapeDtypeStruct((batch * out_rows, w), BF16),
        scratch_shapes=[pltpu.VMEM((rows, w), F32), pltpu.VMEM((w, w), F32), pltpu.VMEM((w, w), F32),
                        pltpu.VMEM((ROW_BLK, N_HEADS * ROW_BLK), F32)],
        compiler_params=_params("arbitrary"),
        name="retention_mixer",
    )(pc, dl, _rows8(gain))


def _rope(x, cos, sin_signed):
    n = x.shape[-1]
    lane = lax.broadcasted_iota(jnp.int32, x.shape, 1)
    swapped = jnp.where(lane % 2 == 0, pltpu.roll(x, n - 1, 1), pltpu.roll(x, 1, 1))
    return x * cos + swapped * sin_signed


def _attn_kernel(q_ref, k_ref, v_ref, cos_ref, sin_ref, qn_ref, kn_ref, o_ref, k_buf, vt_buf, s_scr, *, qb_off):
    j = pl.program_id(1)
    nkb = k_ref.shape[0] // KEY_BLK

    @pl.when(j == 0)
    def _():
        ones_kv = _head_ones(KV_W).astype(BF16)
        kn = _head_rms_norm(k_ref[...].astype(F32), kn_ref[...], ones_kv)
        key_rows = k_ref.shape[0]
        k_buf[...] = _rope(kn, cos_ref[0:key_rows, 0:KV_W], sin_ref[0:key_rows, 0:KV_W]).astype(BF16)
        vt_buf[...] = v_ref[...].astype(F32).T.astype(BF16)

    qb = j + qb_off
    r0 = pl.multiple_of(qb * ROW_BLK, ROW_BLK)
    ones_q = _head_ones(GROUP_W).astype(BF16)
    qn = _head_rms_norm(q_ref[...].astype(F32), qn_ref[...], ones_q)
    qr = _rope(qn, cos_ref[pl.ds(r0, ROW_BLK), :], sin_ref[pl.ds(r0, ROW_BLK), :])
    qr = (qr * (HEAD_DIM ** -0.5 * LOG2_E)).astype(BF16)
    group = GROUP_W // KV_W
    ones_rows = jnp.ones((BF16_SUBLANES, KEY_BLK), BF16)
    n_kv = KV_W // HEAD_DIM
    ksl = lambda kv: slice(kv * HEAD_DIM, (kv + 1) * HEAD_DIM)
    q2 = [jnp.concatenate([qr[:, (kv * group + g) * HEAD_DIM:(kv * group + g + 1) * HEAD_DIM]
                           for g in range(group)], axis=0) for kv in range(n_kv)]

    def scores(kv, kb, m):
        s = _dot_nt(k_buf[kb * KEY_BLK:(kb + 1) * KEY_BLK, ksl(kv)], q2[kv])
        s_scr[kv, kb] = s
        bm = jnp.max(s, axis=0, keepdims=True)
        return bm if m is None else jnp.maximum(m, bm)

    def values(kv, kb, m, acc):
        p = jnp.exp2(s_scr[kv, kb] - m).astype(BF16)
        vt = jnp.concatenate([vt_buf[ksl(kv), kb * KEY_BLK:(kb + 1) * KEY_BLK], ones_rows], axis=0)
        return acc + _dot(vt, p)

    zero = jnp.zeros((HEAD_DIM + BF16_SUBLANES, group * ROW_BLK), F32)
    m_cur = None
    for kb in range(nkb):
        m_cur = scores(0, kb, m_cur)
    outs = []
    for kv in range(n_kv):
        m_next, acc = None, zero
        for kb in range(nkb):
            if kv + 1 < n_kv:
                m_next = scores(kv + 1, kb, m_next)
            acc = values(kv, kb, m_cur, acc)
        outs.append(acc[0:HEAD_DIM] * _recip(acc[HEAD_DIM:HEAD_DIM + 1]))
        m_cur = m_next
    o_t = jnp.concatenate(outs, axis=0).T
    for kv in range(n_kv):
        for g in range(group):
            h = kv * group + g
            o_ref[:, h * HEAD_DIM:(h + 1) * HEAD_DIM] = (
                o_t[g * ROW_BLK:(g + 1) * ROW_BLK, kv * HEAD_DIM:(kv + 1) * HEAD_DIM].astype(o_ref.dtype))


def _attention(pd, cos, sin_signed, q_norm, k_norm, batch, q_first, n_q, key_blks):
    n, width = pd.shape
    rows = n // batch
    bpb = rows // ROW_BLK
    assert bpb % key_blks == 0
    key_rows = key_blks * ROW_BLK
    kcol = GROUP_W // KV_W
    tile = lambda g, reps: _rows8(jnp.tile(g, reps))
    kv_spec = lambda col: pl.BlockSpec((key_rows, KV_W), lambda b, j: (b * (bpb // key_blks), col))
    return pl.pallas_call(
        functools.partial(_attn_kernel, qb_off=q_first),
        grid=(batch, n_q),
        in_specs=[pl.BlockSpec((ROW_BLK, GROUP_W), lambda b, j: (b * bpb + j + q_first, 0)),
                  kv_spec(kcol), kv_spec(kcol + 1),
                  pl.BlockSpec((rows, GROUP_W), lambda b, j: (0, 0)),
                  pl.BlockSpec((rows, GROUP_W), lambda b, j: (0, 0)),
                  pl.BlockSpec((SUBLANES, GROUP_W), lambda b, j: (0, 0)),
                  pl.BlockSpec((SUBLANES, KV_W), lambda b, j: (0, 0))],
        out_specs=pl.BlockSpec((ROW_BLK, GROUP_W), lambda b, j: (b * n_q + j, 0)),
        out_shape=jax.ShapeDtypeStruct((batch * n_q * ROW_BLK, GROUP_W), BF16),
        scratch_shapes=[pltpu.VMEM((key_rows, KV_W), BF16), pltpu.VMEM((KV_W, key_rows), BF16),
                        pltpu.VMEM((KV_W // HEAD_DIM, key_rows // KEY_BLK, KEY_BLK, (GROUP_W // KV_W) * ROW_BLK),
                                   F32)],
        compiler_params=_params("parallel", "arbitrary"),
        name="attention_mixer",
    )(pd, pd, pd, cos, sin_signed, tile(q_norm, N_HEADS), tile(k_norm, KV_W // HEAD_DIM))


def _attention_mixer(pd, cos, sin_signed, q_norm, k_norm, batch, ctx_len, with_ctx):
    rows = pd.shape[0] // batch
    bpb = rows // ROW_BLK
    cb = ctx_len // ROW_BLK
    gx = _attention(pd, cos, sin_signed, q_norm, k_norm, batch, cb, bpb - cb, bpb)
    if not with_ctx:
        return gx
    gc = _attention(pd, cos, sin_signed, q_norm, k_norm, batch, 0, cb, cb)
    return jnp.concatenate([gc.reshape(batch, ctx_len, GROUP_W), gx.reshape(batch, rows - ctx_len, GROUP_W)],
                           axis=1).reshape(batch * rows, GROUP_W)


def _rope_tables(ctx_len, seq_len):
    t = jnp.arange(seq_len)
    row = (t // GRID_W).astype(F32)
    col = (t % GRID_W).astype(F32)
    n_freq = HEAD_DIM // 4
    inv_freq = ROPE_THETA ** (-jnp.arange(n_freq, dtype=F32) / n_freq)
    ang = jnp.concatenate([row[:, None] * inv_freq, col[:, None] * inv_freq], axis=-1)
    cos = jnp.repeat(jnp.cos(ang), 2, axis=-1)
    sin = jnp.stack([-jnp.sin(ang), jnp.sin(ang)], axis=-1).reshape(seq_len, HEAD_DIM)
    cos = jnp.concatenate([jnp.ones((ctx_len, HEAD_DIM), F32), cos], axis=0)
    sin = jnp.concatenate([jnp.zeros((ctx_len, HEAD_DIM), F32), sin], axis=0)
    return jnp.tile(cos, (1, N_HEADS)), jnp.tile(sin, (1, N_HEADS))


def _route(lt, n_experts):
    n_groups = n_experts // EXPERTS_PER_GROUP
    first = lambda x, hit, n: jnp.min(jnp.where(hit, x, float(n)), axis=0, keepdims=True)

    gl = lt[n_experts:n_experts + n_groups]
    gexp = jnp.exp(gl - jnp.max(gl, axis=0, keepdims=True))
    gprob = gexp * _recip(jnp.sum(gexp, axis=0, keepdims=True))
    group_p = jnp.max(gprob, axis=0, keepdims=True)
    grow = lax.broadcasted_iota(jnp.int32, gl.shape, 0).astype(F32)
    gidx = first(grow, gprob == group_p, n_groups)

    el = sum(jnp.where(gidx == float(g), lt[g * EXPERTS_PER_GROUP:(g + 1) * EXPERTS_PER_GROUP], 0.0)
             for g in range(n_groups))
    eexp = jnp.exp(el - jnp.max(el, axis=0, keepdims=True))
    eprob = eexp * _recip(jnp.sum(eexp, axis=0, keepdims=True))
    erow = lax.broadcasted_iota(jnp.int32, el.shape, 0).astype(F32)
    p1 = jnp.max(eprob, axis=0, keepdims=True)
    i1 = first(erow, eprob == p1, EXPERTS_PER_GROUP)
    rest = erow != i1
    p2 = jnp.max(jnp.where(rest, eprob, -1.0), axis=0, keepdims=True)
    i2 = first(erow, rest & (eprob == p2), EXPERTS_PER_GROUP)
    scale = group_p * _recip(p1 + p2)
    lo = jnp.minimum(i1, i2)
    hi = jnp.maximum(i1, i2)
    w_lo = jnp.where(i1 < i2, p1, p2) * scale
    w_hi = jnp.where(i1 < i2, p2, p1) * scale
    pair = lo * (2.0 * EXPERTS_PER_GROUP - 1.0 - lo) * 0.5 + (hi - lo - 1.0)
    bucket = gidx * float(PAIRS_PER_GROUP) + pair
    return jnp.concatenate([bucket, w_lo, w_hi, jnp.zeros((LANES - 3, lt.shape[1]), F32)], axis=0)


def _outproj_kernel(*refs, n_experts, chunks, tiles_per_batch, with_ctx):
    n_fixed = 13
    srcs = refs[:-n_fixed]
    (a_ref, b_ref, c_ref, d_ref, modc_ref, modb_ref, gain_ref, w_ref, wr_ref, br_ref,
     x1_ref, h2_ref, rt_ref) = refs[-n_fixed:]
    d = x1_ref.shape[-1]
    rows = [slice(c * ROW_BLK, (c + 1) * ROW_BLK) for c in range(chunks)]
    mods = [modb_ref[0]] * chunks
    first_tile = pl.program_id(0) % tiles_per_batch == 0
    if with_ctx:
        mods[0] = jnp.where(first_tile, modc_ref[0], mods[0])
    xs = [r[...] for r in srcs[len(srcs) - chunks:]]
    if len(srcs) > chunks:
        xs[0] = jnp.where(first_tile, srcs[0][...], xs[0])
    projs = [_dot(jnp.concatenate([a_ref[rs, :], b_ref[rs, :], c_ref[rs, :], d_ref[rs, :]], axis=1), w_ref[...])
             for rs in rows]
    h2s = []
    for c, rs in enumerate(rows):
        x1 = xs[c] + _tile_rows(mods[c][:, 2 * d:3 * d], ROW_BLK) * projs[c]
        x1_ref[rs, :] = x1
        h2s.append(_norm_modulate(x1, gain_ref[...], mods[c], 3))
        _store_token_tiles(h2_ref, h2s[c], c * ROW_BLK)
    lts = [lax.dot_general(wr_ref[...], h2, (((1,), (1,)), ((), ())), precision=HIGHEST,
                           preferred_element_type=F32) + br_ref[...] for h2 in h2s]
    for rs, lt in zip(rows, lts):
        rt_ref[rs, :] = _route(lt, n_experts).T


def _outproj(srcs, mixers, mod, gain, w_bf16, w_router_t, b_router_t, n_experts, batch, ctx_row):
    d = srcs[0].shape[1]
    n_out = mixers[0].shape[0]
    out_bpb = n_out // ROW_BLK // batch
    tok_bpb = sum(s.shape[0] for s in srcs) // ROW_BLK // batch
    off = tok_bpb - out_bpb
    assert len(srcs) == 1 or off == 0
    chunks = max(c for c in (4, 3, 2, 1) if out_bpb % c == 0)
    tpb = out_bpb // chunks
    tr = chunks * ROW_BLK

    def x_specs(c):
        specs = _resid_specs(srcs, d, tok_bpb, lambda i: (i // tpb) * tok_bpb + (i % tpb) * chunks + c + off)
        return specs if c == 0 else specs[-1:]

    row_spec = lambda w: pl.BlockSpec((tr, w), lambda i: (i, 0))
    full = lambda a: pl.BlockSpec(a.shape, lambda i: (0,) * a.ndim)
    mod_spec = lambda idx: pl.BlockSpec((1, SUBLANES, mod.shape[-1]), idx)
    return pl.pallas_call(
        functools.partial(_outproj_kernel, n_experts=n_experts, chunks=chunks, tiles_per_batch=tpb,
                          with_ctx=off == 0),
        grid=(n_out // tr,),
        in_specs=[s for c in range(chunks) for s in x_specs(c)] + [
            row_spec(GROUP_W), row_spec(GROUP_W), row_spec(GROUP_W), row_spec(GROUP_W),
            mod_spec(lambda i: (ctx_row, 0, 0)), mod_spec(lambda i: (i // tpb, 0, 0)),
            pl.BlockSpec((SUBLANES, d), lambda i: (0, 0)),
            full(w_bf16), full(w_router_t), full(b_router_t)],
        out_specs=[row_spec(d),
                   pl.BlockSpec((tr * TOKEN_SUB, LANES), lambda i: (i, 0)),
                   row_spec(LANES)],
        out_shape=[jax.ShapeDtypeStruct((n_out, d), F32),
                   jax.ShapeDtypeStruct((n_out * TOKEN_SUB, LANES), F32),
                   jax.ShapeDtypeStruct((n_out, LANES), F32)],
        compiler_params=_params("parallel"),
        name="out_proj_router",
    )(*srcs, *([srcs[-1]] * (chunks - 1)), *mixers, mod, mod, _rows8(gain), w_bf16, w_router_t, b_router_t)


def _store_token_tiles(ref, x, first_token=0):
    rows = x.shape[0]
    for j in range(TOKEN_SUB):
        ref[pl.ds(first_token * TOKEN_SUB + j, rows, stride=TOKEN_SUB), :] = x[:, j * LANES:(j + 1) * LANES]


def _load_token_tiles(ref, first_token=0, rows=None):
    rows = ref.shape[0] // TOKEN_SUB if rows is None else rows
    return jnp.concatenate([ref[pl.ds(first_token * TOKEN_SUB + j, rows, stride=TOKEN_SUB), :]
                            for j in range(TOKEN_SUB)], axis=-1)


def _token_rows(t):
    return pl.ds(pl.multiple_of(t * TOKEN_SUB, TOKEN_SUB), TOKEN_SUB)


def _start_token_dmas(n, make_copy):
    def start(r2, carry):
        for p in range(DMA_PRIORITIES):
            make_copy(r2 * DMA_PRIORITIES + p).start(priority=p)
        return carry

    lax.fori_loop(0, n // DMA_PRIORITIES, start, 0, unroll=4)


def _wait_token_dmas(n, make_copy):
    def wait(r, carry):
        make_copy(r).wait()
        return carry

    lax.fori_loop(0, n, wait, 0, unroll=8)


def _token_dmas(n, make_copy):
    _start_token_dmas(n, make_copy)
    _wait_token_dmas(n, make_copy)


def _pack_bf16_pair(a, b):
    au = lax.bitcast_convert_type(a.astype(BF16).astype(F32), jnp.uint32)
    bu = lax.bitcast_convert_type(b.astype(BF16).astype(F32), jnp.uint32)
    return au | (bu >> 16)


def _unpack_bf16_pair(u):
    a = lax.bitcast_convert_type(u & jnp.uint32(0xFFFF0000), F32)
    b = lax.bitcast_convert_type(u << 16, F32)
    return a, b


def _moe_plan(route, n_buckets):
    n = route.shape[0]
    nt = n // ROW_BLK
    n_tiles = nt + n_buckets
    bucket = route[:, 0].astype(jnp.int32)
    onehot = (bucket[:, None] == jnp.arange(n_buckets, dtype=jnp.int32)[None, :]).astype(F32)
    onehot = onehot.reshape(nt, ROW_BLK, n_buckets)
    tile_counts = jnp.sum(onehot, axis=1)
    before = jnp.cumsum(tile_counts, axis=0) - tile_counts
    earlier = (jnp.arange(ROW_BLK)[:, None] > jnp.arange(ROW_BLK)[None, :]).astype(F32)
    rank = jnp.einsum('ij,tjb->tib', earlier, onehot) + before[:, None, :]
    counts = jnp.sum(tile_counts, axis=0)
    tiles = jnp.ceil(counts / ROW_BLK)
    tend = jnp.cumsum(tiles)
    tstart = tend - tiles
    pos = jnp.sum(onehot * (rank + tstart * ROW_BLK), axis=-1).astype(jnp.int32).reshape(n)
    tile_id = jnp.arange(n_tiles, dtype=F32)
    tile_valid = (tile_id < tend[-1]).astype(jnp.int32)
    tile_bucket = jnp.sum((tile_id[:, None] >= tend[None, :]).astype(jnp.int32), axis=1)
    last_bucket = jnp.max(jnp.where(tiles > 0, jnp.arange(n_buckets, dtype=jnp.int32), 0))
    tile_bucket = jnp.minimum(tile_bucket, last_bucket)
    group = tile_bucket // PAIRS_PER_GROUP
    pair = tile_bucket % PAIRS_PER_GROUP
    lo = (pair >= 3).astype(jnp.int32) + (pair >= 5).astype(jnp.int32)
    hi = jnp.where(pair < 3, pair + 1, jnp.where(pair < 5, pair - 1, 3))
    last_tile = jnp.maximum(tend - 1.0, 0.0).astype(jnp.int32)
    has_tile = (tiles > 0).astype(jnp.int32)
    used = tend[-1:].astype(jnp.int32)
    return (group * EXPERTS_PER_GROUP + lo, group * EXPERTS_PER_GROUP + hi, tile_valid, pos,
            (last_tile, has_tile, used))


def _moe_scatter_kernel(last_ref, has_ref, used_ref, pos_ref, h_ref, hs_out, zbuf, sem):
    tile_rows = zbuf.shape[0]
    n_tiles = hs_out.shape[0] // tile_rows

    @pl.when(pl.program_id(0) == 0)
    def _():
        zbuf[...] = jnp.zeros_like(zbuf)
        fill = lambda t: pltpu.make_async_copy(
            zbuf, hs_out.at[pl.ds(pl.multiple_of(t * tile_rows, tile_rows), tile_rows)], sem.at[1])

        def tail(op):
            def body(t, carry):
                op(fill(t))
                return carry
            lax.fori_loop(used_ref[0], n_tiles, body, 0)

        for b in range(last_ref.shape[0]):
            @pl.when(has_ref[b] == 1)
            def _():
                fill(last_ref[b]).start()
        tail(lambda cp: cp.start())
        for b in range(last_ref.shape[0]):
            @pl.when(has_ref[b] == 1)
            def _():
                fill(last_ref[b]).wait()
        tail(lambda cp: cp.wait())

    _token_dmas(h_ref.shape[0] // TOKEN_SUB,
                lambda r: pltpu.make_async_copy(h_ref.at[_token_rows(r)], hs_out.at[_token_rows(pos_ref[0, 0, r])],
                                                sem.at[0]))


def _moe_scatter(last_tile, has_tile, used, pos, h2, n_sorted, tm):
    n = h2.shape[0] // TOKEN_SUB
    grid_spec = pltpu.PrefetchScalarGridSpec(
        num_scalar_prefetch=3,
        grid=(n // tm,),
        in_specs=[pl.BlockSpec((1, 1, tm), lambda i, *_: (i, 0, 0), memory_space=pltpu.SMEM),
                  pl.BlockSpec((tm * TOKEN_SUB, LANES), lambda i, *_: (i, 0))],
        out_specs=pl.BlockSpec(memory_space=pl.ANY),
        scratch_shapes=[pltpu.VMEM((ROW_BLK * TOKEN_SUB, LANES), F32), pltpu.SemaphoreType.DMA((2,))],
    )
    return pl.pallas_call(
        _moe_scatter_kernel,
        grid_spec=grid_spec,
        out_shape=jax.ShapeDtypeStruct((n_sorted * TOKEN_SUB, LANES), F32),
        compiler_params=_params("arbitrary"),
        name="moe_scatter",
    )(last_tile, has_tile, used, pos.reshape(n // tm, 1, tm), h2)


def _moe_expert_kernel(lo_ref, hi_ref, valid_ref, h_ref, wg_lo, wu_lo, wd_lo, wg_hi, wu_hi, wd_hi, y_ref):
    i = pl.program_id(0)

    @pl.when(valid_ref[i] == 0)
    def _():
        y_ref[...] = jnp.zeros_like(y_ref)

    @pl.when(valid_ref[i] == 1)
    def _():
        h = _load_token_tiles(h_ref).astype(BF16)

        g_lo, u_lo = _dot(h, wg_lo[0]), _dot(h, wu_lo[0])
        g_hi, u_hi = _dot(h, wg_hi[0]), _dot(h, wu_hi[0])
        y_lo = _dot((_silu(g_lo) * u_lo).astype(BF16), wd_lo[0])
        y_hi = _dot((_silu(g_hi) * u_hi).astype(BF16), wd_hi[0])
        _store_token_tiles(y_ref, _pack_bf16_pair(y_lo, y_hi))


def _moe_experts(tile_lo, tile_hi, tile_valid, h_sorted, wg, wu, wd):
    n_tiles = tile_lo.shape[0]
    by_lo = lambda a: pl.BlockSpec((1,) + a.shape[1:], lambda i, lo, hi, v: (lo[i], 0, 0))
    by_hi = lambda a: pl.BlockSpec((1,) + a.shape[1:], lambda i, lo, hi, v: (hi[i], 0, 0))
    tile_spec = pl.BlockSpec((ROW_BLK * TOKEN_SUB, LANES), lambda i, lo, hi, v: (i, 0))
    grid_spec = pltpu.PrefetchScalarGridSpec(
        num_scalar_prefetch=3,
        grid=(n_tiles,),
        in_specs=[tile_spec, by_lo(wg), by_lo(wu), by_lo(wd), by_hi(wg), by_hi(wu), by_hi(wd)],
        out_specs=tile_spec,
    )
    return pl.pallas_call(
        _moe_expert_kernel,
        grid_spec=grid_spec,
        out_shape=jax.ShapeDtypeStruct((n_tiles * ROW_BLK * TOKEN_SUB, LANES), jnp.uint32),
        compiler_params=_params("parallel"),
        name="moe_experts",
    )(tile_lo, tile_hi, tile_valid, h_sorted, wg, wu, wd, wg, wu, wd)


def _moe_combine_kernel(pos_ref, pos_next_ref, y_hbm, rt_ref, x1_ref, modc_ref, modb_ref, o_ref, buf, sem,
                        *, ctx_len, tiles_per_batch):
    d = x1_ref.shape[-1]
    tm = x1_ref.shape[0]
    i = pl.program_id(0)
    half = i % 2

    def copy(idx_ref, h):
        return lambda r: pltpu.make_async_copy(y_hbm.at[_token_rows(idx_ref[0, 0, r])],
                                               buf.at[_token_rows(h * tm + r)], sem.at[h])

    @pl.when(i == 0)
    def _():
        _start_token_dmas(tm, copy(pos_ref, 0))

    @pl.when(i + 1 < pl.num_programs(0))
    def _():
        _start_token_dmas(tm, copy(pos_next_ref, 1 - half))

    _wait_token_dmas(tm, copy(pos_ref, half))
    y_lo, y_hi = _unpack_bf16_pair(_load_token_tiles(buf, half * tm, tm))
    rt = rt_ref[...]
    y = rt[:, 1:2] * y_lo + rt[:, 2:3] * y_hi
    gate = _tile_rows(modb_ref[0][:, 5 * d:6 * d], y.shape[0])
    if ctx_len:
        first = pl.program_id(0) % tiles_per_batch == 0
        row = lax.broadcasted_iota(jnp.int32, y.shape, 0)
        gate = jnp.where(first & (row < ctx_len), _tile_rows(modc_ref[0][:, 5 * d:6 * d], y.shape[0]), gate)
    o_ref[...] = x1_ref[...] + gate * y


def _moe_combine(pos, y_sorted, route, x1, mod, batch, ctx_len, ctx_row, with_ctx, tm):
    n, d = x1.shape
    tiles_per_batch = n // batch // tm
    n_tiles = n // tm
    row_spec = lambda w: pl.BlockSpec((tm, w), lambda i: (i, 0))
    pos = pos.reshape(n_tiles, 1, tm)
    return pl.pallas_call(
        functools.partial(_moe_combine_kernel, ctx_len=ctx_len if with_ctx else 0, tiles_per_batch=tiles_per_batch),
        grid=(n_tiles,),
        in_specs=[pl.BlockSpec((1, 1, tm), lambda i: (i, 0, 0), memory_space=pltpu.SMEM),
                  pl.BlockSpec((1, 1, tm), lambda i: (jnp.minimum(i + 1, n_tiles - 1), 0, 0),
                               memory_space=pltpu.SMEM),
                  pl.BlockSpec(memory_space=pl.ANY),
                  row_spec(LANES), row_spec(d),
                  pl.BlockSpec((1, SUBLANES, mod.shape[-1]), lambda i: (ctx_row, 0, 0)),
                  pl.BlockSpec((1, SUBLANES, mod.shape[-1]), lambda i: (i // tiles_per_batch, 0, 0))],
        out_specs=row_spec(d),
        out_shape=jax.ShapeDtypeStruct((n, d), F32),
        scratch_shapes=[pltpu.VMEM((2 * tm * TOKEN_SUB, LANES), jnp.uint32), pltpu.SemaphoreType.DMA((2,))],
        compiler_params=_params("arbitrary"),
        name="moe_combine",
    )(pos, pos, y_sorted, route, x1, mod, mod)


def kernel(x, c, ctx, c_ctx, ada_w, ada_b, norm_mix, norm_ffn, w_in, hgrn_lb_logits, hgrn_norm, conv_w,
           ret_decay_logit, ret_norm, q_norm, k_norm, w_out, router_group_w, router_group_b, router_expert_w,
           router_expert_b, expert_w_gate, expert_w_up, expert_w_down):
    batch, seq_len, d = x.shape
    ctx_len = ctx.shape[1]
    depth = ada_w.shape[0]
    n_experts = expert_w_gate.shape[1]
    n_groups = router_group_w.shape[-1]
    assert ctx_len == ROW_BLK and seq_len % ROW_BLK == 0 and batch + 1 <= MOD_ROWS
    assert n_experts + n_groups <= ROUTER_ROWS and n_experts == n_groups * EXPERTS_PER_GROUP
    assert d == TOKEN_SUB * LANES
    rows = ctx_len + seq_len
    bpb = rows // ROW_BLK
    ctx_row = batch

    cv = jnp.concatenate([c, c_ctx[None], jnp.zeros((MOD_ROWS - batch - 1, d), F32)], axis=0)
    mod_all = _modulation(cv, ada_w, ada_b)
    cos, sin_signed = _rope_tables(ctx_len, seq_len)
    if depth > 1:
        tokens = (ctx.reshape(batch * ctx_len, d), x.reshape(batch * seq_len, d))
    else:
        tokens = (jnp.concatenate([ctx, x], axis=1).reshape(batch * rows, d),)

    for layer in range(depth):
        with_ctx = layer < depth - 1
        out_rows = rows if with_ctx else seq_len
        mod = jnp.broadcast_to(mod_all[layer][:, None, :], (MOD_ROWS, SUBLANES, 6 * d))
        pa, pb, pc, pd = _inproj(tokens, mod, norm_mix[layer], w_in[layer].astype(BF16), bpb, ctx_row)
        mixers = (
            _hgrn(pa, hgrn_lb_logits, hgrn_norm[layer], layer, batch, out_rows),
            _conv(pb, conv_w[layer], batch, ctx_len, out_rows),
            _retention(pc, ret_decay_logit[layer], ret_norm[layer], batch, out_rows),
            _attention_mixer(pd, cos, sin_signed, q_norm[layer], k_norm[layer], batch, ctx_len, with_ctx),
        )
        pad = ROUTER_ROWS - n_experts - n_groups
        w_router_t = jnp.concatenate([router_expert_w[layer].T, router_group_w[layer].T, jnp.zeros((pad, d), F32)])
        b_router = jnp.concatenate([router_expert_b[layer], router_group_b[layer], jnp.zeros((pad,), F32)])
        b_router_t = jnp.broadcast_to(b_router[:, None], (ROUTER_ROWS, ROW_BLK))
        x1, h2, rt = _outproj(tokens, mixers, mod, norm_ffn[layer], w_out[layer].astype(BF16), w_router_t,
                              b_router_t, n_experts, batch, ctx_row)
        tile_lo, tile_hi, tile_valid, pos, fill_plan = _moe_plan(rt, n_groups * PAIRS_PER_GROUP)
        tm = max(t for t in range(ROW_BLK, 4 * ROW_BLK + 1, ROW_BLK) if out_rows % t == 0)
        h_sorted = _moe_scatter(*fill_plan, pos, h2, tile_lo.shape[0] * ROW_BLK, tm)
        y_sorted = _moe_experts(tile_lo, tile_hi, tile_valid, h_sorted, expert_w_gate[layer].astype(BF16),
                                expert_w_up[layer].astype(BF16), expert_w_down[layer].astype(BF16))
        tokens = (_moe_combine(pos, y_sorted, rt, x1, mod, batch, ctx_len, ctx_row, with_ctx, tm),)
    return tokens[0].reshape(batch, seq_len, d)
```
